```python
import math
import jax, jax.numpy as jnp
from jax import lax
import numpy as np

D_MODEL = 2048
BATCH = 2
SEQ = 4096
DEPTH = 2

D_SSM = D_MODEL // 4
D_ATTN = D_MODEL // 2
D_CONV = D_MODEL // 4
D_MIX = D_SSM + D_ATTN + D_CONV
SSM_GROUP = 16
SSM_GROUPS = D_SSM // SSM_GROUP
SSM_STATE = 64
DT_MIN = 0.001
DT_MAX = 0.1
ATTN_HEAD_DIM = 64
ATTN_HEADS = D_ATTN // (2 * ATTN_HEAD_DIM)
ATTN_V_DIM = 2 * ATTN_HEAD_DIM
Q_BLOCK = 128
ROPE_THETA = 10000.0
CONV_WIDTH = 3
D_FF = 5504
N_BRANCH = 3
NORM_EPS = 1e-6
D_IN = D_SSM + 3 * D_ATTN + 3 * D_CONV
SPLITS = tuple(int(s) for s in np.cumsum([D_SSM, D_ATTN, D_ATTN, D_ATTN, D_CONV, D_CONV]))

kernel_name = "hybrid_s5_diffattn_shortconv_macaron"


def rms_norm(x, g):
    xf = x.astype(jnp.float32)
    y = xf * lax.rsqrt(jnp.mean(xf * xf, axis=-1, keepdims=True) + NORM_EPS)
    return (y * g.astype(jnp.float32)).astype(x.dtype)


def swiglu(h, w13, w2):
    a, b = jnp.split(h @ w13, 2, axis=-1)
    return (jax.nn.silu(a) * b) @ w2


def rope_tables(seq_len, dtype):
    pos = jnp.arange(seq_len, dtype=jnp.float32)
    inv = ROPE_THETA ** (-jnp.arange(0, ATTN_HEAD_DIM, 2, dtype=jnp.float32) / ATTN_HEAD_DIM)
    ang = pos[:, None] * inv[None, :]
    return jnp.cos(ang).astype(dtype), jnp.sin(ang).astype(dtype)


def apply_rope(x, cos, sin):
    half = ATTN_HEAD_DIM // 2
    x1, x2 = x[..., :half], x[..., half:]
    c = cos[:, None, None, :]
    s = sin[:, None, None, :]
    return jnp.concatenate([x1 * c - x2 * s, x2 * c + x1 * s], axis=-1)


def _complex_linear_combine(left, right):
    a1r, a1i, b1r, b1i = left
    a2r, a2i, b2r, b2i = right
    ar = a1r * a2r - a1i * a2i
    ai = a1r * a2i + a1i * a2r
    br = a2r * b1r - a2i * b1i + b2r
    bi = a2r * b1i + a2i * b1r + b2i
    return ar, ai, br, bi


def s5_mixer(u, lam_re, lam_im, log_dt, b_re, b_im, c_re, c_im, d_skip, w_glu, b_glu):
    f32 = jnp.float32
    bsz, seq_len, _ = u.shape
    uf = u.astype(f32)
    ug = uf.reshape(bsz, seq_len, SSM_GROUPS, SSM_GROUP)
    bu_re = jnp.einsum('blgh,gnh->blgn', ug, b_re.astype(f32))
    bu_im = jnp.einsum('blgh,gnh->blgn', ug, b_im.astype(f32))
    state_re = jnp.zeros_like(bu_re)
    state_im = jnp.zeros_like(bu_im)
    for direction, rev in ((0, False), (1, True)):
        lr = lam_re[direction].astype(f32)
        li = lam_im[direction].astype(f32)
        dt = jnp.exp(log_dt[direction].astype(f32))[:, None]
        mag = jnp.exp(dt * lr)
        ar = mag * jnp.cos(dt * li)
        ai = mag * jnp.sin(dt * li)
        denom = lr * lr + li * li
        nr = ar - 1.0
        coef_re = (nr * lr + ai * li) / denom
        coef_im = (ai * lr - nr * li) / denom
        bb_re = coef_re * bu_re - coef_im * bu_im
        bb_im = coef_re * bu_im + coef_im * bu_re
        a_re = jnp.broadcast_to(ar, bb_re.shape)
        a_im = jnp.broadcast_to(ai, bb_im.shape)
        _, _, s_re, s_im = lax.associative_scan(
            _complex_linear_combine, (a_re, a_im, bb_re, bb_im), reverse=rev, axis=1)
        state_re = state_re + s_re
        state_im = state_im + s_im
    y = (jnp.einsum('blgn,ghn->blgh', state_re, c_re.astype(f32))
         - jnp.einsum('blgn,ghn->blgh', state_im, c_im.astype(f32)))
    y = y.reshape(bsz, seq_len, D_SSM) + d_skip.astype(f32) * uf
    y = jax.nn.gelu(y)
    y = y * jax.nn.sigmoid(y @ w_glu.astype(f32) + b_glu.astype(f32))
    return y.astype(u.dtype)


def diff_attention(q, k, v, cos, sin, lam_vec, subln_g, lambda_init):
    bsz, seq_len, _ = q.shape
    q = apply_rope(q.reshape(bsz, seq_len, ATTN_HEADS, 2, ATTN_HEAD_DIM), cos, sin)
    k = apply_rope(k.reshape(bsz, seq_len, ATTN_HEADS, 2, ATTN_HEAD_DIM), cos, sin)
    v = v.reshape(bsz, seq_len, ATTN_HEADS, ATTN_V_DIM)
    lv = lam_vec.astype(jnp.float32)
    lam = jnp.exp(jnp.sum(lv[0] * lv[1])) - jnp.exp(jnp.sum(lv[2] * lv[3])) + lambda_init
    scale = ATTN_HEAD_DIM ** -0.5
    n_blocks = seq_len // Q_BLOCK
    qb = q.reshape(bsz, n_blocks, Q_BLOCK, ATTN_HEADS, 2, ATTN_HEAD_DIM).transpose(1, 0, 2, 3, 4, 5)

    def attend_block(q_blk):
        s = jnp.einsum('bqhcd,bkhcd->bhcqk', q_blk, k).astype(jnp.float32) * scale
        p = jax.nn.softmax(s, axis=-1)
        w = p[:, :, 0] - lam * p[:, :, 1]
        return jnp.einsum('bhqk,bkhd->bqhd', w.astype(v.dtype), v)

    o = lax.map(attend_block, qb)
    o = o.transpose(1, 0, 2, 3, 4).reshape(bsz, seq_len, ATTN_HEADS, ATTN_V_DIM)
    o = rms_norm(o, subln_g) * (1.0 - lambda_init)
    return o.reshape(bsz, seq_len, D_ATTN)


def short_conv(bg, cg, xv, conv_w):
    z = cg * xv
    zc = lax.conv_general_dilated(
        z, conv_w[:, None, :].astype(z.dtype), window_strides=(1,),
        padding=((CONV_WIDTH // 2, CONV_WIDTH // 2),),
        dimension_numbers=('NWC', 'WIO', 'NWC'), feature_group_count=D_CONV)
    return bg * zc


def setup_inputs(seed: int = 0) -> dict:
    key = jax.random.key(seed)
    ks = jax.random.split(key, 26)
    f32 = jnp.float32
    nrm = lambda k, shape, s: jax.random.normal(k, shape, f32) * s
    x = jax.random.normal(ks[0], (BATCH, SEQ, D_MODEL), f32)
    norm_w = 1.0 + nrm(ks[1], (DEPTH, 3, D_MODEL), 0.01)
    ffn_w13 = nrm(ks[2], (DEPTH, 2, D_MODEL, 2 * D_FF), D_MODEL ** -0.5)
    ffn_w2 = nrm(ks[3], (DEPTH, 2, D_FF, D_MODEL), D_FF ** -0.5)
    w_in = nrm(ks[4], (DEPTH, D_MODEL, D_IN), D_MODEL ** -0.5)
    s5_lambda_re = -0.5 + nrm(ks[5], (DEPTH, 2, SSM_GROUPS, SSM_STATE), 0.01)
    s5_lambda_im = (math.pi * jnp.arange(SSM_STATE, dtype=f32)
                    + nrm(ks[6], (DEPTH, 2, SSM_GROUPS, SSM_STATE), 0.01))
    s5_log_dt = jax.random.uniform(ks[7], (DEPTH, 2, SSM_GROUPS), f32,
                                   minval=math.log(DT_MIN), maxval=math.log(DT_MAX))
    s5_b_re = nrm(ks[8], (DEPTH, SSM_GROUPS, SSM_STATE, SSM_GROUP), SSM_GROUP ** -0.5)
    s5_b_im = nrm(ks[9], (DEPTH, SSM_GROUPS, SSM_STATE, SSM_GROUP), SSM_GROUP ** -0.5)
    s5_c_re = nrm(ks[10], (DEPTH, SSM_GROUPS, SSM_GROUP, SSM_STATE), SSM_STATE ** -0.5)
    s5_c_im = nrm(ks[11], (DEPTH, SSM_GROUPS, SSM_GROUP, SSM_STATE), SSM_STATE ** -0.5)
    s5_d = nrm(ks[12], (DEPTH, D_SSM), 1.0)
    s5_w_glu = nrm(ks[13], (DEPTH, D_SSM, D_SSM), D_SSM ** -0.5)
    s5_b_glu = nrm(ks[14], (DEPTH, D_SSM), 0.01)
    diff_lambda = nrm(ks[15], (DEPTH, 4, ATTN_HEAD_DIM), 0.1)
    diff_subln = 1.0 + nrm(ks[16], (DEPTH, ATTN_V_DIM), 0.01)
    conv_w = nrm(ks[17], (DEPTH, CONV_WIDTH, D_CONV), CONV_WIDTH ** -0.5)
    w_branch = jnp.concatenate([
        nrm(ks[18], (DEPTH, D_SSM, D_MODEL), D_SSM ** -0.5),
        nrm(ks[19], (DEPTH, D_ATTN, D_MODEL), D_ATTN ** -0.5),
        nrm(ks[20], (DEPTH, D_CONV, D_MODEL), D_CONV ** -0.5)], axis=1)
    w_gate = nrm(ks[21], (DEPTH, D_MODEL, N_BRANCH * D_MODEL), D_MODEL ** -0.5)
    b_gate = nrm(ks[22], (DEPTH, N_BRANCH * D_MODEL), 0.01)
    w_out = nrm(ks[23], (DEPTH, D_MODEL, D_MODEL), D_MODEL ** -0.5)
    final_norm = 1.0 + nrm(ks[24], (D_MODEL,), 0.01)
    return {"x": x, "norm_w": norm_w, "ffn_w13": ffn_w13, "ffn_w2": ffn_w2, "w_in": w_in,
            "s5_lambda_re": s5_lambda_re, "s5_lambda_im": s5_lambda_im, "s5_log_dt": s5_log_dt,
            "s5_b_re": s5_b_re, "s5_b_im": s5_b_im, "s5_c_re": s5_c_re, "s5_c_im": s5_c_im,
            "s5_d": s5_d, "s5_w_glu": s5_w_glu, "s5_b_glu": s5_b_glu,
            "diff_lambda": diff_lambda, "diff_subln": diff_subln, "conv_w": conv_w,
            "w_branch": w_branch, "w_gate": w_gate, "b_gate": b_gate, "w_out": w_out,
            "final_norm": final_norm}


def reference(x, norm_w, ffn_w13, ffn_w2, w_in, s5_lambda_re, s5_lambda_im, s5_log_dt,
              s5_b_re, s5_b_im, s5_c_re, s5_c_im, s5_d, s5_w_glu, s5_b_glu,
              diff_lambda, diff_subln, conv_w, w_branch, w_gate, b_gate, w_out, final_norm):
    bsz, seq_len, _ = x.shape
    cos, sin = rope_tables(seq_len, x.dtype)
    r_a, r_b = D_SSM, D_SSM + D_ATTN
    for l in range(DEPTH):
        lambda_init = 0.8 - 0.6 * math.exp(-0.3 * l)
        x = x + 0.5 * swiglu(rms_norm(x, norm_w[l, 0]), ffn_w13[l, 0], ffn_w2[l, 0])
        h = rms_norm(x, norm_w[l, 1])
        u_ssm, q, k, v, bg, cg, xv = jnp.split(h @ w_in[l], SPLITS, axis=-1)
        y_a = s5_mixer(u_ssm, s5_lambda_re[l], s5_lambda_im[l], s5_log_dt[l],
                       s5_b_re[l], s5_b_im[l], s5_c_re[l], s5_c_im[l],
                       s5_d[l], s5_w_glu[l], s5_b_glu[l])
        y_b = diff_attention(q, k, v, cos, sin, diff_lambda[l], diff_subln[l], lambda_init)
        y_c = short_conv(bg, cg, xv, conv_w[l])
        p_a = y_a @ w_branch[l, :r_a]
        p_b = y_b @ w_branch[l, r_a:r_b]
        p_c = y_c @ w_branch[l, r_b:]
        g = jax.nn.sigmoid(h @ w_gate[l] + b_gate[l]).reshape(bsz, seq_len, N_BRANCH, D_MODEL)
        merged = g[:, :, 0] * p_a + g[:, :, 1] * p_b + g[:, :, 2] * p_c
        x = x + merged @ w_out[l]
        x = x + 0.5 * swiglu(rms_norm(x, norm_w[l, 2]), ffn_w13[l, 1], ffn_w2[l, 1])
    return rms_norm(x, final_norm)
```

```python
import functools
import math

import jax
import jax.numpy as jnp
from jax import lax
from jax.experimental import pallas as pl
from jax.experimental.pallas import tpu as pltpu

NORM_EPS = 1e-6
ROPE_THETA = 10000.0
N_BRANCH = 3
LANES = 128
SUBLANES = 8
MIB = 1024 * 1024
F32 = jnp.float32
BF16 = jnp.bfloat16


def _rms(x, g):
    return x * lax.rsqrt(jnp.mean(x * x, axis=-1, keepdims=True) + NORM_EPS) * g


def _round_up(a, b):
    return (a + b - 1) // b * b


def _tile(n, want):
    t = min(n, want)
    while n % t:
        t -= 1
    return t


def _params(sem, vmem_mib):
    return pltpu.CompilerParams(dimension_semantics=sem, vmem_limit_bytes=vmem_mib * MIB)


def _ffn_kernel(x_ref, g_ref, w1_ref, w3_ref, w2_ref, fg_ref, o_ref, h_ref, *, final):
    j = pl.program_id(1)

    @pl.when(j == 0)
    def _():
        h_ref[...] = _rms(x_ref[...], g_ref[...]).astype(BF16)
        o_ref[...] = jnp.zeros_like(o_ref)

    h = h_ref[...]
    a = jnp.dot(h, w1_ref[...], preferred_element_type=F32)
    b = jnp.dot(h, w3_ref[...], preferred_element_type=F32)
    act = (a * jax.nn.sigmoid(a) * b).astype(BF16)
    o_ref[...] += jnp.dot(act, w2_ref[...], preferred_element_type=F32)

    @pl.when(j == pl.num_programs(1) - 1)
    def _():
        y = x_ref[...] + 0.5 * o_ref[...]
        if final:
            y = _rms(y, fg_ref[...])
        o_ref[...] = y


def _ffn(x, g, w1, w3, w2, fg, *, final, tm, tf):
    n, d = x.shape
    ffp = w1.shape[1]
    return pl.pallas_call(
        functools.partial(_ffn_kernel, final=final),
        grid=(n // tm, ffp // tf),
        in_specs=[
            pl.BlockSpec((tm, d), lambda i, j: (i, 0)),
            pl.BlockSpec((1, d), lambda i, j: (0, 0)),
            pl.BlockSpec((d, tf), lambda i, j: (0, j)),
            pl.BlockSpec((d, tf), lambda i, j: (0, j)),
            pl.BlockSpec((tf, d), lambda i, j: (j, 0)),
            pl.BlockSpec((1, d), lambda i, j: (0, 0)),
        ],
        out_specs=pl.BlockSpec((tm, d), lambda i, j: (i, 0)),
        out_shape=jax.ShapeDtypeStruct((n, d), F32),
        scratch_shapes=[pltpu.VMEM((tm, d), BF16)],
        compiler_params=_params(("parallel", "arbitrary"), 52),
        name="ffn",
    )(x, g, w1, w3, w2, fg)


def _rope_store(acc, cos, sin, o_ref, scale):
    lane = lax.broadcasted_iota(jnp.int32, cos.shape, 1)
    first_half = (lane & (LANES // 4)) == 0
    for c in range(acc.shape[1] // LANES):
        blk = acc[:, c * LANES:(c + 1) * LANES]
        partner = jnp.where(first_half,
                            pltpu.roll(blk, LANES - LANES // 4, axis=1),
                            pltpu.roll(blk, LANES // 4, axis=1))
        o_ref[:, c * LANES:(c + 1) * LANES] = ((blk * cos + partner * sin) * scale).astype(o_ref.dtype)


def _proj_kernel(x_ref, g_ref, w_ref, b_ref, cos_ref, sin_ref, o_ref, h_ref, *, n_gate, n_attn, q_scale):
    j = pl.program_id(1)

    @pl.when(j == 0)
    def _():
        h_ref[...] = _rms(x_ref[...], g_ref[...]).astype(BF16)

    acc = jnp.dot(h_ref[...], w_ref[...], preferred_element_type=F32)

    @pl.when(j < n_gate)
    def _():
        o_ref[...] = jax.nn.sigmoid(acc + b_ref[...]).astype(o_ref.dtype)

    @pl.when((j >= n_gate) & (j < n_gate + n_attn))
    def _():
        _rope_store(acc, cos_ref[...], sin_ref[...], o_ref, q_scale)

    @pl.when((j >= n_gate + n_attn) & (j < n_gate + 2 * n_attn))
    def _():
        _rope_store(acc, cos_ref[...], sin_ref[...], o_ref, 1.0)

    @pl.when(j >= n_gate + 2 * n_attn)
    def _():
        o_ref[...] = acc.astype(o_ref.dtype)


def _proj(x, g, w, b, cos, sin, *, seq_len, n_gate, n_attn, q_scale, tm, tn):
    n, d = x.shape
    cols = w.shape[1]
    pos_blocks = seq_len // tm
    return pl.pallas_call(
        functools.partial(_proj_kernel, n_gate=n_gate, n_attn=n_attn, q_scale=q_scale),
        grid=(n // tm, cols // tn),
        in_specs=[
            pl.BlockSpec((tm, d), lambda i, j: (i, 0)),
            pl.BlockSpec((1, d), lambda i, j: (0, 0)),
            pl.BlockSpec((d, tn), lambda i, j: (0, j)),
            pl.BlockSpec((1, tn), lambda i, j: (0, j)),
            pl.BlockSpec((tm, LANES), lambda i, j: (i % pos_blocks, 0)),
            pl.BlockSpec((tm, LANES), lambda i, j: (i % pos_blocks, 0)),
        ],
        out_specs=pl.BlockSpec((tm, tn), lambda i, j: (i, j)),
        out_shape=jax.ShapeDtypeStruct((n, cols), BF16),
        scratch_shapes=[pltpu.VMEM((tm, d), BF16)],
        compiler_params=_params(("parallel", "arbitrary"), 52),
        name="proj",
    )(x, g, w, b, cos, sin)


def _scan_chunk(bu_ref, tab_ref, carry_ref, *, reverse, lane_group):
    t_len, s2 = bu_ref.shape
    s_dim = s2 // 2
    n_tiles = t_len // SUBLANES
    shifts = (1, 2, 4)
    for lg in range(s_dim // lane_group):
        lo = lg * lane_group
        re_cols = pl.ds(lo, lane_group)
        im_cols = pl.ds(s_dim + lo, lane_group)

        def body(it, carry):
            c_re, c_im = carry
            tile = (n_tiles - 1 - it) if reverse else it
            rows = pl.ds(pl.multiple_of(tile * SUBLANES, SUBLANES), SUBLANES)
            x_re = bu_ref[rows, re_cols]
            x_im = bu_ref[rows, im_cols]
            for lvl, sh in enumerate(shifts):
                a_re = tab_ref[2 * lvl, :, re_cols]
                a_im = tab_ref[2 * lvl + 1, :, re_cols]
                roll_by = (SUBLANES - sh) if reverse else sh
                r_re = pltpu.roll(x_re, roll_by, axis=0)
                r_im = pltpu.roll(x_im, roll_by, axis=0)
                x_re, x_im = (x_re + a_re * r_re - a_im * r_im,
                              x_im + a_re * r_im + a_im * r_re)
            p_re = tab_ref[6, :, re_cols]
            p_im = tab_ref[7, :, re_cols]
            x_re, x_im = (x_re + p_re * c_re - p_im * c_im,
                          x_im + p_re * c_im + p_im * c_re)
            bu_ref[rows, re_cols] = x_re
            bu_ref[rows, im_cols] = x_im
            edge = 0 if reverse else SUBLANES - 1
            n_re = jnp.broadcast_to(x_re[edge:edge + 1, :], x_re.shape)
            n_im = jnp.broadcast_to(x_im[edge:edge + 1, :], x_im.shape)
            return n_re, n_im

        c0 = (carry_ref[0, :, re_cols], carry_ref[1, :, re_cols])
        c_re, c_im = lax.fori_loop(0, n_tiles, body, c0, unroll=2)
        carry_ref[0, :, re_cols] = c_re
        carry_ref[1, :, re_cols] = c_im


def _s5_kernel(u_ref, bmat_ref, tab_ref, cmat_ref, y_ref, bu_ref, carry_ref, *, lane_group):
    d = pl.program_id(1)
    c = pl.program_id(2)

    @pl.when(c == 0)
    def _():
        carry_ref[...] = jnp.zeros_like(carry_ref)

    bu_ref[...] = jnp.dot(u_ref[...], bmat_ref[...], preferred_element_type=F32)

    @pl.when(d == 0)
    def _():
        _scan_chunk(bu_ref, tab_ref.at[0], carry_ref, reverse=False, lane_group=lane_group)

    @pl.when(d == 1)
    def _():
        _scan_chunk(bu_ref, tab_ref.at[0], carry_ref, reverse=True, lane_group=lane_group)

    y_ref[0] = jnp.dot(bu_ref[...].astype(BF16), cmat_ref[0], preferred_element_type=F32)


def _s5_scan(proj, bmat, tabs, cmat, *, batch, seq_len, u_col, d_ssm, tc):
    n = proj.shape[0]
    n_chunks = seq_len // tc
    s2 = bmat.shape[1]
    s_dim = s2 // 2
    lane_group = min(s_dim, 4 * LANES)

    def chunk_row(b, d, c):
        return b * n_chunks + jnp.where(d == 0, c, n_chunks - 1 - c)

    return pl.pallas_call(
        functools.partial(_s5_kernel, lane_group=lane_group),
        grid=(batch, 2, n_chunks),
        in_specs=[
            pl.BlockSpec((tc, d_ssm), lambda b, d, c: (chunk_row(b, d, c), u_col)),
            pl.BlockSpec((d_ssm, s2), lambda b, d, c: (0, 0)),
            pl.BlockSpec((1, 8, SUBLANES, s_dim), lambda b, d, c: (d, 0, 0, 0)),
            pl.BlockSpec((1, s2, d_ssm), lambda b, d, c: (d, 0, 0)),
        ],
        out_specs=pl.BlockSpec((1, tc, d_ssm), lambda b, d, c: (d, chunk_row(b, d, c), 0)),
        out_shape=jax.ShapeDtypeStruct((2, n, d_ssm), F32),
        scratch_shapes=[pltpu.VMEM((tc, s2), F32), pltpu.VMEM((2, SUBLANES, s_dim), F32)],
        compiler_params=_params(("parallel", "arbitrary", "arbitrary"), 52),
        name="s5_scan",
    )(proj, bmat, tabs, cmat)


def _glu_kernel(ys_ref, u_ref, dskip_ref, w_ref, b_ref, o_ref):
    y = ys_ref[0] + ys_ref[1] + dskip_ref[...] * u_ref[...].astype(F32)
    y = jax.nn.gelu(y)
    z = jnp.dot(y.astype(BF16), w_ref[...], preferred_element_type=F32) + b_ref[...]
    o_ref[...] = (y * jax.nn.sigmoid(z)).astype(o_ref.dtype)


def _s5_glu(ys, proj, dskip, w, b, *, u_col, tm):
    _, n, d_ssm = ys.shape
    return pl.pallas_call(
        _glu_kernel,
        grid=(n // tm,),
        in_specs=[
            pl.BlockSpec((2, tm, d_ssm), lambda i: (0, i, 0)),
            pl.BlockSpec((tm, d_ssm), lambda i: (i, u_col)),
            pl.BlockSpec((1, d_ssm), lambda i: (0, 0)),
            pl.BlockSpec((d_ssm, d_ssm), lambda i: (0, 0)),
            pl.BlockSpec((1, d_ssm), lambda i: (0, 0)),
        ],
        out_specs=pl.BlockSpec((tm, d_ssm), lambda i: (i, 0)),
        out_shape=jax.ShapeDtypeStruct((n, d_ssm), BF16),
        compiler_params=_params(("parallel",), 40),
        name="s5_glu",
    )(ys, proj, dskip, w, b)


def _attn_kernel(q_ref, k_ref, v_ref, lam_ref, sg_ref, o_ref, *, lambda_init, head_dim):
    q = q_ref[0]
    k = k_ref[0]
    lane = lax.broadcasted_iota(jnp.int32, q.shape, 1)
    zero = jnp.zeros_like(q)
    dims = (((1,), (1,)), ((), ()))
    s1 = lax.dot_general(jnp.where(lane < head_dim, q, zero), k, dims, preferred_element_type=F32)
    s2 = lax.dot_general(jnp.where(lane >= head_dim, q, zero), k, dims, preferred_element_type=F32)
    e1 = jnp.exp(s1 - jnp.max(s1, axis=-1, keepdims=True))
    e2 = jnp.exp(s2 - jnp.max(s2, axis=-1, keepdims=True))
    lv = lam_ref[...]
    lam = (jnp.exp(jnp.sum(lv[0:1] * lv[1:2], axis=-1, keepdims=True))
           - jnp.exp(jnp.sum(lv[2:3] * lv[3:4], axis=-1, keepdims=True)) + lambda_init)
    r1 = 1.0 / jnp.sum(e1, axis=-1, keepdims=True)
    r2 = lam / jnp.sum(e2, axis=-1, keepdims=True)
    w = (e1 * r1 - e2 * r2).astype(BF16)
    o = jnp.dot(w, v_ref[0], preferred_element_type=F32)
    o_ref[0] = (_rms(o, sg_ref[...]) * (1.0 - lambda_init)).astype(o_ref.dtype)


def _attention(proj3, lam_vec, subln, *, q_col, k_col, v_col, n_heads, head_dim, lambda_init, tq):
    batch, seq_len, _ = proj3.shape
    vd = 2 * head_dim
    return pl.pallas_call(
        functools.partial(_attn_kernel, lambda_init=lambda_init, head_dim=head_dim),
        grid=(batch, n_heads, seq_len // tq),
        in_specs=[
            pl.BlockSpec((1, tq, vd), lambda b, h, i: (b, i, q_col + h)),
            pl.BlockSpec((1, seq_len, vd), lambda b, h, i: (b, 0, k_col + h)),
            pl.BlockSpec((1, seq_len, vd), lambda b, h, i: (b, 0, v_col + h)),
            pl.BlockSpec((4, head_dim), lambda b, h, i: (0, 0)),
            pl.BlockSpec((1, vd), lambda b, h, i: (0, 0)),
        ],
        out_specs=pl.BlockSpec((1, tq, vd), lambda b, h, i: (b, i, h)),
        out_shape=jax.ShapeDtypeStruct((batch, seq_len, n_heads * vd), BF16),
        compiler_params=_params(("parallel", "parallel", "arbitrary"), 52),
        name="diff_attn",
    )(proj3, proj3, proj3, lam_vec, subln)


def _conv_kernel(bg_ref, cg_ref, xv_ref, w_ref, o_ref):
    z = cg_ref[0].astype(F32) * xv_ref[0].astype(F32)
    seq_len = z.shape[0]
    row = lax.broadcasted_iota(jnp.int32, z.shape, 0)
    z_prev = jnp.where(row == 0, 0.0, pltpu.roll(z, 1, axis=0))
    z_next = jnp.where(row == seq_len - 1, 0.0, pltpu.roll(z, seq_len - 1, axis=0))
    w = w_ref[...]
    zc = w[0:1] * z_prev + w[1:2] * z + w[2:3] * z_next
    o_ref[0] = (bg_ref[0].astype(F32) * zc).astype(o_ref.dtype)


def _short_conv(proj3, conv_w, *, bg_col, d_conv):
    batch, seq_len, _ = proj3.shape
    tc = min(d_conv, LANES)
    per = d_conv // tc
    return pl.pallas_call(
        _conv_kernel,
        grid=(batch, per),
        in_specs=[
            pl.BlockSpec((1, seq_len, tc), lambda b, j: (b, 0, bg_col * per + j)),
            pl.BlockSpec((1, seq_len, tc), lambda b, j: (b, 0, (bg_col + 1) * per + j)),
            pl.BlockSpec((1, seq_len, tc), lambda b, j: (b, 0, (bg_col + 2) * per + j)),
            pl.BlockSpec((conv_w.shape[0], tc), lambda b, j: (0, j)),
        ],
        out_specs=pl.BlockSpec((1, seq_len, tc), lambda b, j: (b, 0, j)),
        out_shape=jax.ShapeDtypeStruct((batch, seq_len, d_conv), BF16),
        compiler_params=_params(("parallel", "parallel"), 40),
        name="short_conv",
    )(proj3, proj3, proj3, conv_w)


def _merge_kernel(x_ref, ya_ref, yb_ref, yc_ref, ga_ref, gb_ref, gc_ref, wb_ref, wo_ref, o_ref):
    r_a = ya_ref.shape[1]
    r_b = r_a + yb_ref.shape[1]
    pa = jnp.dot(ya_ref[...], wb_ref[0:r_a, :], preferred_element_type=F32)
    merged = ga_ref[...].astype(F32) * pa
    pb = jnp.dot(yb_ref[...], wb_ref[r_a:r_b, :], preferred_element_type=F32)
    merged += gb_ref[...].astype(F32) * pb
    pc = jnp.dot(yc_ref[...], wb_ref[r_b:, :], preferred_element_type=F32)
    merged += gc_ref[...].astype(F32) * pc
    o_ref[...] = x_ref[...] + jnp.dot(merged.astype(BF16), wo_ref[...], preferred_element_type=F32)


def _merge(x, ya, yb, yc, proj, wb, wo, *, tm):
    n, d = x.shape
    row = lambda i: (i, 0)
    const = lambda i: (0, 0)
    return pl.pallas_call(
        _merge_kernel,
        grid=(n // tm,),
        in_specs=[
            pl.BlockSpec((tm, d), row),
            pl.BlockSpec((tm, ya.shape[1]), row),
            pl.BlockSpec((tm, yb.shape[1]), row),
            pl.BlockSpec((tm, yc.shape[1]), row),
            pl.BlockSpec((tm, d), lambda i: (i, 0)),
            pl.BlockSpec((tm, d), lambda i: (i, 1)),
            pl.BlockSpec((tm, d), lambda i: (i, 2)),
            pl.BlockSpec(wb.shape, const),
            pl.BlockSpec(wo.shape, const),
        ],
        out_specs=pl.BlockSpec((tm, d), row),
        out_shape=jax.ShapeDtypeStruct((n, d), F32),
        compiler_params=_params(("parallel",), 56),
        name="merge",
    )(x, ya, yb, yc, proj, proj, proj, wb, wo)


def _rope_tables(seq_len, head_dim):
    pos = jnp.arange(seq_len, dtype=F32)
    inv = ROPE_THETA ** (-jnp.arange(0, head_dim, 2, dtype=F32) / head_dim)
    ang = pos[:, None] * inv[None, :]
    cos, sin = jnp.cos(ang), jnp.sin(ang)
    reps = LANES // head_dim
    return (jnp.tile(jnp.concatenate([cos, cos], axis=1), (1, reps)),
            jnp.tile(jnp.concatenate([-sin, sin], axis=1), (1, reps)))


def _s5_tables(lam_re, lam_im, log_dt, b_re, b_im, c_re, c_im):
    n_groups, n_state, group = b_re.shape
    s_dim = n_groups * n_state
    eye = jnp.eye(n_groups, dtype=F32)
    bmat = jnp.concatenate([
        jnp.einsum('gk,gnh->ghkn', eye, b_re).reshape(n_groups * group, s_dim),
        jnp.einsum('gk,gnh->ghkn', eye, b_im).reshape(n_groups * group, s_dim)], axis=1)
    cmats, tabs = [], []
    row = jnp.arange(SUBLANES)[:, None]
    for d in range(2):
        lr, li = lam_re[d], lam_im[d]
        dt = jnp.exp(log_dt[d])[:, None]
        mag = jnp.exp(dt * lr)
        ar, ai = mag * jnp.cos(dt * li), mag * jnp.sin(dt * li)
        denom = lr * lr + li * li
        nr = ar - 1.0
        kr = (nr * lr + ai * li) / denom
        ki = (ai * lr - nr * li) / denom
        er = c_re * kr[:, None, :] - c_im * ki[:, None, :]
        ei = c_re * ki[:, None, :] + c_im * kr[:, None, :]
        cmats.append(jnp.concatenate([
            jnp.einsum('gk,ghn->gnkh', eye, er).reshape(s_dim, n_groups * group),
            jnp.einsum('gk,ghn->gnkh', eye, -ei).reshape(s_dim, n_groups * group)], axis=0))
        pows = [(ar.reshape(-1), ai.reshape(-1))]
        for _ in range(SUBLANES - 1):
            pr, pi = pows[-1]
            pows.append((pr * pows[0][0] - pi * pows[0][1], pr * pows[0][1] + pi * pows[0][0]))
        rows = []
        for sh in (1, 2, 4):
            keep = (row >= sh) if d == 0 else (row < SUBLANES - sh)
            rows += [jnp.where(keep, pows[sh - 1][0][None, :], 0.0), jnp.where(keep, pows[sh - 1][1][None, :], 0.0)]
        order = jnp.arange(SUBLANES) if d == 0 else jnp.arange(SUBLANES)[::-1]
        rows += [jnp.stack([p[0] for p in pows])[order], jnp.stack([p[1] for p in pows])[order]]
        tabs.append(jnp.stack(rows))
    return bmat.astype(BF16), jnp.stack(tabs), jnp.stack(cmats).astype(BF16)


def kernel(x, norm_w, ffn_w13, ffn_w2, w_in, s5_lambda_re, s5_lambda_im, s5_log_dt, s5_b_re, s5_b_im, s5_c_re, s5_c_im, s5_d, s5_w_glu, s5_b_glu, diff_lambda, diff_subln, conv_w, w_branch, w_gate, b_gate, w_out, final_norm):
    batch, seq_len, d_model = x.shape
    depth = norm_w.shape[0]
    d_ff = ffn_w2.shape[2]
    d_ssm = s5_d.shape[-1]
    d_conv = conv_w.shape[-1]
    d_attn = (w_in.shape[-1] - d_ssm - 3 * d_conv) // 3
    head_dim = diff_lambda.shape[-1]
    n_heads = d_attn // (2 * head_dim)
    n = batch * seq_len
    assert 2 * head_dim == LANES and d_ssm == d_conv and d_attn == 2 * d_ssm

    tm_ffn = _tile(n, 512)
    tf = _tile(_round_up(d_ff, 512), 512) if d_ff >= 512 else d_ff
    ffp = _round_up(d_ff, tf)
    tm_proj = _tile(seq_len, 1024)
    tn_proj = d_attn
    tm_merge = _tile(n, 256)
    tq = _tile(seq_len, 256)
    t_chunk = _tile(seq_len, 512)

    cos, sin = _rope_tables(seq_len, head_dim)
    xf = x.reshape(n, d_model)
    splits = [d_ssm, d_ssm + d_attn, d_ssm + 2 * d_attn, d_ssm + 3 * d_attn,
              d_ssm + 3 * d_attn + d_conv, d_ssm + 3 * d_attn + 2 * d_conv]
    n_gate = N_BRANCH * d_model // tn_proj
    gate_cols = N_BRANCH * d_model
    q_col = gate_cols // LANES
    k_col = q_col + d_attn // LANES
    v_col = k_col + d_attn // LANES
    u_col = (gate_cols + 3 * d_attn) // d_ssm
    bg_col = u_col + 1

    for l in range(depth):
        lambda_init = 0.8 - 0.6 * math.exp(-0.3 * l)
        w_u, w_q, w_k, w_v, w_bg, w_cg, w_xv = jnp.split(w_in[l], splits, axis=-1)
        w_cat = jnp.concatenate([w_gate[l], w_q, w_k, w_v, w_u, w_bg, w_cg, w_xv], axis=1).astype(BF16)
        b_cat = jnp.concatenate([b_gate[l], jnp.zeros((w_in.shape[-1],), F32)])[None, :]
        bmat, tabs, cmat = _s5_tables(s5_lambda_re[l], s5_lambda_im[l], s5_log_dt[l],
                                      s5_b_re[l], s5_b_im[l], s5_c_re[l], s5_c_im[l])

        def ffn_weights(idx):
            pad = ((0, 0), (0, ffp - d_ff))
            w1 = jnp.pad(ffn_w13[l, idx, :, :d_ff].astype(BF16), pad)
            w3 = jnp.pad(ffn_w13[l, idx, :, d_ff:].astype(BF16), pad)
            w2 = jnp.pad(ffn_w2[l, idx].astype(BF16), ((0, ffp - d_ff), (0, 0)))
            return w1, w3, w2

        w1, w3, w2 = ffn_weights(0)
        xf = _ffn(xf, norm_w[l, 0][None, :], w1, w3, w2, final_norm[None, :], final=False, tm=tm_ffn, tf=tf)

        proj = _proj(xf, norm_w[l, 1][None, :], w_cat, b_cat, cos, sin, seq_len=seq_len, n_gate=n_gate,
                     n_attn=d_attn // tn_proj, q_scale=head_dim ** -0.5, tm=tm_proj, tn=tn_proj)
        proj3 = proj.reshape(batch, seq_len, proj.shape[1])

        ys = _s5_scan(proj, bmat, tabs, cmat, batch=batch, seq_len=seq_len, u_col=u_col, d_ssm=d_ssm,
                      tc=t_chunk)
        y_a = _s5_glu(ys, proj, s5_d[l][None, :], s5_w_glu[l].astype(BF16), s5_b_glu[l][None, :],
                      u_col=u_col, tm=_tile(n, 1024))
        y_b = _attention(proj3, diff_lambda[l], diff_subln[l][None, :], q_col=q_col, k_col=k_col, v_col=v_col,
                         n_heads=n_heads, head_dim=head_dim, lambda_init=lambda_init, tq=tq)
        y_c = _short_conv(proj3, conv_w[l], bg_col=bg_col, d_conv=d_conv)

        xf = _merge(xf, y_a, y_b.reshape(n, d_attn), y_c.reshape(n, d_conv), proj,
                    w_branch[l].astype(BF16), w_out[l].astype(BF16), tm=tm_merge)

        w1, w3, w2 = ffn_weights(1)
        xf = _ffn(xf, norm_w[l, 2][None, :], w1, w3, w2, final_norm[None, :], final=(l == depth - 1),
                  tm=tm_ffn, tf=tf)
    return xf.reshape(batch, seq_len, d_model)
```

```python
import functools
import math

import jax
import jax.numpy as jnp
from jax import lax
from jax.experimental import pallas as pl
from jax.experimental.pallas import tpu as pltpu

NORM_EPS = 1e-6
ROPE_THETA = 10000.0
N_BRANCH = 3
LANES = 128
SUBLANES = 8
MIB = 1024 * 1024
F32 = jnp.float32
BF16 = jnp.bfloat16


def _rms(x, g):
    return x * lax.rsqrt(jnp.mean(x * x, axis=-1, keepdims=True) + NORM_EPS) * g


def _round_up(a, b):
    return (a + b - 1) // b * b


def _tile(n, want):
    t = min(n, want)
    while n % t:
        t -= 1
    return t


def _params(sem, vmem_mib):
    return pltpu.CompilerParams(dimension_semantics=sem, vmem_limit_bytes=vmem_mib * MIB)


def _ffn_kernel(x_ref, g_ref, wa_ref, *rest, final, d_ff, n_units):
    wb_refs = rest[:n_units]
    w2_refs = rest[n_units:2 * n_units]
    fg_ref, o_ref, h_ref = rest[2 * n_units:]
    j = pl.program_id(1)
    tf = wa_ref.shape[1]

    @pl.when(j == 0)
    def _():
        h_ref[...] = _rms(x_ref[...], g_ref[...]).astype(BF16)
        o_ref[...] = jnp.zeros_like(o_ref)

    h = h_ref[...]
    a = jnp.dot(h, wa_ref[...], preferred_element_type=F32)
    b = jnp.dot(h, jnp.concatenate([r[...] for r in wb_refs], axis=1), preferred_element_type=F32)
    col = j * tf + lax.broadcasted_iota(jnp.int32, a.shape, 1)
    act = jnp.where(col < d_ff, a * jax.nn.sigmoid(a) * b, 0.0).astype(BF16)
    w2 = jnp.concatenate([r[...] for r in w2_refs], axis=0)
    o_ref[...] += jnp.dot(act, w2, preferred_element_type=F32)

    @pl.when(j == pl.num_programs(1) - 1)
    def _():
        y = x_ref[...] + 0.5 * o_ref[...]
        if final:
            y = _rms(y, fg_ref[...])
        o_ref[...] = y


def _ffn(x, g, w13, w2, fg, *, layer, idx, final, tm, tf):
    n, d = x.shape
    d_ff = w2.shape[2]
    assert d_ff % LANES == 0 and tf % LANES == 0 and 2 * d_ff >= _round_up(d_ff, tf)
    n_units = tf // LANES
    ff_units = d_ff // LANES

    def wb_spec(r):
        return pl.BlockSpec((None, None, d, LANES), lambda i, j: (
            layer, idx, 0, ff_units + jnp.minimum(j * n_units + r, ff_units - 1)))

    def w2_spec(r):
        return pl.BlockSpec((None, None, LANES, d), lambda i, j: (
            layer, idx, jnp.minimum(j * n_units + r, ff_units - 1), 0))

    return pl.pallas_call(
        functools.partial(_ffn_kernel, final=final, d_ff=d_ff, n_units=n_units),
        grid=(n // tm, pl.cdiv(d_ff, tf)),
        in_specs=[
            pl.BlockSpec((tm, d), lambda i, j: (i, 0)),
            pl.BlockSpec((1, d), lambda i, j: (0, 0)),
            pl.BlockSpec((None, None, d, tf), lambda i, j: (layer, idx, 0, j)),
            *[wb_spec(r) for r in range(n_units)],
            *[w2_spec(r) for r in range(n_units)],
            pl.BlockSpec((1, d), lambda i, j: (0, 0)),
        ],
        out_specs=pl.BlockSpec((tm, d), lambda i, j: (i, 0)),
        out_shape=jax.ShapeDtypeStruct((n, d), F32),
        scratch_shapes=[pltpu.VMEM((tm, d), BF16)],
        compiler_params=_params(("parallel", "arbitrary"), 52),
        name="ffn",
    )(x, g, w13, *([w13] * n_units), *([w2] * n_units), fg)


def _rope(acc, cos, sin, scale):
    lane = lax.broadcasted_iota(jnp.int32, cos.shape, 1)
    first_half = (lane & (LANES // 4)) == 0
    out = []
    for c in range(acc.shape[1] // LANES):
        blk = acc[:, c * LANES:(c + 1) * LANES]
        partner = jnp.where(first_half,
                            pltpu.roll(blk, LANES - LANES // 4, axis=1),
                            pltpu.roll(blk, LANES // 4, axis=1))
        out.append((blk * cos + partner * sin) * scale)
    return jnp.concatenate(out, axis=1)


def _proj_kernel(x_ref, g_ref, wg_ref, wi_ref, b_ref, cos_ref, sin_ref, o_ref, h_ref, *,
                 n_gate, n_attn, q_scale, chunk):
    j = pl.program_id(1)
    tn = o_ref.shape[1]

    @pl.when(j == 0)
    def _():
        h_ref[...] = _rms(x_ref[...], g_ref[...]).astype(BF16)

    def run(w_ref, epilogue):
        for c in range(tn // chunk):
            cols = slice(c * chunk, (c + 1) * chunk)
            acc = jnp.dot(h_ref[...], w_ref[:, cols].astype(BF16), preferred_element_type=F32)
            o_ref[:, cols] = epilogue(acc, cols).astype(o_ref.dtype)

    q_lo = n_gate + 1
    k_lo = q_lo + n_attn

    @pl.when(j < n_gate)
    def _():
        run(wg_ref, lambda acc, cols: jax.nn.sigmoid(acc + b_ref[:, cols]))

    @pl.when((j >= q_lo) & (j < k_lo))
    def _():
        run(wi_ref, lambda acc, cols: _rope(acc, cos_ref[...], sin_ref[...], q_scale))

    @pl.when((j >= k_lo) & (j < k_lo + n_attn))
    def _():
        run(wi_ref, lambda acc, cols: _rope(acc, cos_ref[...], sin_ref[...], 1.0))

    @pl.when((j == n_gate) | (j >= k_lo + n_attn))
    def _():
        run(wi_ref, lambda acc, cols: acc)


def _proj(x, g, w_gate, b_gate, w_in, cos, sin, *, layer, seq_len, n_attn, q_scale, tm, tn):
    n, d = x.shape
    n_gate = w_gate.shape[2] // tn
    n_in = w_in.shape[2] // tn
    pos_blocks = seq_len // tm
    gate_blk = lambda i, j: (layer, 0, jnp.minimum(j, n_gate - 1))
    return pl.pallas_call(
        functools.partial(_proj_kernel, n_gate=n_gate, n_attn=n_attn, q_scale=q_scale,
                          chunk=min(tn, 2 * LANES)),
        grid=(n // tm, n_gate + n_in),
        in_specs=[
            pl.BlockSpec((tm, d), lambda i, j: (i, 0)),
            pl.BlockSpec((1, d), lambda i, j: (0, 0)),
            pl.BlockSpec((None, d, tn), gate_blk),
            pl.BlockSpec((None, d, tn), lambda i, j: (layer, 0, jnp.maximum(j - n_gate, 0))),
            pl.BlockSpec((None, 1, tn), gate_blk),
            pl.BlockSpec((tm, LANES), lambda i, j: (i % pos_blocks, 0)),
            pl.BlockSpec((tm, LANES), lambda i, j: (i % pos_blocks, 0)),
        ],
        out_specs=pl.BlockSpec((tm, tn), lambda i, j: (i, j)),
        out_shape=jax.ShapeDtypeStruct((n, (n_gate + n_in) * tn), BF16),
        scratch_shapes=[pltpu.VMEM((tm, d), BF16)],
        compiler_params=_params(("parallel", "arbitrary"), 52),
        name="proj",
    )(x, g, w_gate, w_in, b_gate, cos, sin)


def _scan_chunk(bu_ref, tab_ref, carry_ref, *, reverse, lane_group):
    t_len, s2 = bu_ref.shape
    s_dim = s2 // 2
    n_tiles = t_len // SUBLANES
    shifts = (1, 2, 4)
    for lg in range(s_dim // lane_group):
        lo = lg * lane_group
        re_cols = pl.ds(lo, lane_group)
        im_cols = pl.ds(s_dim + lo, lane_group)

        def body(it, carry):
            c_re, c_im = carry
            tile = (n_tiles - 1 - it) if reverse else it
            rows = pl.ds(pl.multiple_of(tile * SUBLANES, SUBLANES), SUBLANES)
            x_re = bu_ref[rows, re_cols]
            x_im = bu_ref[rows, im_cols]
            for lvl, sh in enumerate(shifts):
                a_re = tab_ref[2 * lvl, :, re_cols]
                a_im = tab_ref[2 * lvl + 1, :, re_cols]
                roll_by = (SUBLANES - sh) if reverse else sh
                r_re = pltpu.roll(x_re, roll_by, axis=0)
                r_im = pltpu.roll(x_im, roll_by, axis=0)
                x_re, x_im = (x_re + a_re * r_re - a_im * r_im,
                              x_im + a_re * r_im + a_im * r_re)
            p_re = tab_ref[6, :, re_cols]
            p_im = tab_ref[7, :, re_cols]
            x_re, x_im = (x_re + p_re * c_re - p_im * c_im,
                          x_im + p_re * c_im + p_im * c_re)
            bu_ref[rows, re_cols] = x_re
            bu_ref[rows, im_cols] = x_im
            edge = 0 if reverse else SUBLANES - 1
            n_re = jnp.broadcast_to(x_re[edge:edge + 1, :], x_re.shape)
            n_im = jnp.broadcast_to(x_im[edge:edge + 1, :], x_im.shape)
            return n_re, n_im

        c0 = (carry_ref[0, :, re_cols], carry_ref[1, :, re_cols])
        c_re, c_im = lax.fori_loop(0, n_tiles, body, c0, unroll=2)
        carry_ref[0, :, re_cols] = c_re
        carry_ref[1, :, re_cols] = c_im


def _s5_kernel(u_ref, bmat_ref, tab_ref, cmat_ref, y_ref, bu_ref, carry_ref, *, lane_group):
    d = pl.program_id(1)
    c = pl.program_id(2)

    @pl.when(c == 0)
    def _():
        carry_ref[...] = jnp.zeros_like(carry_ref)

    bu_ref[...] = jnp.dot(u_ref[...], bmat_ref[...], preferred_element_type=F32)

    @pl.when(d == 0)
    def _():
        _scan_chunk(bu_ref, tab_ref.at[0], carry_ref, reverse=False, lane_group=lane_group)

    @pl.when(d == 1)
    def _():
        _scan_chunk(bu_ref, tab_ref.at[0], carry_ref, reverse=True, lane_group=lane_group)

    y_ref[0] = jnp.dot(bu_ref[...].astype(BF16), cmat_ref[0], preferred_element_type=F32)


def _s5_scan(proj, bmat, tabs, cmat, *, batch, seq_len, u_col, d_ssm, tc):
    n = proj.shape[0]
    n_chunks = seq_len // tc
    s2 = bmat.shape[1]
    s_dim = s2 // 2
    lane_group = min(s_dim, 4 * LANES)

    def chunk_row(b, d, c):
        return b * n_chunks + jnp.where(d == 0, c, n_chunks - 1 - c)

    return pl.pallas_call(
        functools.partial(_s5_kernel, lane_group=lane_group),
        grid=(batch, 2, n_chunks),
        in_specs=[
            pl.BlockSpec((tc, d_ssm), lambda b, d, c: (chunk_row(b, d, c), u_col)),
            pl.BlockSpec((d_ssm, s2), lambda b, d, c: (0, 0)),
            pl.BlockSpec((1, 8, SUBLANES, s_dim), lambda b, d, c: (d, 0, 0, 0)),
            pl.BlockSpec((1, s2, d_ssm), lambda b, d, c: (d, 0, 0)),
        ],
        out_specs=pl.BlockSpec((1, tc, d_ssm), lambda b, d, c: (d, chunk_row(b, d, c), 0)),
        out_shape=jax.ShapeDtypeStruct((2, n, d_ssm), F32),
        scratch_shapes=[pltpu.VMEM((tc, s2), F32), pltpu.VMEM((2, SUBLANES, s_dim), F32)],
        compiler_params=_params(("parallel", "arbitrary", "arbitrary"), 52),
        name="s5_scan",
    )(proj, bmat, tabs, cmat)


def _glu_kernel(ys_ref, u_ref, dskip_ref, w_ref, b_ref, o_ref):
    y = ys_ref[0] + ys_ref[1] + dskip_ref[...] * u_ref[...].astype(F32)
    y = jax.nn.gelu(y)
    z = jnp.dot(y.astype(BF16), w_ref[...], preferred_element_type=F32) + b_ref[...]
    o_ref[...] = (y * jax.nn.sigmoid(z)).astype(o_ref.dtype)


def _s5_glu(ys, proj, dskip, w, b, *, u_col, tm):
    _, n, d_ssm = ys.shape
    return pl.pallas_call(
        _glu_kernel,
        grid=(n // tm,),
        in_specs=[
            pl.BlockSpec((2, tm, d_ssm), lambda i: (0, i, 0)),
            pl.BlockSpec((tm, d_ssm), lambda i: (i, u_col)),
            pl.BlockSpec((1, d_ssm), lambda i: (0, 0)),
            pl.BlockSpec((d_ssm, d_ssm), lambda i: (0, 0)),
            pl.BlockSpec((1, d_ssm), lambda i: (0, 0)),
        ],
        out_specs=pl.BlockSpec((tm, d_ssm), lambda i: (i, 0)),
        out_shape=jax.ShapeDtypeStruct((n, d_ssm), BF16),
        compiler_params=_params(("parallel",), 40),
        name="s5_glu",
    )(ys, proj, dskip, w, b)


def _attn_kernel(q_ref, k_ref, v_ref, lam_ref, sg_ref, o_ref, *, lambda_init, head_dim):
    q = q_ref[0]
    k = k_ref[0]
    lane = lax.broadcasted_iota(jnp.int32, q.shape, 1)
    zero = jnp.zeros_like(q)
    dims = (((1,), (1,)), ((), ()))
    s1 = lax.dot_general(jnp.where(lane < head_dim, q, zero), k, dims, preferred_element_type=F32)
    s2 = lax.dot_general(jnp.where(lane >= head_dim, q, zero), k, dims, preferred_element_type=F32)
    e1 = jnp.exp2(s1 - jnp.max(s1, axis=-1, keepdims=True))
    e2 = jnp.exp2(s2 - jnp.max(s2, axis=-1, keepdims=True))
    lv = lam_ref[...]
    lam = (jnp.exp(jnp.sum(lv[0:1] * lv[1:2], axis=-1, keepdims=True))
           - jnp.exp(jnp.sum(lv[2:3] * lv[3:4], axis=-1, keepdims=True)) + lambda_init)
    r1 = 1.0 / jnp.sum(e1, axis=-1, keepdims=True)
    r2 = lam / jnp.sum(e2, axis=-1, keepdims=True)
    w = (e1 * r1 - e2 * r2).astype(BF16)
    o = jnp.dot(w, v_ref[0], preferred_element_type=F32)
    o_ref[0] = (_rms(o, sg_ref[...]) * (1.0 - lambda_init)).astype(o_ref.dtype)


def _attention(proj3, lam_vec, subln, *, q_col, k_col, v_col, n_heads, head_dim, lambda_init, tq):
    batch, seq_len, _ = proj3.shape
    vd = 2 * head_dim
    return pl.pallas_call(
        functools.partial(_attn_kernel, lambda_init=lambda_init, head_dim=head_dim),
        grid=(batch, n_heads, seq_len // tq),
        in_specs=[
            pl.BlockSpec((1, tq, vd), lambda b, h, i: (b, i, q_col + h)),
            pl.BlockSpec((1, seq_len, vd), lambda b, h, i: (b, 0, k_col + h)),
            pl.BlockSpec((1, seq_len, vd), lambda b, h, i: (b, 0, v_col + h)),
            pl.BlockSpec((4, head_dim), lambda b, h, i: (0, 0)),
            pl.BlockSpec((1, vd), lambda b, h, i: (0, 0)),
        ],
        out_specs=pl.BlockSpec((1, tq, vd), lambda b, h, i: (b, i, h)),
        out_shape=jax.ShapeDtypeStruct((batch, seq_len, n_heads * vd), BF16),
        compiler_params=_params(("parallel", "parallel", "arbitrary"), 52),
        name="diff_attn",
    )(proj3, proj3, proj3, lam_vec, subln)


def _conv_kernel(bg_ref, cg_ref, xv_ref, w_ref, o_ref):
    z = cg_ref[0].astype(F32) * xv_ref[0].astype(F32)
    seq_len = z.shape[0]
    row = lax.broadcasted_iota(jnp.int32, z.shape, 0)
    z_prev = jnp.where(row == 0, 0.0, pltpu.roll(z, 1, axis=0))
    z_next = jnp.where(row == seq_len - 1, 0.0, pltpu.roll(z, seq_len - 1, axis=0))
    w = w_ref[...]
    zc = w[0:1] * z_prev + w[1:2] * z + w[2:3] * z_next
    o_ref[0] = (bg_ref[0].astype(F32) * zc).astype(o_ref.dtype)


def _short_conv(proj3, conv_w, *, bg_col, d_conv):
    batch, seq_len, _ = proj3.shape
    tc = min(d_conv, LANES)
    per = d_conv // tc
    return pl.pallas_call(
        _conv_kernel,
        grid=(batch, per),
        in_specs=[
            pl.BlockSpec((1, seq_len, tc), lambda b, j: (b, 0, bg_col * per + j)),
            pl.BlockSpec((1, seq_len, tc), lambda b, j: (b, 0, (bg_col + 1) * per + j)),
            pl.BlockSpec((1, seq_len, tc), lambda b, j: (b, 0, (bg_col + 2) * per + j)),
            pl.BlockSpec((conv_w.shape[0], tc), lambda b, j: (0, j)),
        ],
        out_specs=pl.BlockSpec((1, seq_len, tc), lambda b, j: (b, 0, j)),
        out_shape=jax.ShapeDtypeStruct((batch, seq_len, d_conv), BF16),
        compiler_params=_params(("parallel", "parallel"), 40),
        name="short_conv",
    )(proj3, proj3, proj3, conv_w)


def _merge_kernel(x_ref, ya_ref, yb_ref, yc_ref, ga_ref, gb_ref, gc_ref, wb_ref, wo_ref, o_ref):
    r_a = ya_ref.shape[1]
    r_b = r_a + yb_ref.shape[1]
    pa = jnp.dot(ya_ref[...], wb_ref[0:r_a, :], preferred_element_type=F32)
    merged = ga_ref[...].astype(F32) * pa
    pb = jnp.dot(yb_ref[...], wb_ref[r_a:r_b, :], preferred_element_type=F32)
    merged += gb_ref[...].astype(F32) * pb
    pc = jnp.dot(yc_ref[...], wb_ref[r_b:, :], preferred_element_type=F32)
    merged += gc_ref[...].astype(F32) * pc
    o_ref[...] = x_ref[...] + jnp.dot(merged.astype(BF16), wo_ref[...], preferred_element_type=F32)


def _merge(x, ya, yb, yc, proj, wb, wo, *, tm):
    n, d = x.shape
    row = lambda i: (i, 0)
    const = lambda i: (0, 0)
    return pl.pallas_call(
        _merge_kernel,
        grid=(n // tm,),
        in_specs=[
            pl.BlockSpec((tm, d), row),
            pl.BlockSpec((tm, ya.shape[1]), row),
            pl.BlockSpec((tm, yb.shape[1]), row),
            pl.BlockSpec((tm, yc.shape[1]), row),
            pl.BlockSpec((tm, d), lambda i: (i, 0)),
            pl.BlockSpec((tm, d), lambda i: (i, 1)),
            pl.BlockSpec((tm, d), lambda i: (i, 2)),
            pl.BlockSpec(wb.shape, const),
            pl.BlockSpec(wo.shape, const),
        ],
        out_specs=pl.BlockSpec((tm, d), row),
        out_shape=jax.ShapeDtypeStruct((n, d), F32),
        compiler_params=_params(("parallel",), 56),
        name="merge",
    )(x, ya, yb, yc, proj, proj, proj, wb, wo)


def _rope_tables(seq_len, head_dim):
    pos = jnp.arange(seq_len, dtype=F32)
    inv = ROPE_THETA ** (-jnp.arange(0, head_dim, 2, dtype=F32) / head_dim)
    ang = pos[:, None] * inv[None, :]
    cos, sin = jnp.cos(ang), jnp.sin(ang)
    reps = LANES // head_dim
    return (jnp.tile(jnp.concatenate([cos, cos], axis=1), (1, reps)),
            jnp.tile(jnp.concatenate([-sin, sin], axis=1), (1, reps)))


def _s5_tables(lam_re, lam_im, log_dt, b_re, b_im, c_re, c_im):
    n_groups, n_state, group = b_re.shape
    s_dim = n_groups * n_state
    eye = jnp.eye(n_groups, dtype=F32)
    bmat = jnp.concatenate([
        jnp.einsum('gk,gnh->ghkn', eye, b_re).reshape(n_groups * group, s_dim),
        jnp.einsum('gk,gnh->ghkn', eye, b_im).reshape(n_groups * group, s_dim)], axis=1)
    cmats, tabs = [], []
    row = jnp.arange(SUBLANES)[:, None]
    for d in range(2):
        lr, li = lam_re[d], lam_im[d]
        dt = jnp.exp(log_dt[d])[:, None]
        mag = jnp.exp(dt * lr)
        ar, ai = mag * jnp.cos(dt * li), mag * jnp.sin(dt * li)
        denom = lr * lr + li * li
        nr = ar - 1.0
        kr = (nr * lr + ai * li) / denom
        ki = (ai * lr - nr * li) / denom
        er = c_re * kr[:, None, :] - c_im * ki[:, None, :]
        ei = c_re * ki[:, None, :] + c_im * kr[:, None, :]
        cmats.append(jnp.concatenate([
            jnp.einsum('gk,ghn->gnkh', eye, er).reshape(s_dim, n_groups * group),
            jnp.einsum('gk,ghn->gnkh', eye, -ei).reshape(s_dim, n_groups * group)], axis=0))
        pows = [(ar.reshape(-1), ai.reshape(-1))]
        for _ in range(SUBLANES - 1):
            pr, pi = pows[-1]
            pows.append((pr * pows[0][0] - pi * pows[0][1], pr * pows[0][1] + pi * pows[0][0]))
        rows = []
        for sh in (1, 2, 4):
            keep = (row >= sh) if d == 0 else (row < SUBLANES - sh)
            rows += [jnp.where(keep, pows[sh - 1][0][None, :], 0.0), jnp.where(keep, pows[sh - 1][1][None, :], 0.0)]
        order = jnp.arange(SUBLANES) if d == 0 else jnp.arange(SUBLANES)[::-1]
        rows += [jnp.stack([p[0] for p in pows])[order], jnp.stack([p[1] for p in pows])[order]]
        tabs.append(jnp.stack(rows))
    return bmat.astype(BF16), jnp.stack(tabs), jnp.stack(cmats).astype(BF16)


def kernel(x, norm_w, ffn_w13, ffn_w2, w_in, s5_lambda_re, s5_lambda_im, s5_log_dt, s5_b_re, s5_b_im, s5_c_re, s5_c_im, s5_d, s5_w_glu, s5_b_glu, diff_lambda, diff_subln, conv_w, w_branch, w_gate, b_gate, w_out, final_norm):
    batch, seq_len, d_model = x.shape
    depth = norm_w.shape[0]
    d_ff = ffn_w2.shape[2]
    d_ssm = s5_d.shape[-1]
    d_conv = conv_w.shape[-1]
    d_attn = (w_in.shape[-1] - d_ssm - 3 * d_conv) // 3
    head_dim = diff_lambda.shape[-1]
    n_heads = d_attn // (2 * head_dim)
    n = batch * seq_len
    assert 2 * head_dim == LANES and d_ssm == d_conv and d_attn == 2 * d_ssm

    tm_ffn = _tile(n, 512)
    tf = min(512, d_ff)
    tm_proj = _tile(seq_len, 1024)
    tn_proj = d_ssm
    tm_merge = _tile(n, 256)
    tq = _tile(seq_len, 256)
    t_chunk = _tile(seq_len, 512)

    cos, sin = _rope_tables(seq_len, head_dim)
    xf = x.reshape(n, d_model)
    gate_cols = N_BRANCH * d_model
    u_col = gate_cols // d_ssm
    q_col = (gate_cols + d_ssm) // LANES
    k_col = q_col + d_attn // LANES
    v_col = k_col + d_attn // LANES
    bg_col = (gate_cols + d_ssm + 3 * d_attn) // d_conv
    w13_bf = ffn_w13.astype(BF16)
    w2_bf = ffn_w2.astype(BF16)
    b_gate3 = b_gate[:, None, :]
    fg = final_norm[None, :]

    for l in range(depth):
        lambda_init = 0.8 - 0.6 * math.exp(-0.3 * l)
        bmat, tabs, cmat = _s5_tables(s5_lambda_re[l], s5_lambda_im[l], s5_log_dt[l],
                                      s5_b_re[l], s5_b_im[l], s5_c_re[l], s5_c_im[l])

        xf = _ffn(xf, norm_w[l, 0][None, :], w13_bf, w2_bf, fg, layer=l, idx=0, final=False, tm=tm_ffn, tf=tf)

        proj = _proj(xf, norm_w[l, 1][None, :], w_gate, b_gate3, w_in, cos, sin, layer=l, seq_len=seq_len,
                     n_attn=d_attn // tn_proj, q_scale=head_dim ** -0.5 * math.log2(math.e),
                     tm=tm_proj, tn=tn_proj)
        proj3 = proj.reshape(batch, seq_len, proj.shape[1])

        ys = _s5_scan(proj, bmat, tabs, cmat, batch=batch, seq_len=seq_len, u_col=u_col, d_ssm=d_ssm,
                      tc=t_chunk)
        y_a = _s5_glu(ys, proj, s5_d[l][None, :], s5_w_glu[l].astype(BF16), s5_b_glu[l][None, :],
                      u_col=u_col, tm=_tile(n, 1024))
        y_b = _attention(proj3, diff_lambda[l], diff_subln[l][None, :], q_col=q_col, k_col=k_col, v_col=v_col,
                         n_heads=n_heads, head_dim=head_dim, lambda_init=lambda_init, tq=tq)
        y_c = _short_conv(proj3, conv_w[l], bg_col=bg_col, d_conv=d_conv)

        xf = _merge(xf, y_a, y_b.reshape(n, d_attn), y_c.reshape(n, d_conv), proj,
                    w_branch[l].astype(BF16), w_out[l].astype(BF16), tm=tm_merge)

        xf = _ffn(xf, norm_w[l, 2][None, :], w13_bf, w2_bf, fg, layer=l, idx=1, final=(l == depth - 1),
                  tm=tm_ffn, tf=tf)
    return xf.reshape(batch, seq_len, d_model)
```

```python
import functools
import math

import jax
import jax.numpy as jnp
from jax import lax
from jax.experimental import pallas as pl
from jax.experimental.pallas import tpu as pltpu

NORM_EPS = 1e-6
ROPE_THETA = 10000.0
N_BRANCH = 3
LANES = 128
SUBLANES = 8
MIB = 1024 * 1024
F32 = jnp.float32
BF16 = jnp.bfloat16


def _rms(x, g):
    return x * lax.rsqrt(jnp.mean(x * x, axis=-1, keepdims=True) + NORM_EPS) * g


def _round_up(a, b):
    return (a + b - 1) // b * b


def _tile(n, want):
    t = min(n, want)
    while n % t:
        t -= 1
    return t


def _params(sem, vmem_mib):
    return pltpu.CompilerParams(dimension_semantics=sem, vmem_limit_bytes=vmem_mib * MIB)


def _ffn_kernel(x_ref, g_ref, wa_ref, *rest, final, d_ff, n_units):
    wb_refs = rest[:n_units]
    w2_refs = rest[n_units:2 * n_units]
    fg_ref, o_ref, h_ref = rest[2 * n_units:]
    j = pl.program_id(1)
    tf = wa_ref.shape[1]

    @pl.when(j == 0)
    def _():
        h_ref[...] = _rms(x_ref[...], g_ref[...]).astype(BF16)
        o_ref[...] = jnp.zeros_like(o_ref)

    h = h_ref[...]
    a = jnp.dot(h, wa_ref[...].astype(BF16), preferred_element_type=F32)
    wb = jnp.concatenate([r[...].astype(BF16) for r in wb_refs], axis=1)
    b = jnp.dot(h, wb, preferred_element_type=F32)
    col = j * tf + lax.broadcasted_iota(jnp.int32, a.shape, 1)
    act = jnp.where(col < d_ff, a * jax.nn.sigmoid(a) * b, 0.0).astype(BF16)
    w2 = jnp.concatenate([r[...].astype(BF16) for r in w2_refs], axis=0)
    o_ref[...] += jnp.dot(act, w2, preferred_element_type=F32)

    @pl.when(j == pl.num_programs(1) - 1)
    def _():
        y = x_ref[...] + 0.5 * o_ref[...]
        if final:
            y = _rms(y, fg_ref[...])
        o_ref[...] = y


def _ffn(x, g, w13, w2, fg, *, layer, idx, final, tm, tf):
    n, d = x.shape
    d_ff = w2.shape[2]
    assert d_ff % LANES == 0 and tf % LANES == 0 and 2 * d_ff >= _round_up(d_ff, tf)
    n_units = tf // LANES
    ff_units = d_ff // LANES

    def wb_spec(r):
        return pl.BlockSpec((None, None, d, LANES), lambda i, j: (
            layer, idx, 0, ff_units + jnp.minimum(j * n_units + r, ff_units - 1)))

    def w2_spec(r):
        return pl.BlockSpec((None, None, LANES, d), lambda i, j: (
            layer, idx, jnp.minimum(j * n_units + r, ff_units - 1), 0))

    return pl.pallas_call(
        functools.partial(_ffn_kernel, final=final, d_ff=d_ff, n_units=n_units),
        grid=(n // tm, pl.cdiv(d_ff, tf)),
        in_specs=[
            pl.BlockSpec((tm, d), lambda i, j: (i, 0)),
            pl.BlockSpec((1, d), lambda i, j: (0, 0)),
            pl.BlockSpec((None, None, d, tf), lambda i, j: (layer, idx, 0, j)),
            *[wb_spec(r) for r in range(n_units)],
            *[w2_spec(r) for r in range(n_units)],
            pl.BlockSpec((1, d), lambda i, j: (0, 0)),
        ],
        out_specs=pl.BlockSpec((tm, d), lambda i, j: (i, 0)),
        out_shape=jax.ShapeDtypeStruct((n, d), F32),
        scratch_shapes=[pltpu.VMEM((tm, d), BF16)],
        compiler_params=_params(("parallel", "arbitrary"), 60),
        name="ffn",
    )(x, g, w13, *([w13] * n_units), *([w2] * n_units), fg)


def _rope(acc, cos, sin, scale):
    lane = lax.broadcasted_iota(jnp.int32, cos.shape, 1)
    first_half = (lane & (LANES // 4)) == 0
    out = []
    for c in range(acc.shape[1] // LANES):
        blk = acc[:, c * LANES:(c + 1) * LANES]
        partner = jnp.where(first_half,
                            pltpu.roll(blk, LANES - LANES // 4, axis=1),
                            pltpu.roll(blk, LANES // 4, axis=1))
        out.append((blk * cos + partner * sin) * scale)
    return jnp.concatenate(out, axis=1)


def _proj_kernel(x_ref, g_ref, wg_ref, wi_ref, b_ref, cos_ref, sin_ref, o_ref, h_ref, *,
                 n_gate, n_attn, q_scale, chunk):
    j = pl.program_id(1)
    tn = o_ref.shape[1]

    @pl.when(j == 0)
    def _():
        h_ref[...] = _rms(x_ref[...], g_ref[...]).astype(BF16)

    def run(w_ref, epilogue):
        for c in range(tn // chunk):
            cols = slice(c * chunk, (c + 1) * chunk)
            acc = jnp.dot(h_ref[...], w_ref[:, cols].astype(BF16), preferred_element_type=F32)
            o_ref[:, cols] = epilogue(acc, cols).astype(o_ref.dtype)

    q_lo = n_gate + 1
    k_lo = q_lo + n_attn

    @pl.when(j < n_gate)
    def _():
        run(wg_ref, lambda acc, cols: jax.nn.sigmoid(acc + b_ref[:, cols]))

    @pl.when((j >= q_lo) & (j < k_lo))
    def _():
        run(wi_ref, lambda acc, cols: _rope(acc, cos_ref[...], sin_ref[...], q_scale))

    @pl.when((j >= k_lo) & (j < k_lo + n_attn))
    def _():
        run(wi_ref, lambda acc, cols: _rope(acc, cos_ref[...], sin_ref[...], 1.0))

    @pl.when((j == n_gate) | (j >= k_lo + n_attn))
    def _():
        run(wi_ref, lambda acc, cols: acc)


def _proj(x, g, w_gate, b_gate, w_in, cos, sin, *, layer, seq_len, n_attn, q_scale, tm, tn):
    n, d = x.shape
    n_gate = w_gate.shape[2] // tn
    n_in = w_in.shape[2] // tn
    pos_blocks = seq_len // tm
    gate_blk = lambda i, j: (layer, 0, jnp.minimum(j, n_gate - 1))
    return pl.pallas_call(
        functools.partial(_proj_kernel, n_gate=n_gate, n_attn=n_attn, q_scale=q_scale,
                          chunk=min(tn, 2 * LANES)),
        grid=(n // tm, n_gate + n_in),
        in_specs=[
            pl.BlockSpec((tm, d), lambda i, j: (i, 0)),
            pl.BlockSpec((1, d), lambda i, j: (0, 0)),
            pl.BlockSpec((None, d, tn), gate_blk),
            pl.BlockSpec((None, d, tn), lambda i, j: (layer, 0, jnp.maximum(j - n_gate, 0))),
            pl.BlockSpec((None, 1, tn), gate_blk),
            pl.BlockSpec((tm, LANES), lambda i, j: (i % pos_blocks, 0)),
            pl.BlockSpec((tm, LANES), lambda i, j: (i % pos_blocks, 0)),
        ],
        out_specs=pl.BlockSpec((tm, tn), lambda i, j: (i, j)),
        out_shape=jax.ShapeDtypeStruct((n, (n_gate + n_in) * tn), BF16),
        scratch_shapes=[pltpu.VMEM((tm, d), BF16)],
        compiler_params=_params(("parallel", "arbitrary"), 52),
        name="proj",
    )(x, g, w_gate, w_in, b_gate, cos, sin)


def _scan_chunk(bu_ref, tab_ref, carry_ref, *, reverse, lane_group):
    t_len, s2 = bu_ref.shape
    s_dim = s2 // 2
    n_tiles = t_len // SUBLANES
    shifts = (1, 2, 4)
    for lg in range(s_dim // lane_group):
        lo = lg * lane_group
        re_cols = pl.ds(lo, lane_group)
        im_cols = pl.ds(s_dim + lo, lane_group)

        def body(it, carry):
            c_re, c_im = carry
            tile = (n_tiles - 1 - it) if reverse else it
            rows = pl.ds(pl.multiple_of(tile * SUBLANES, SUBLANES), SUBLANES)
            x_re = bu_ref[rows, re_cols]
            x_im = bu_ref[rows, im_cols]
            for lvl, sh in enumerate(shifts):
                a_re = tab_ref[2 * lvl, :, re_cols]
                a_im = tab_ref[2 * lvl + 1, :, re_cols]
                roll_by = (SUBLANES - sh) if reverse else sh
                r_re = pltpu.roll(x_re, roll_by, axis=0)
                r_im = pltpu.roll(x_im, roll_by, axis=0)
                x_re, x_im = (x_re + a_re * r_re - a_im * r_im,
                              x_im + a_re * r_im + a_im * r_re)
            p_re = tab_ref[6, :, re_cols]
            p_im = tab_ref[7, :, re_cols]
            x_re, x_im = (x_re + p_re * c_re - p_im * c_im,
                          x_im + p_re * c_im + p_im * c_re)
            bu_ref[rows, re_cols] = x_re
            bu_ref[rows, im_cols] = x_im
            edge = 0 if reverse else SUBLANES - 1
            n_re = jnp.broadcast_to(x_re[edge:edge + 1, :], x_re.shape)
            n_im = jnp.broadcast_to(x_im[edge:edge + 1, :], x_im.shape)
            return n_re, n_im

        c0 = (carry_ref[0, :, re_cols], carry_ref[1, :, re_cols])
        c_re, c_im = lax.fori_loop(0, n_tiles, body, c0, unroll=2)
        carry_ref[0, :, re_cols] = c_re
        carry_ref[1, :, re_cols] = c_im


def _s5_kernel(u_ref, bmat_ref, tab_ref, cmat_ref, y_ref, bu_ref, carry_ref, *, lane_group):
    d = pl.program_id(1)
    c = pl.program_id(2)

    @pl.when(c == 0)
    def _():
        carry_ref[...] = jnp.zeros_like(carry_ref)

    bu_ref[...] = jnp.dot(u_ref[...], bmat_ref[...], preferred_element_type=F32)

    @pl.when(d == 0)
    def _():
        _scan_chunk(bu_ref, tab_ref.at[0], carry_ref, reverse=False, lane_group=lane_group)

    @pl.when(d == 1)
    def _():
        _scan_chunk(bu_ref, tab_ref.at[0], carry_ref, reverse=True, lane_group=lane_group)

    y_ref[0] = jnp.dot(bu_ref[...].astype(BF16), cmat_ref[0], preferred_element_type=F32)


def _s5_scan(proj, bmat, tabs, cmat, *, batch, seq_len, u_col, d_ssm, tc):
    n = proj.shape[0]
    n_chunks = seq_len // tc
    s2 = bmat.shape[1]
    s_dim = s2 // 2
    lane_group = min(s_dim, 4 * LANES)

    def chunk_row(b, d, c):
        return b * n_chunks + jnp.where(d == 0, c, n_chunks - 1 - c)

    return pl.pallas_call(
        functools.partial(_s5_kernel, lane_group=lane_group),
        grid=(batch, 2, n_chunks),
        in_specs=[
            pl.BlockSpec((tc, d_ssm), lambda b, d, c: (chunk_row(b, d, c), u_col)),
            pl.BlockSpec((d_ssm, s2), lambda b, d, c: (0, 0)),
            pl.BlockSpec((1, 8, SUBLANES, s_dim), lambda b, d, c: (d, 0, 0, 0)),
            pl.BlockSpec((1, s2, d_ssm), lambda b, d, c: (d, 0, 0)),
        ],
        out_specs=pl.BlockSpec((1, tc, d_ssm), lambda b, d, c: (d, chunk_row(b, d, c), 0)),
        out_shape=jax.ShapeDtypeStruct((2, n, d_ssm), F32),
        scratch_shapes=[pltpu.VMEM((tc, s2), F32), pltpu.VMEM((2, SUBLANES, s_dim), F32)],
        compiler_params=_params(("parallel", "arbitrary", "arbitrary"), 52),
        name="s5_scan",
    )(proj, bmat, tabs, cmat)


def _glu_kernel(ys_ref, u_ref, dskip_ref, w_ref, b_ref, o_ref):
    y = ys_ref[0] + ys_ref[1] + dskip_ref[...] * u_ref[...].astype(F32)
    y = jax.nn.gelu(y)
    z = jnp.dot(y.astype(BF16), w_ref[...], preferred_element_type=F32) + b_ref[...]
    o_ref[...] = (y * jax.nn.sigmoid(z)).astype(o_ref.dtype)


def _s5_glu(ys, proj, dskip, w, b, *, u_col, tm):
    _, n, d_ssm = ys.shape
    return pl.pallas_call(
        _glu_kernel,
        grid=(n // tm,),
        in_specs=[
            pl.BlockSpec((2, tm, d_ssm), lambda i: (0, i, 0)),
            pl.BlockSpec((tm, d_ssm), lambda i: (i, u_col)),
            pl.BlockSpec((1, d_ssm), lambda i: (0, 0)),
            pl.BlockSpec((d_ssm, d_ssm), lambda i: (0, 0)),
            pl.BlockSpec((1, d_ssm), lambda i: (0, 0)),
        ],
        out_specs=pl.BlockSpec((tm, d_ssm), lambda i: (i, 0)),
        out_shape=jax.ShapeDtypeStruct((n, d_ssm), BF16),
        compiler_params=_params(("parallel",), 40),
        name="s5_glu",
    )(ys, proj, dskip, w, b)


def _attn_kernel(q_ref, k_ref, v_ref, lam_ref, sg_ref, o_ref, *, lambda_init, head_dim, kc):
    q = q_ref[0]
    tq = q.shape[0]
    lane = lax.broadcasted_iota(jnp.int32, q.shape, 1)
    zero = jnp.zeros_like(q)
    qq = jnp.concatenate([jnp.where(lane < head_dim, q, zero), jnp.where(lane >= head_dim, q, zero)], axis=0)
    n_chunks = k_ref.shape[1] // kc

    def scores(c):
        return lax.dot_general(k_ref[0, pl.ds(c * kc, kc), :], qq, (((1,), (1,)), ((), ())),
                               preferred_element_type=F32)

    score_lead = 2
    m = l = acc = None
    s_q = [scores(c) for c in range(min(score_lead, n_chunks))]
    pending = None

    def value_stage(acc, item):
        c, e_b, alpha = item
        pv = lax.dot_general(v_ref[0, pl.ds(c * kc, kc), :], e_b, (((0,), (0,)), ((), ())),
                             preferred_element_type=F32)
        return pv if acc is None else acc * alpha + pv

    for c in range(n_chunks):
        if c + score_lead < n_chunks:
            s_q.append(scores(c + score_lead))
        s = s_q[c]
        m_c = jnp.max(s, axis=0, keepdims=True)
        m_new = m_c if m is None else jnp.maximum(m, m_c)
        e = jnp.exp2(s - m_new)
        l_c = jnp.sum(e, axis=0, keepdims=True)
        alpha = None if m is None else jnp.exp2(m - m_new)
        l = l_c if m is None else l * alpha + l_c
        if pending is not None:
            acc = value_stage(acc, pending)
        pending = (c, e.astype(BF16), alpha)
        m = m_new
    acc = value_stage(acc, pending)
    lv = lam_ref[...]
    lam = (jnp.exp(jnp.sum(lv[0:1] * lv[1:2], axis=-1, keepdims=True))
           - jnp.exp(jnp.sum(lv[2:3] * lv[3:4], axis=-1, keepdims=True)) + lambda_init)
    o_t = acc[:, 0:tq] * (1.0 / l[:, 0:tq]) - acc[:, tq:2 * tq] * (lam / l[:, tq:2 * tq])
    o_ref[0] = (_rms(o_t.T, sg_ref[...]) * (1.0 - lambda_init)).astype(o_ref.dtype)


def _attention(proj3, lam_vec, subln, *, q_col, k_col, v_col, n_heads, head_dim, lambda_init, tq):
    batch, seq_len, _ = proj3.shape
    vd = 2 * head_dim
    return pl.pallas_call(
        functools.partial(_attn_kernel, lambda_init=lambda_init, head_dim=head_dim, kc=_tile(seq_len, 512)),
        grid=(batch, n_heads, seq_len // tq),
        in_specs=[
            pl.BlockSpec((1, tq, vd), lambda b, h, i: (b, i, q_col + h)),
            pl.BlockSpec((1, seq_len, vd), lambda b, h, i: (b, 0, k_col + h)),
            pl.BlockSpec((1, seq_len, vd), lambda b, h, i: (b, 0, v_col + h)),
            pl.BlockSpec((4, head_dim), lambda b, h, i: (0, 0)),
            pl.BlockSpec((1, vd), lambda b, h, i: (0, 0)),
        ],
        out_specs=pl.BlockSpec((1, tq, vd), lambda b, h, i: (b, i, h)),
        out_shape=jax.ShapeDtypeStruct((batch, seq_len, n_heads * vd), BF16),
        compiler_params=_params(("parallel", "parallel", "arbitrary"), 52),
        name="diff_attn",
    )(proj3, proj3, proj3, lam_vec, subln)


def _conv_kernel(bg_ref, cg_ref, xv_ref, w_ref, o_ref):
    z = cg_ref[0].astype(F32) * xv_ref[0].astype(F32)
    seq_len = z.shape[0]
    row = lax.broadcasted_iota(jnp.int32, z.shape, 0)
    z_prev = jnp.where(row == 0, 0.0, pltpu.roll(z, 1, axis=0))
    z_next = jnp.where(row == seq_len - 1, 0.0, pltpu.roll(z, seq_len - 1, axis=0))
    w = w_ref[...]
    zc = w[0:1] * z_prev + w[1:2] * z + w[2:3] * z_next
    o_ref[0] = (bg_ref[0].astype(F32) * zc).astype(o_ref.dtype)


def _short_conv(proj3, conv_w, *, bg_col, d_conv):
    batch, seq_len, _ = proj3.shape
    tc = min(d_conv, LANES)
    per = d_conv // tc
    return pl.pallas_call(
        _conv_kernel,
        grid=(batch, per),
        in_specs=[
            pl.BlockSpec((1, seq_len, tc), lambda b, j: (b, 0, bg_col * per + j)),
            pl.BlockSpec((1, seq_len, tc), lambda b, j: (b, 0, (bg_col + 1) * per + j)),
            pl.BlockSpec((1, seq_len, tc), lambda b, j: (b, 0, (bg_col + 2) * per + j)),
            pl.BlockSpec((conv_w.shape[0], tc), lambda b, j: (0, j)),
        ],
        out_specs=pl.BlockSpec((1, seq_len, tc), lambda b, j: (b, 0, j)),
        out_shape=jax.ShapeDtypeStruct((batch, seq_len, d_conv), BF16),
        compiler_params=_params(("parallel", "parallel"), 40),
        name="short_conv",
    )(proj3, proj3, proj3, conv_w)


def _merge_kernel(x_ref, ya_ref, yb_ref, yc_ref, ga_ref, gb_ref, gc_ref, wb_ref, wo_ref, o_ref):
    r_a = ya_ref.shape[1]
    r_b = r_a + yb_ref.shape[1]
    pa = jnp.dot(ya_ref[...], wb_ref[0:r_a, :], preferred_element_type=F32)
    merged = ga_ref[...].astype(F32) * pa
    pb = jnp.dot(yb_ref[...], wb_ref[r_a:r_b, :], preferred_element_type=F32)
    merged += gb_ref[...].astype(F32) * pb
    pc = jnp.dot(yc_ref[...], wb_ref[r_b:, :], preferred_element_type=F32)
    merged += gc_ref[...].astype(F32) * pc
    o_ref[...] = x_ref[...] + jnp.dot(merged.astype(BF16), wo_ref[...], preferred_element_type=F32)


def _merge(x, ya, yb, yc, proj, wb, wo, *, tm):
    n, d = x.shape
    row = lambda i: (i, 0)
    const = lambda i: (0, 0)
    return pl.pallas_call(
        _merge_kernel,
        grid=(n // tm,),
        in_specs=[
            pl.BlockSpec((tm, d), row),
            pl.BlockSpec((tm, ya.shape[1]), row),
            pl.BlockSpec((tm, yb.shape[1]), row),
            pl.BlockSpec((tm, yc.shape[1]), row),
            pl.BlockSpec((tm, d), lambda i: (i, 0)),
            pl.BlockSpec((tm, d), lambda i: (i, 1)),
            pl.BlockSpec((tm, d), lambda i: (i, 2)),
            pl.BlockSpec(wb.shape, const),
            pl.BlockSpec(wo.shape, const),
        ],
        out_specs=pl.BlockSpec((tm, d), row),
        out_shape=jax.ShapeDtypeStruct((n, d), F32),
        compiler_params=_params(("parallel",), 56),
        name="merge",
    )(x, ya, yb, yc, proj, proj, proj, wb, wo)


def _rope_tables(seq_len, head_dim):
    pos = jnp.arange(seq_len, dtype=F32)
    inv = ROPE_THETA ** (-jnp.arange(0, head_dim, 2, dtype=F32) / head_dim)
    ang = pos[:, None] * inv[None, :]
    cos, sin = jnp.cos(ang), jnp.sin(ang)
    reps = LANES // head_dim
    return (jnp.tile(jnp.concatenate([cos, cos], axis=1), (1, reps)),
            jnp.tile(jnp.concatenate([-sin, sin], axis=1), (1, reps)))


def _s5_tables(lam_re, lam_im, log_dt, b_re, b_im, c_re, c_im):
    n_groups, n_state, group = b_re.shape
    s_dim = n_groups * n_state
    eye = jnp.eye(n_groups, dtype=F32)
    bmat = jnp.concatenate([
        jnp.einsum('gk,gnh->ghkn', eye, b_re).reshape(n_groups * group, s_dim),
        jnp.einsum('gk,gnh->ghkn', eye, b_im).reshape(n_groups * group, s_dim)], axis=1)
    cmats, tabs = [], []
    row = jnp.arange(SUBLANES)[:, None]
    for d in range(2):
        lr, li = lam_re[d], lam_im[d]
        dt = jnp.exp(log_dt[d])[:, None]
        mag = jnp.exp(dt * lr)
        ar, ai = mag * jnp.cos(dt * li), mag * jnp.sin(dt * li)
        denom = lr * lr + li * li
        nr = ar - 1.0
        kr = (nr * lr + ai * li) / denom
        ki = (ai * lr - nr * li) / denom
        er = c_re * kr[:, None, :] - c_im * ki[:, None, :]
        ei = c_re * ki[:, None, :] + c_im * kr[:, None, :]
        cmats.append(jnp.concatenate([
            jnp.einsum('gk,ghn->gnkh', eye, er).reshape(s_dim, n_groups * group),
            jnp.einsum('gk,ghn->gnkh', eye, -ei).reshape(s_dim, n_groups * group)], axis=0))
        pows = [(ar.reshape(-1), ai.reshape(-1))]
        for _ in range(SUBLANES - 1):
            pr, pi = pows[-1]
            pows.append((pr * pows[0][0] - pi * pows[0][1], pr * pows[0][1] + pi * pows[0][0]))
        rows = []
        for sh in (1, 2, 4):
            keep = (row >= sh) if d == 0 else (row < SUBLANES - sh)
            rows += [jnp.where(keep, pows[sh - 1][0][None, :], 0.0), jnp.where(keep, pows[sh - 1][1][None, :], 0.0)]
        order = jnp.arange(SUBLANES) if d == 0 else jnp.arange(SUBLANES)[::-1]
        rows += [jnp.stack([p[0] for p in pows])[order], jnp.stack([p[1] for p in pows])[order]]
        tabs.append(jnp.stack(rows))
    return bmat.astype(BF16), jnp.stack(tabs), jnp.stack(cmats).astype(BF16)


def kernel(x, norm_w, ffn_w13, ffn_w2, w_in, s5_lambda_re, s5_lambda_im, s5_log_dt, s5_b_re, s5_b_im, s5_c_re, s5_c_im, s5_d, s5_w_glu, s5_b_glu, diff_lambda, diff_subln, conv_w, w_branch, w_gate, b_gate, w_out, final_norm):
    batch, seq_len, d_model = x.shape
    depth = norm_w.shape[0]
    d_ff = ffn_w2.shape[2]
    d_ssm = s5_d.shape[-1]
    d_conv = conv_w.shape[-1]
    d_attn = (w_in.shape[-1] - d_ssm - 3 * d_conv) // 3
    head_dim = diff_lambda.shape[-1]
    n_heads = d_attn // (2 * head_dim)
    n = batch * seq_len
    assert 2 * head_dim == LANES and d_ssm == d_conv and d_attn == 2 * d_ssm

    tm_ffn = _tile(n, 1024)
    tf = min(256, d_ff)
    tm_proj = _tile(seq_len, 1024)
    tn_proj = d_ssm
    tm_merge = _tile(n, 256)
    tq = _tile(seq_len, 256)
    t_chunk = _tile(seq_len, 512)

    cos, sin = _rope_tables(seq_len, head_dim)
    xf = x.reshape(n, d_model)
    gate_cols = N_BRANCH * d_model
    u_col = gate_cols // d_ssm
    q_col = (gate_cols + d_ssm) // LANES
    k_col = q_col + d_attn // LANES
    v_col = k_col + d_attn // LANES
    bg_col = (gate_cols + d_ssm + 3 * d_attn) // d_conv
    w13_bf = ffn_w13
    w2_bf = ffn_w2
    b_gate3 = b_gate[:, None, :]
    fg = final_norm[None, :]

    for l in range(depth):
        lambda_init = 0.8 - 0.6 * math.exp(-0.3 * l)
        bmat, tabs, cmat = _s5_tables(s5_lambda_re[l], s5_lambda_im[l], s5_log_dt[l],
                                      s5_b_re[l], s5_b_im[l], s5_c_re[l], s5_c_im[l])

        xf = _ffn(xf, norm_w[l, 0][None, :], w13_bf, w2_bf, fg, layer=l, idx=0, final=False, tm=tm_ffn, tf=tf)

        proj = _proj(xf, norm_w[l, 1][None, :], w_gate, b_gate3, w_in, cos, sin, layer=l, seq_len=seq_len,
                     n_attn=d_attn // tn_proj, q_scale=head_dim ** -0.5 * math.log2(math.e),
                     tm=tm_proj, tn=tn_proj)
        proj3 = proj.reshape(batch, seq_len, proj.shape[1])

        ys = _s5_scan(proj, bmat, tabs, cmat, batch=batch, seq_len=seq_len, u_col=u_col, d_ssm=d_ssm,
                      tc=t_chunk)
        y_a = _s5_glu(ys, proj, s5_d[l][None, :], s5_w_glu[l].astype(BF16), s5_b_glu[l][None, :],
                      u_col=u_col, tm=_tile(n, 1024))
        y_b = _attention(proj3, diff_lambda[l], diff_subln[l][None, :], q_col=q_col, k_col=k_col, v_col=v_col,
                         n_heads=n_heads, head_dim=head_dim, lambda_init=lambda_init, tq=tq)
        y_c = _short_conv(proj3, conv_w[l], bg_col=bg_col, d_conv=d_conv)

        xf = _merge(xf, y_a, y_b.reshape(n, d_attn), y_c.reshape(n, d_conv), proj,
                    w_branch[l].astype(BF16), w_out[l].astype(BF16), tm=tm_merge)

        xf = _ffn(xf, norm_w[l, 2][None, :], w13_bf, w2_bf, fg, layer=l, idx=1, final=(l == depth - 1),
                  tm=tm_ffn, tf=tf)
    return xf.reshape(batch, seq_len, d_model)
```

```python
import functools
import math

import jax
import jax.numpy as jnp
from jax import lax
from jax.experimental import pallas as pl
from jax.experimental.pallas import tpu as pltpu

NORM_EPS = 1e-6
ROPE_THETA = 10000.0
N_BRANCH = 3
LANES = 128
SUBLANES = 8
MIB = 1024 * 1024
F32 = jnp.float32
BF16 = jnp.bfloat16


def _rms(x, g):
    return x * lax.rsqrt(jnp.mean(x * x, axis=-1, keepdims=True) + NORM_EPS) * g


def _round_up(a, b):
    return (a + b - 1) // b * b


def _tile(n, want):
    t = min(n, want)
    while n % t:
        t -= 1
    return t


def _params(sem, vmem_mib):
    return pltpu.CompilerParams(dimension_semantics=sem, vmem_limit_bytes=vmem_mib * MIB)


def _ffn_kernel(x_ref, g_ref, wa_ref, *rest, post, d_ff, n_units):
    wb_refs = rest[:n_units]
    w2_refs = rest[n_units:2 * n_units]
    if post == "next":
        fg_ref, o_ref, hn_ref = rest[2 * n_units:]
        h_ref = hn_ref
    else:
        fg_ref, o_ref, h_ref = rest[2 * n_units:]
    j = pl.program_id(1)
    tf = wa_ref.shape[1]

    @pl.when(j == 0)
    def _():
        h_ref[...] = _rms(x_ref[...], g_ref[...]).astype(BF16)
        o_ref[...] = jnp.zeros_like(o_ref)

    h = h_ref[...]
    a = jnp.dot(h, wa_ref[...].astype(BF16), preferred_element_type=F32)
    wb = jnp.concatenate([r[...].astype(BF16) for r in wb_refs], axis=1)
    b = jnp.dot(h, wb, preferred_element_type=F32)
    col = j * tf + lax.broadcasted_iota(jnp.int32, a.shape, 1)
    act = jnp.where(col < d_ff, a * jax.nn.sigmoid(a) * b, 0.0).astype(BF16)
    w2 = jnp.concatenate([r[...].astype(BF16) for r in w2_refs], axis=0)
    o_ref[...] += jnp.dot(act, w2, preferred_element_type=F32)

    @pl.when(j == pl.num_programs(1) - 1)
    def _():
        y = x_ref[...] + 0.5 * o_ref[...]
        if post == "final":
            y = _rms(y, fg_ref[...])
        if post == "next":
            hn_ref[...] = _rms(y, fg_ref[...]).astype(BF16)
        o_ref[...] = y


def _ffn(x, g, w13, w2, fg, *, layer, idx, post, tm, tf):
    n, d = x.shape
    row_blk = pl.BlockSpec((tm, d), lambda i, j: (i, 0))
    out_specs, out_shape = row_blk, jax.ShapeDtypeStruct((n, d), F32)
    if post == "next":
        out_specs, out_shape = [row_blk, row_blk], [out_shape, jax.ShapeDtypeStruct((n, d), BF16)]
    d_ff = w2.shape[2]
    assert d_ff % LANES == 0 and tf % LANES == 0 and 2 * d_ff >= _round_up(d_ff, tf)
    n_units = tf // LANES
    ff_units = d_ff // LANES

    def wb_spec(r):
        return pl.BlockSpec((None, None, d, LANES), lambda i, j: (
            layer, idx, 0, ff_units + jnp.minimum(j * n_units + r, ff_units - 1)))

    def w2_spec(r):
        return pl.BlockSpec((None, None, LANES, d), lambda i, j: (
            layer, idx, jnp.minimum(j * n_units + r, ff_units - 1), 0))

    return pl.pallas_call(
        functools.partial(_ffn_kernel, post=post, d_ff=d_ff, n_units=n_units),
        grid=(n // tm, pl.cdiv(d_ff, tf)),
        in_specs=[
            row_blk,
            pl.BlockSpec((1, d), lambda i, j: (0, 0)),
            pl.BlockSpec((None, None, d, tf), lambda i, j: (layer, idx, 0, j)),
            *[wb_spec(r) for r in range(n_units)],
            *[w2_spec(r) for r in range(n_units)],
            pl.BlockSpec((1, d), lambda i, j: (0, 0)),
        ],
        out_specs=out_specs,
        out_shape=out_shape,
        scratch_shapes=[] if post == "next" else [pltpu.VMEM((tm, d), BF16)],
        compiler_params=_params(("parallel", "arbitrary"), 62),
        name="ffn",
    )(x, g, w13, *([w13] * n_units), *([w2] * n_units), fg)


def _rope(acc, cos, sin, scale):
    lane = lax.broadcasted_iota(jnp.int32, cos.shape, 1)
    first_half = (lane & (LANES // 4)) == 0
    out = []
    for c in range(acc.shape[1] // LANES):
        blk = acc[:, c * LANES:(c + 1) * LANES]
        partner = jnp.where(first_half,
                            pltpu.roll(blk, LANES - LANES // 4, axis=1),
                            pltpu.roll(blk, LANES // 4, axis=1))
        out.append((blk * cos + partner * sin) * scale)
    return jnp.concatenate(out, axis=1)


def _proj_kernel(h_ref, wg_ref, wi_ref, b_ref, cos_ref, sin_ref, o_ref, *,
                 n_gate, n_in, u_cols, attn_cols, q_scale, chunk):
    j = pl.program_id(1)
    tn = o_ref.shape[1]

    def run(w_ref, epilogue):
        for c in range(tn // chunk):
            cols = slice(c * chunk, (c + 1) * chunk)
            acc = jnp.dot(h_ref[...], w_ref[:, cols].astype(BF16), preferred_element_type=F32)
            o_ref[:, cols] = epilogue(acc, c).astype(o_ref.dtype)

    @pl.when(j < n_gate)
    def _():
        run(wg_ref, lambda acc, c: jax.nn.sigmoid(acc + b_ref[:, c * chunk:(c + 1) * chunk]))

    def in_epilogue(blk):
        def epilogue(acc, c):
            col = blk * tn + c * chunk
            if u_cols <= col < u_cols + attn_cols:
                return _rope(acc, cos_ref[...], sin_ref[...], q_scale)
            if u_cols + attn_cols <= col < u_cols + 2 * attn_cols:
                return _rope(acc, cos_ref[...], sin_ref[...], 1.0)
            return acc
        return epilogue

    for blk in range(n_in):
        @pl.when(j == n_gate + blk)
        def _(blk=blk):
            run(wi_ref, in_epilogue(blk))


def _proj(h, w_gate, b_gate, w_in, cos, sin, *, layer, seq_len, u_cols, attn_cols, q_scale, tm, tn):
    n, d = h.shape
    n_gate = w_gate.shape[2] // tn
    n_in = w_in.shape[2] // tn
    chunk = min(u_cols, 2 * LANES)
    assert u_cols % chunk == 0 and attn_cols % chunk == 0 and tn % chunk == 0
    pos_blocks = seq_len // tm
    gate_blk = lambda i, j: (layer, 0, jnp.minimum(j, n_gate - 1))
    return pl.pallas_call(
        functools.partial(_proj_kernel, n_gate=n_gate, n_in=n_in, u_cols=u_cols, attn_cols=attn_cols,
                          q_scale=q_scale, chunk=chunk),
        grid=(n // tm, n_gate + n_in),
        in_specs=[
            pl.BlockSpec((tm, d), lambda i, j: (i, 0)),
            pl.BlockSpec((None, d, tn), gate_blk),
            pl.BlockSpec((None, d, tn), lambda i, j: (layer, 0, jnp.maximum(j - n_gate, 0))),
            pl.BlockSpec((None, 1, tn), gate_blk),
            pl.BlockSpec((tm, LANES), lambda i, j: (i % pos_blocks, 0)),
            pl.BlockSpec((tm, LANES), lambda i, j: (i % pos_blocks, 0)),
        ],
        out_specs=pl.BlockSpec((tm, tn), lambda i, j: (i, j)),
        out_shape=jax.ShapeDtypeStruct((n, (n_gate + n_in) * tn), BF16),
        compiler_params=_params(("parallel", "arbitrary"), 56),
        name="proj",
    )(h, w_gate, w_in, b_gate, cos, sin)


def _scan_chunk(bu_ref, tab_ref, carry_ref, *, reverse, lane_group):
    t_len, s2 = bu_ref.shape
    s_dim = s2 // 2
    n_tiles = t_len // SUBLANES
    shifts = (1, 2, 4)
    for lg in range(s_dim // lane_group):
        lo = lg * lane_group
        re_cols = pl.ds(lo, lane_group)
        im_cols = pl.ds(s_dim + lo, lane_group)

        def body(it, carry):
            c_re, c_im = carry
            tile = (n_tiles - 1 - it) if reverse else it
            rows = pl.ds(pl.multiple_of(tile * SUBLANES, SUBLANES), SUBLANES)
            x_re = bu_ref[rows, re_cols]
            x_im = bu_ref[rows, im_cols]
            for lvl, sh in enumerate(shifts):
                a_re = tab_ref[2 * lvl, :, re_cols]
                a_im = tab_ref[2 * lvl + 1, :, re_cols]
                roll_by = (SUBLANES - sh) if reverse else sh
                r_re = pltpu.roll(x_re, roll_by, axis=0)
                r_im = pltpu.roll(x_im, roll_by, axis=0)
                x_re, x_im = (x_re + a_re * r_re - a_im * r_im,
                              x_im + a_re * r_im + a_im * r_re)
            p_re = tab_ref[6, :, re_cols]
            p_im = tab_ref[7, :, re_cols]
            x_re, x_im = (x_re + p_re * c_re - p_im * c_im,
                          x_im + p_re * c_im + p_im * c_re)
            bu_ref[rows, re_cols] = x_re
            bu_ref[rows, im_cols] = x_im
            edge = 0 if reverse else SUBLANES - 1
            n_re = jnp.broadcast_to(x_re[edge:edge + 1, :], x_re.shape)
            n_im = jnp.broadcast_to(x_im[edge:edge + 1, :], x_im.shape)
            return n_re, n_im

        c0 = (carry_ref[0, :, re_cols], carry_ref[1, :, re_cols])
        c_re, c_im = lax.fori_loop(0, n_tiles, body, c0, unroll=2)
        carry_ref[0, :, re_cols] = c_re
        carry_ref[1, :, re_cols] = c_im


def _s5_kernel(u_ref, bmat_ref, tab_ref, cmat_ref, y_ref, bu_ref, carry_ref, *, lane_group):
    d = pl.program_id(1)
    c = pl.program_id(2)

    @pl.when(c == 0)
    def _():
        carry_ref[...] = jnp.zeros_like(carry_ref)

    d_ssm, s2 = bmat_ref.shape
    s_dim = s2 // 2
    halves = [(slice(h * d_ssm // 2, (h + 1) * d_ssm // 2), slice(h * s_dim // 2, (h + 1) * s_dim // 2),
               slice(s_dim + h * s_dim // 2, s_dim + (h + 1) * s_dim // 2)) for h in range(2)]
    for ch, st_re, st_im in halves:
        u_half = u_ref[:, ch]
        bu_ref[:, st_re] = jnp.dot(u_half, bmat_ref[ch, st_re], preferred_element_type=F32)
        bu_ref[:, st_im] = jnp.dot(u_half, bmat_ref[ch, st_im], preferred_element_type=F32)

    @pl.when(d == 0)
    def _():
        _scan_chunk(bu_ref, tab_ref.at[0], carry_ref, reverse=False, lane_group=lane_group)

    @pl.when(d == 1)
    def _():
        _scan_chunk(bu_ref, tab_ref.at[0], carry_ref, reverse=True, lane_group=lane_group)

    for ch, st_re, st_im in halves:
        y_ref[0, :, ch] = (
            jnp.dot(bu_ref[:, st_re].astype(BF16), cmat_ref[0, st_re, ch], preferred_element_type=F32)
            + jnp.dot(bu_ref[:, st_im].astype(BF16), cmat_ref[0, st_im, ch], preferred_element_type=F32))


def _s5_scan(proj, bmat, tabs, cmat, *, batch, seq_len, u_col, d_ssm, tc):
    n = proj.shape[0]
    n_chunks = seq_len // tc
    s2 = bmat.shape[1]
    s_dim = s2 // 2
    lane_group = min(s_dim, 4 * LANES)

    def chunk_row(b, d, c):
        return b * n_chunks + jnp.where(d == 0, c, n_chunks - 1 - c)

    return pl.pallas_call(
        functools.partial(_s5_kernel, lane_group=lane_group),
        grid=(batch, 2, n_chunks),
        in_specs=[
            pl.BlockSpec((tc, d_ssm), lambda b, d, c: (chunk_row(b, d, c), u_col)),
            pl.BlockSpec((d_ssm, s2), lambda b, d, c: (0, 0)),
            pl.BlockSpec((1, 8, SUBLANES, s_dim), lambda b, d, c: (d, 0, 0, 0)),
            pl.BlockSpec((1, s2, d_ssm), lambda b, d, c: (d, 0, 0)),
        ],
        out_specs=pl.BlockSpec((1, tc, d_ssm), lambda b, d, c: (d, chunk_row(b, d, c), 0)),
        out_shape=jax.ShapeDtypeStruct((2, n, d_ssm), F32),
        scratch_shapes=[pltpu.VMEM((tc, s2), F32), pltpu.VMEM((2, SUBLANES, s_dim), F32)],
        compiler_params=_params(("parallel", "arbitrary", "arbitrary"), 52),
        name="s5_scan",
    )(proj, bmat, tabs, cmat)


def _glu_kernel(ys_ref, u_ref, dskip_ref, w_ref, b_ref, o_ref):
    y = ys_ref[0] + ys_ref[1] + dskip_ref[...] * u_ref[...].astype(F32)
    y = jax.nn.gelu(y)
    z = jnp.dot(y.astype(BF16), w_ref[...], preferred_element_type=F32) + b_ref[...]
    o_ref[...] = (y * jax.nn.sigmoid(z)).astype(o_ref.dtype)


def _s5_glu(ys, proj, dskip, w, b, *, u_col, tm):
    _, n, d_ssm = ys.shape
    return pl.pallas_call(
        _glu_kernel,
        grid=(n // tm,),
        in_specs=[
            pl.BlockSpec((2, tm, d_ssm), lambda i: (0, i, 0)),
            pl.BlockSpec((tm, d_ssm), lambda i: (i, u_col)),
            pl.BlockSpec((1, d_ssm), lambda i: (0, 0)),
            pl.BlockSpec((d_ssm, d_ssm), lambda i: (0, 0)),
            pl.BlockSpec((1, d_ssm), lambda i: (0, 0)),
        ],
        out_specs=pl.BlockSpec((tm, d_ssm), lambda i: (i, 0)),
        out_shape=jax.ShapeDtypeStruct((n, d_ssm), BF16),
        compiler_params=_params(("parallel",), 40),
        name="s5_glu",
    )(ys, proj, dskip, w, b)


def _attn_kernel(q_ref, k_ref, v_ref, lam_ref, sg_ref, o_ref, *, lambda_init, head_dim, kc):
    q = q_ref[0]
    tq = q.shape[0]
    lane = lax.broadcasted_iota(jnp.int32, q.shape, 1)
    zero = jnp.zeros_like(q)
    qq = jnp.concatenate([jnp.where(lane < head_dim, q, zero), jnp.where(lane >= head_dim, q, zero)], axis=0)
    n_chunks = k_ref.shape[1] // kc

    def scores(c):
        return lax.dot_general(k_ref[0, pl.ds(c * kc, kc), :], qq, (((1,), (1,)), ((), ())),
                               preferred_element_type=F32)

    score_lead = 2
    m = l = acc = None
    s_q = [scores(c) for c in range(min(score_lead, n_chunks))]
    pending = None

    def value_stage(acc, item):
        c, e_b, alpha = item
        pv = lax.dot_general(v_ref[0, pl.ds(c * kc, kc), :], e_b, (((0,), (0,)), ((), ())),
                             preferred_element_type=F32)
        return pv if acc is None else acc * alpha + pv

    for c in range(n_chunks):
        if c + score_lead < n_chunks:
            s_q.append(scores(c + score_lead))
        s = s_q[c]
        m_c = jnp.max(s, axis=0, keepdims=True)
        m_new = m_c if m is None else jnp.maximum(m, m_c)
        e = jnp.exp2(s - m_new)
        l_c = jnp.sum(e, axis=0, keepdims=True)
        alpha = None if m is None else jnp.exp2(m - m_new)
        l = l_c if m is None else l * alpha + l_c
        if pending is not None:
            acc = value_stage(acc, pending)
        pending = (c, e.astype(BF16), alpha)
        m = m_new
    acc = value_stage(acc, pending)
    lv = lam_ref[...]
    lam = (jnp.exp(jnp.sum(lv[0:1] * lv[1:2], axis=-1, keepdims=True))
           - jnp.exp(jnp.sum(lv[2:3] * lv[3:4], axis=-1, keepdims=True)) + lambda_init)
    o_t = acc[:, 0:tq] * (1.0 / l[:, 0:tq]) - acc[:, tq:2 * tq] * (lam / l[:, tq:2 * tq])
    o_ref[0] = (_rms(o_t.T, sg_ref[...]) * (1.0 - lambda_init)).astype(o_ref.dtype)


def _attention(proj3, lam_vec, subln, *, q_col, k_col, v_col, n_heads, head_dim, lambda_init, tq):
    batch, seq_len, _ = proj3.shape
    vd = 2 * head_dim
    return pl.pallas_call(
        functools.partial(_attn_kernel, lambda_init=lambda_init, head_dim=head_dim, kc=_tile(seq_len, 512)),
        grid=(batch, n_heads, seq_len // tq),
        in_specs=[
            pl.BlockSpec((1, tq, vd), lambda b, h, i: (b, i, q_col + h)),
            pl.BlockSpec((1, seq_len, vd), lambda b, h, i: (b, 0, k_col + h)),
            pl.BlockSpec((1, seq_len, vd), lambda b, h, i: (b, 0, v_col + h)),
            pl.BlockSpec((4, head_dim), lambda b, h, i: (0, 0)),
            pl.BlockSpec((1, vd), lambda b, h, i: (0, 0)),
        ],
        out_specs=pl.BlockSpec((1, tq, vd), lambda b, h, i: (b, i, h)),
        out_shape=jax.ShapeDtypeStruct((batch, seq_len, n_heads * vd), BF16),
        compiler_params=_params(("parallel", "parallel", "arbitrary"), 52),
        name="diff_attn",
    )(proj3, proj3, proj3, lam_vec, subln)


def _conv_kernel(bg_ref, cg_ref, xv_ref, w_ref, o_ref):
    z = cg_ref[0].astype(F32) * xv_ref[0].astype(F32)
    seq_len = z.shape[0]
    row = lax.broadcasted_iota(jnp.int32, z.shape, 0)
    z_prev = jnp.where(row == 0, 0.0, pltpu.roll(z, 1, axis=0))
    z_next = jnp.where(row == seq_len - 1, 0.0, pltpu.roll(z, seq_len - 1, axis=0))
    w = w_ref[...]
    zc = w[0:1] * z_prev + w[1:2] * z + w[2:3] * z_next
    o_ref[0] = (bg_ref[0].astype(F32) * zc).astype(o_ref.dtype)


def _short_conv(proj3, conv_w, *, bg_col, d_conv):
    batch, seq_len, _ = proj3.shape
    tc = min(d_conv, LANES)
    per = d_conv // tc
    return pl.pallas_call(
        _conv_kernel,
        grid=(batch, per),
        in_specs=[
            pl.BlockSpec((1, seq_len, tc), lambda b, j: (b, 0, bg_col * per + j)),
            pl.BlockSpec((1, seq_len, tc), lambda b, j: (b, 0, (bg_col + 1) * per + j)),
            pl.BlockSpec((1, seq_len, tc), lambda b, j: (b, 0, (bg_col + 2) * per + j)),
            pl.BlockSpec((conv_w.shape[0], tc), lambda b, j: (0, j)),
        ],
        out_specs=pl.BlockSpec((1, seq_len, tc), lambda b, j: (b, 0, j)),
        out_shape=jax.ShapeDtypeStruct((batch, seq_len, d_conv), BF16),
        compiler_params=_params(("parallel", "parallel"), 40),
        name="short_conv",
    )(proj3, proj3, proj3, conv_w)


def _merge_kernel(x_ref, ya_ref, yb_ref, yc_ref, ga_ref, gb_ref, gc_ref, wb_ref, wo_ref, o_ref):
    r_a = ya_ref.shape[1]
    r_b = r_a + yb_ref.shape[1]
    pa = jnp.dot(ya_ref[...], wb_ref[0:r_a, :], preferred_element_type=F32)
    merged = ga_ref[...].astype(F32) * pa
    pb = jnp.dot(yb_ref[...], wb_ref[r_a:r_b, :], preferred_element_type=F32)
    merged += gb_ref[...].astype(F32) * pb
    pc = jnp.dot(yc_ref[...], wb_ref[r_b:, :], preferred_element_type=F32)
    merged += gc_ref[...].astype(F32) * pc
    o_ref[...] = x_ref[...] + jnp.dot(merged.astype(BF16), wo_ref[...], preferred_element_type=F32)


def _merge(x, ya, yb, yc, proj, wb, wo, *, tm):
    n, d = x.shape
    row = lambda i: (i, 0)
    const = lambda i: (0, 0)
    return pl.pallas_call(
        _merge_kernel,
        grid=(n // tm,),
        in_specs=[
            pl.BlockSpec((tm, d), row),
            pl.BlockSpec((tm, ya.shape[1]), row),
            pl.BlockSpec((tm, yb.shape[1]), row),
            pl.BlockSpec((tm, yc.shape[1]), row),
            pl.BlockSpec((tm, d), lambda i: (i, 0)),
            pl.BlockSpec((tm, d), lambda i: (i, 1)),
            pl.BlockSpec((tm, d), lambda i: (i, 2)),
            pl.BlockSpec(wb.shape, const),
            pl.BlockSpec(wo.shape, const),
        ],
        out_specs=pl.BlockSpec((tm, d), row),
        out_shape=jax.ShapeDtypeStruct((n, d), F32),
        compiler_params=_params(("parallel",), 56),
        name="merge",
    )(x, ya, yb, yc, proj, proj, proj, wb, wo)


def _rope_tables(seq_len, head_dim):
    pos = jnp.arange(seq_len, dtype=F32)
    inv = ROPE_THETA ** (-jnp.arange(0, head_dim, 2, dtype=F32) / head_dim)
    ang = pos[:, None] * inv[None, :]
    cos, sin = jnp.cos(ang), jnp.sin(ang)
    reps = LANES // head_dim
    return (jnp.tile(jnp.concatenate([cos, cos], axis=1), (1, reps)),
            jnp.tile(jnp.concatenate([-sin, sin], axis=1), (1, reps)))


def _s5_tables(lam_re, lam_im, log_dt, b_re, b_im, c_re, c_im):
    n_groups, n_state, group = b_re.shape
    s_dim = n_groups * n_state
    eye = jnp.eye(n_groups, dtype=F32)
    bmat = jnp.concatenate([
        jnp.einsum('gk,gnh->ghkn', eye, b_re).reshape(n_groups * group, s_dim),
        jnp.einsum('gk,gnh->ghkn', eye, b_im).reshape(n_groups * group, s_dim)], axis=1)
    cmats, tabs = [], []
    row = jnp.arange(SUBLANES)[:, None]
    for d in range(2):
        lr, li = lam_re[d], lam_im[d]
        dt = jnp.exp(log_dt[d])[:, None]
        mag = jnp.exp(dt * lr)
        ar, ai = mag * jnp.cos(dt * li), mag * jnp.sin(dt * li)
        denom = lr * lr + li * li
        nr = ar - 1.0
        kr = (nr * lr + ai * li) / denom
        ki = (ai * lr - nr * li) / denom
        er = c_re * kr[:, None, :] - c_im * ki[:, None, :]
        ei = c_re * ki[:, None, :] + c_im * kr[:, None, :]
        cmats.append(jnp.concatenate([
            jnp.einsum('gk,ghn->gnkh', eye, er).reshape(s_dim, n_groups * group),
            jnp.einsum('gk,ghn->gnkh', eye, -ei).reshape(s_dim, n_groups * group)], axis=0))
        pows = [(ar.reshape(-1), ai.reshape(-1))]
        for _ in range(SUBLANES - 1):
            pr, pi = pows[-1]
            pows.append((pr * pows[0][0] - pi * pows[0][1], pr * pows[0][1] + pi * pows[0][0]))
        rows = []
        for sh in (1, 2, 4):
            keep = (row >= sh) if d == 0 else (row < SUBLANES - sh)
            rows += [jnp.where(keep, pows[sh - 1][0][None, :], 0.0), jnp.where(keep, pows[sh - 1][1][None, :], 0.0)]
        order = jnp.arange(SUBLANES) if d == 0 else jnp.arange(SUBLANES)[::-1]
        rows += [jnp.stack([p[0] for p in pows])[order], jnp.stack([p[1] for p in pows])[order]]
        tabs.append(jnp.stack(rows))
    return bmat.astype(BF16), jnp.stack(tabs), jnp.stack(cmats).astype(BF16)


def kernel(x, norm_w, ffn_w13, ffn_w2, w_in, s5_lambda_re, s5_lambda_im, s5_log_dt, s5_b_re, s5_b_im, s5_c_re, s5_c_im, s5_d, s5_w_glu, s5_b_glu, diff_lambda, diff_subln, conv_w, w_branch, w_gate, b_gate, w_out, final_norm):
    batch, seq_len, d_model = x.shape
    depth = norm_w.shape[0]
    d_ff = ffn_w2.shape[2]
    d_ssm = s5_d.shape[-1]
    d_conv = conv_w.shape[-1]
    d_attn = (w_in.shape[-1] - d_ssm - 3 * d_conv) // 3
    head_dim = diff_lambda.shape[-1]
    n_heads = d_attn // (2 * head_dim)
    n = batch * seq_len
    assert 2 * head_dim == LANES and d_ssm == d_conv and d_attn == 2 * d_ssm

    tm_ffn = _tile(n, 1024)
    tf = min(256, d_ff)
    tm_proj = _tile(seq_len, 1024)
    tn_proj = d_attn
    tm_merge = _tile(n, 256)
    tq = _tile(seq_len, 256)
    t_chunk = _tile(seq_len, 512)

    cos, sin = _rope_tables(seq_len, head_dim)
    xf = x.reshape(n, d_model)
    gate_cols = N_BRANCH * d_model
    u_col = gate_cols // d_ssm
    q_col = (gate_cols + d_ssm) // LANES
    k_col = q_col + d_attn // LANES
    v_col = k_col + d_attn // LANES
    bg_col = (gate_cols + d_ssm + 3 * d_attn) // d_conv
    b_gate3 = b_gate[:, None, :]

    for l in range(depth):
        lambda_init = 0.8 - 0.6 * math.exp(-0.3 * l)
        bmat, tabs, cmat = _s5_tables(s5_lambda_re[l], s5_lambda_im[l], s5_log_dt[l],
                                      s5_b_re[l], s5_b_im[l], s5_c_re[l], s5_c_im[l])

        xf, h_mix = _ffn(xf, norm_w[l, 0][None, :], ffn_w13, ffn_w2, norm_w[l, 1][None, :], layer=l, idx=0,
                         post="next", tm=tm_ffn, tf=tf)

        proj = _proj(h_mix, w_gate, b_gate3, w_in, cos, sin, layer=l, seq_len=seq_len, u_cols=d_ssm,
                     attn_cols=d_attn, q_scale=head_dim ** -0.5 * math.log2(math.e), tm=tm_proj, tn=tn_proj)
        proj3 = proj.reshape(batch, seq_len, proj.shape[1])

        ys = _s5_scan(proj, bmat, tabs, cmat, batch=batch, seq_len=seq_len, u_col=u_col, d_ssm=d_ssm,
                      tc=t_chunk)
        y_a = _s5_glu(ys, proj, s5_d[l][None, :], s5_w_glu[l].astype(BF16), s5_b_glu[l][None, :],
                      u_col=u_col, tm=_tile(n, 1024))
        y_b = _attention(proj3, diff_lambda[l], diff_subln[l][None, :], q_col=q_col, k_col=k_col, v_col=v_col,
                         n_heads=n_heads, head_dim=head_dim, lambda_init=lambda_init, tq=tq)
        y_c = _short_conv(proj3, conv_w[l], bg_col=bg_col, d_conv=d_conv)

        xf = _merge(xf, y_a, y_b.reshape(n, d_attn), y_c.reshape(n, d_conv), proj,
                    w_branch[l].astype(BF16), w_out[l].astype(BF16), tm=tm_merge)

        xf = _ffn(xf, norm_w[l, 2][None, :], ffn_w13, ffn_w2, final_norm[None, :], layer=l, idx=1,
                  post="final" if l == depth - 1 else "none", tm=tm_ffn, tf=tf)
    return xf.reshape(batch, seq_len, d_model)
```

```python
import functools
import math

import jax
import jax.numpy as jnp
from jax import lax
from jax.experimental import pallas as pl
from jax.experimental.pallas import tpu as pltpu

NORM_EPS = 1e-6
ROPE_THETA = 10000.0
N_BRANCH = 3
LANES = 128
SUBLANES = 8
MIB = 1024 * 1024
F32 = jnp.float32
BF16 = jnp.bfloat16


def _rms(x, g):
    return x * lax.rsqrt(jnp.mean(x * x, axis=-1, keepdims=True) + NORM_EPS) * g


def _round_up(a, b):
    return (a + b - 1) // b * b


def _tile(n, want):
    t = min(n, want)
    while n % t:
        t -= 1
    return t


def _params(sem, vmem_mib):
    return pltpu.CompilerParams(dimension_semantics=sem, vmem_limit_bytes=vmem_mib * MIB)


def _ffn_kernel(x_ref, g_ref, wa_ref, *rest, post, d_ff, n_units):
    wb_refs = rest[:n_units]
    w2_refs = rest[n_units:2 * n_units]
    if post == "next":
        fg_ref, o_ref, hn_ref = rest[2 * n_units:]
        h_ref = hn_ref
    else:
        fg_ref, o_ref, h_ref = rest[2 * n_units:]
    j = pl.program_id(1)
    tf = wa_ref.shape[1]

    @pl.when(j == 0)
    def _():
        h_ref[...] = _rms(x_ref[...], g_ref[...]).astype(BF16)
        o_ref[...] = jnp.zeros_like(o_ref)

    h = h_ref[...]
    a = jnp.dot(h, wa_ref[...].astype(BF16), preferred_element_type=F32)
    wb = jnp.concatenate([r[...].astype(BF16) for r in wb_refs], axis=1)
    b = jnp.dot(h, wb, preferred_element_type=F32)
    col = j * tf + lax.broadcasted_iota(jnp.int32, a.shape, 1)
    act = jnp.where(col < d_ff, a * jax.nn.sigmoid(a) * b, 0.0).astype(BF16)
    w2 = jnp.concatenate([r[...].astype(BF16) for r in w2_refs], axis=0)
    o_ref[...] += jnp.dot(act, w2, preferred_element_type=F32)

    @pl.when(j == pl.num_programs(1) - 1)
    def _():
        y = x_ref[...] + 0.5 * o_ref[...]
        if post == "final":
            y = _rms(y, fg_ref[...])
        if post == "next":
            hn_ref[...] = _rms(y, fg_ref[...]).astype(BF16)
        o_ref[...] = y


def _ffn(x, g, w13, w2, fg, *, layer, idx, post, tm, tf):
    n, d = x.shape
    row_blk = pl.BlockSpec((tm, d), lambda i, j: (i, 0))
    out_specs, out_shape = row_blk, jax.ShapeDtypeStruct((n, d), F32)
    if post == "next":
        out_specs, out_shape = [row_blk, row_blk], [out_shape, jax.ShapeDtypeStruct((n, d), BF16)]
    d_ff = w2.shape[2]
    assert d_ff % LANES == 0 and tf % LANES == 0 and 2 * d_ff >= _round_up(d_ff, tf)
    n_units = tf // LANES
    ff_units = d_ff // LANES

    def wb_spec(r):
        return pl.BlockSpec((None, None, d, LANES), lambda i, j: (
            layer, idx, 0, ff_units + jnp.minimum(j * n_units + r, ff_units - 1)))

    def w2_spec(r):
        return pl.BlockSpec((None, None, LANES, d), lambda i, j: (
            layer, idx, jnp.minimum(j * n_units + r, ff_units - 1), 0))

    return pl.pallas_call(
        functools.partial(_ffn_kernel, post=post, d_ff=d_ff, n_units=n_units),
        grid=(n // tm, pl.cdiv(d_ff, tf)),
        in_specs=[
            row_blk,
            pl.BlockSpec((1, d), lambda i, j: (0, 0)),
            pl.BlockSpec((None, None, d, tf), lambda i, j: (layer, idx, 0, j)),
            *[wb_spec(r) for r in range(n_units)],
            *[w2_spec(r) for r in range(n_units)],
            pl.BlockSpec((1, d), lambda i, j: (0, 0)),
        ],
        out_specs=out_specs,
        out_shape=out_shape,
        scratch_shapes=[] if post == "next" else [pltpu.VMEM((tm, d), BF16)],
        compiler_params=_params(("parallel", "arbitrary"), 62),
        name="ffn",
    )(x, g, w13, *([w13] * n_units), *([w2] * n_units), fg)


def _rope(acc, cos, sin, scale):
    lane = lax.broadcasted_iota(jnp.int32, cos.shape, 1)
    first_half = (lane & (LANES // 4)) == 0
    out = []
    for c in range(acc.shape[1] // LANES):
        blk = acc[:, c * LANES:(c + 1) * LANES]
        partner = jnp.where(first_half,
                            pltpu.roll(blk, LANES - LANES // 4, axis=1),
                            pltpu.roll(blk, LANES // 4, axis=1))
        out.append((blk * cos + partner * sin) * scale)
    return jnp.concatenate(out, axis=1)


def _proj_kernel(h_ref, wg_ref, wi_ref, b_ref, cos_ref, sin_ref, o_ref, *,
                 n_gate, n_in, u_cols, attn_cols, q_scale, chunk):
    j = pl.program_id(1)
    tn = o_ref.shape[1]

    def run(w_ref, epilogue):
        for c in range(tn // chunk):
            cols = slice(c * chunk, (c + 1) * chunk)
            acc = jnp.dot(h_ref[...], w_ref[:, cols].astype(BF16), preferred_element_type=F32)
            o_ref[:, cols] = epilogue(acc, c).astype(o_ref.dtype)

    @pl.when(j < n_gate)
    def _():
        run(wg_ref, lambda acc, c: jax.nn.sigmoid(acc + b_ref[:, c * chunk:(c + 1) * chunk]))

    def in_epilogue(blk):
        def epilogue(acc, c):
            col = blk * tn + c * chunk
            if u_cols <= col < u_cols + attn_cols:
                return _rope(acc, cos_ref[...], sin_ref[...], q_scale)
            if u_cols + attn_cols <= col < u_cols + 2 * attn_cols:
                return _rope(acc, cos_ref[...], sin_ref[...], 1.0)
            return acc
        return epilogue

    for blk in range(n_in):
        @pl.when(j == n_gate + blk)
        def _(blk=blk):
            run(wi_ref, in_epilogue(blk))


def _proj(h, w_gate, b_gate, w_in, cos, sin, *, layer, seq_len, u_cols, attn_cols, q_scale, tm, tn):
    n, d = h.shape
    n_gate = w_gate.shape[2] // tn
    n_in = w_in.shape[2] // tn
    chunk = min(u_cols, 2 * LANES)
    assert u_cols % chunk == 0 and attn_cols % chunk == 0 and tn % chunk == 0
    pos_blocks = seq_len // tm
    gate_blk = lambda i, j: (layer, 0, jnp.minimum(j, n_gate - 1))
    return pl.pallas_call(
        functools.partial(_proj_kernel, n_gate=n_gate, n_in=n_in, u_cols=u_cols, attn_cols=attn_cols,
                          q_scale=q_scale, chunk=chunk),
        grid=(n // tm, n_gate + n_in),
        in_specs=[
            pl.BlockSpec((tm, d), lambda i, j: (i, 0)),
            pl.BlockSpec((None, d, tn), gate_blk),
            pl.BlockSpec((None, d, tn), lambda i, j: (layer, 0, jnp.maximum(j - n_gate, 0))),
            pl.BlockSpec((None, 1, tn), gate_blk),
            pl.BlockSpec((tm, LANES), lambda i, j: (i % pos_blocks, 0)),
            pl.BlockSpec((tm, LANES), lambda i, j: (i % pos_blocks, 0)),
        ],
        out_specs=pl.BlockSpec((tm, tn), lambda i, j: (i, j)),
        out_shape=jax.ShapeDtypeStruct((n, (n_gate + n_in) * tn), BF16),
        compiler_params=_params(("parallel", "arbitrary"), 56),
        name="proj",
    )(h, w_gate, w_in, b_gate, cos, sin)


def _scan_chunk(bu_ref, tab_ref, carry_ref, *, reverse, lane_group):
    t_len, s2 = bu_ref.shape
    s_dim = s2 // 2
    n_tiles = t_len // SUBLANES
    shifts = (1, 2, 4)
    for lg in range(s_dim // lane_group):
        lo = lg * lane_group
        re_cols = pl.ds(lo, lane_group)
        im_cols = pl.ds(s_dim + lo, lane_group)

        def body(it, carry):
            c_re, c_im = carry
            tile = (n_tiles - 1 - it) if reverse else it
            rows = pl.ds(pl.multiple_of(tile * SUBLANES, SUBLANES), SUBLANES)
            x_re = bu_ref[rows, re_cols]
            x_im = bu_ref[rows, im_cols]
            for lvl, sh in enumerate(shifts):
                a_re = tab_ref[2 * lvl, :, re_cols]
                a_im = tab_ref[2 * lvl + 1, :, re_cols]
                roll_by = (SUBLANES - sh) if reverse else sh
                r_re = pltpu.roll(x_re, roll_by, axis=0)
                r_im = pltpu.roll(x_im, roll_by, axis=0)
                x_re, x_im = (x_re + a_re * r_re - a_im * r_im,
                              x_im + a_re * r_im + a_im * r_re)
            p_re = tab_ref[6, :, re_cols]
            p_im = tab_ref[7, :, re_cols]
            x_re, x_im = (x_re + p_re * c_re - p_im * c_im,
                          x_im + p_re * c_im + p_im * c_re)
            bu_ref[rows, re_cols] = x_re
            bu_ref[rows, im_cols] = x_im
            edge = 0 if reverse else SUBLANES - 1
            n_re = jnp.broadcast_to(x_re[edge:edge + 1, :], x_re.shape)
            n_im = jnp.broadcast_to(x_im[edge:edge + 1, :], x_im.shape)
            return n_re, n_im

        c0 = (carry_ref[0, :, re_cols], carry_ref[1, :, re_cols])
        c_re, c_im = lax.fori_loop(0, n_tiles, body, c0, unroll=2)
        carry_ref[0, :, re_cols] = c_re
        carry_ref[1, :, re_cols] = c_im


def _s5_kernel(u_ref, bmat_ref, tab_ref, cmat_ref, y_ref, bu_ref, carry_ref, *, lane_group):
    d = pl.program_id(1)
    c = pl.program_id(2)

    @pl.when(c == 0)
    def _():
        carry_ref[...] = jnp.zeros_like(carry_ref)

    d_ssm, s2 = bmat_ref.shape
    s_dim = s2 // 2
    halves = [(slice(h * d_ssm // 2, (h + 1) * d_ssm // 2), slice(h * s_dim // 2, (h + 1) * s_dim // 2),
               slice(s_dim + h * s_dim // 2, s_dim + (h + 1) * s_dim // 2)) for h in range(2)]
    for ch, st_re, st_im in halves:
        u_half = u_ref[:, ch]
        bu_ref[:, st_re] = jnp.dot(u_half, bmat_ref[ch, st_re], preferred_element_type=F32)
        bu_ref[:, st_im] = jnp.dot(u_half, bmat_ref[ch, st_im], preferred_element_type=F32)

    @pl.when(d == 0)
    def _():
        _scan_chunk(bu_ref, tab_ref.at[0], carry_ref, reverse=False, lane_group=lane_group)

    @pl.when(d == 1)
    def _():
        _scan_chunk(bu_ref, tab_ref.at[0], carry_ref, reverse=True, lane_group=lane_group)

    for ch, st_re, st_im in halves:
        y_ref[0, :, ch] = (
            jnp.dot(bu_ref[:, st_re].astype(BF16), cmat_ref[0, st_re, ch], preferred_element_type=F32)
            + jnp.dot(bu_ref[:, st_im].astype(BF16), cmat_ref[0, st_im, ch], preferred_element_type=F32))


def _s5_scan(proj, bmat, tabs, cmat, *, batch, seq_len, u_col, d_ssm, tc):
    n = proj.shape[0]
    n_chunks = seq_len // tc
    s2 = bmat.shape[1]
    s_dim = s2 // 2
    lane_group = min(s_dim, 4 * LANES)

    def chunk_row(b, d, c):
        return b * n_chunks + jnp.where(d == 0, c, n_chunks - 1 - c)

    return pl.pallas_call(
        functools.partial(_s5_kernel, lane_group=lane_group),
        grid=(batch, 2, n_chunks),
        in_specs=[
            pl.BlockSpec((tc, d_ssm), lambda b, d, c: (chunk_row(b, d, c), u_col)),
            pl.BlockSpec((d_ssm, s2), lambda b, d, c: (0, 0)),
            pl.BlockSpec((1, 8, SUBLANES, s_dim), lambda b, d, c: (d, 0, 0, 0)),
            pl.BlockSpec((1, s2, d_ssm), lambda b, d, c: (d, 0, 0)),
        ],
        out_specs=pl.BlockSpec((1, tc, d_ssm), lambda b, d, c: (d, chunk_row(b, d, c), 0)),
        out_shape=jax.ShapeDtypeStruct((2, n, d_ssm), F32),
        scratch_shapes=[pltpu.VMEM((tc, s2), F32), pltpu.VMEM((2, SUBLANES, s_dim), F32)],
        compiler_params=_params(("parallel", "arbitrary", "arbitrary"), 52),
        name="s5_scan",
    )(proj, bmat, tabs, cmat)


def _glu_kernel(ys_ref, u_ref, dskip_ref, w_ref, b_ref, o_ref):
    y = ys_ref[0] + ys_ref[1] + dskip_ref[...] * u_ref[...].astype(F32)
    y = jax.nn.gelu(y)
    z = jnp.dot(y.astype(BF16), w_ref[...], preferred_element_type=F32) + b_ref[...]
    o_ref[...] = (y * jax.nn.sigmoid(z)).astype(o_ref.dtype)


def _s5_glu(ys, proj, dskip, w, b, *, u_col, tm):
    _, n, d_ssm = ys.shape
    return pl.pallas_call(
        _glu_kernel,
        grid=(n // tm,),
        in_specs=[
            pl.BlockSpec((2, tm, d_ssm), lambda i: (0, i, 0)),
            pl.BlockSpec((tm, d_ssm), lambda i: (i, u_col)),
            pl.BlockSpec((1, d_ssm), lambda i: (0, 0)),
            pl.BlockSpec((d_ssm, d_ssm), lambda i: (0, 0)),
            pl.BlockSpec((1, d_ssm), lambda i: (0, 0)),
        ],
        out_specs=pl.BlockSpec((tm, d_ssm), lambda i: (i, 0)),
        out_shape=jax.ShapeDtypeStruct((n, d_ssm), BF16),
        compiler_params=_params(("parallel",), 40),
        name="s5_glu",
    )(ys, proj, dskip, w, b)


def _attn_kernel(q_ref, k_ref, v_ref, lam_ref, sg_ref, o_ref, *, lambda_init, head_dim, kc):
    q = q_ref[0]
    tq = q.shape[0]
    lane = lax.broadcasted_iota(jnp.int32, q.shape, 1)
    zero = jnp.zeros_like(q)
    qq = jnp.concatenate([jnp.where(lane < head_dim, q, zero), jnp.where(lane >= head_dim, q, zero)], axis=0)
    n_chunks = k_ref.shape[1] // kc
    sub = 2 * SUBLANES

    def scores(c):
        return lax.dot_general(k_ref[0, pl.ds(c * kc, kc), :], qq, (((1,), (1,)), ((), ())),
                               preferred_element_type=F32)

    def exp_stage(s, m_new):
        pieces, part = [], None
        for r in range(0, kc, sub):
            e_r = jnp.exp2(s[r:r + sub] - m_new)
            pieces.append(e_r.astype(BF16))
            for g in range(sub // SUBLANES):
                tile = e_r[g * SUBLANES:(g + 1) * SUBLANES]
                part = tile if part is None else part + tile
        return jnp.concatenate(pieces, axis=0), jnp.sum(part, axis=0, keepdims=True)

    def value_stage(acc, item):
        c, e_b, alpha = item
        pv = lax.dot_general(v_ref[0, pl.ds(c * kc, kc), :], e_b, (((0,), (0,)), ((), ())),
                             preferred_element_type=F32)
        return pv if acc is None else acc * alpha + pv

    score_lead = 2
    m = l = acc = None
    s_q = [scores(c) for c in range(min(score_lead, n_chunks))]
    pending = None
    for c in range(n_chunks):
        if c + score_lead < n_chunks:
            s_q.append(scores(c + score_lead))
        s = s_q[c]
        m_c = jnp.max(s, axis=0, keepdims=True)
        m_new = m_c if m is None else jnp.maximum(m, m_c)
        e_b, l_c = exp_stage(s, m_new)
        alpha = None if m is None else jnp.exp2(m - m_new)
        l = l_c if m is None else l * alpha + l_c
        if pending is not None:
            acc = value_stage(acc, pending)
        pending = (c, e_b, alpha)
        m = m_new
    acc = value_stage(acc, pending)
    lv = lam_ref[...]
    lam = (jnp.exp(jnp.sum(lv[0:1] * lv[1:2], axis=-1, keepdims=True))
           - jnp.exp(jnp.sum(lv[2:3] * lv[3:4], axis=-1, keepdims=True)) + lambda_init)
    o_t = acc[:, 0:tq] * (1.0 / l[:, 0:tq]) - acc[:, tq:2 * tq] * (lam / l[:, tq:2 * tq])
    o_ref[0] = (_rms(o_t.T, sg_ref[...]) * (1.0 - lambda_init)).astype(o_ref.dtype)


def _attention(proj3, lam_vec, subln, *, q_col, k_col, v_col, n_heads, head_dim, lambda_init, tq):
    batch, seq_len, _ = proj3.shape
    vd = 2 * head_dim
    return pl.pallas_call(
        functools.partial(_attn_kernel, lambda_init=lambda_init, head_dim=head_dim, kc=_tile(seq_len, 512)),
        grid=(batch, n_heads, seq_len // tq),
        in_specs=[
            pl.BlockSpec((1, tq, vd), lambda b, h, i: (b, i, q_col + h)),
            pl.BlockSpec((1, seq_len, vd), lambda b, h, i: (b, 0, k_col + h)),
            pl.BlockSpec((1, seq_len, vd), lambda b, h, i: (b, 0, v_col + h)),
            pl.BlockSpec((4, head_dim), lambda b, h, i: (0, 0)),
            pl.BlockSpec((1, vd), lambda b, h, i: (0, 0)),
        ],
        out_specs=pl.BlockSpec((1, tq, vd), lambda b, h, i: (b, i, h)),
        out_shape=jax.ShapeDtypeStruct((batch, seq_len, n_heads * vd), BF16),
        compiler_params=_params(("parallel", "parallel", "arbitrary"), 52),
        name="diff_attn",
    )(proj3, proj3, proj3, lam_vec, subln)


def _conv_kernel(bg_ref, cg_ref, xv_ref, w_ref, o_ref):
    z = cg_ref[0].astype(F32) * xv_ref[0].astype(F32)
    seq_len = z.shape[0]
    row = lax.broadcasted_iota(jnp.int32, z.shape, 0)
    z_prev = jnp.where(row == 0, 0.0, pltpu.roll(z, 1, axis=0))
    z_next = jnp.where(row == seq_len - 1, 0.0, pltpu.roll(z, seq_len - 1, axis=0))
    w = w_ref[...]
    zc = w[0:1] * z_prev + w[1:2] * z + w[2:3] * z_next
    o_ref[0] = (bg_ref[0].astype(F32) * zc).astype(o_ref.dtype)


def _short_conv(proj3, conv_w, *, bg_col, d_conv):
    batch, seq_len, _ = proj3.shape
    tc = min(d_conv, LANES)
    per = d_conv // tc
    return pl.pallas_call(
        _conv_kernel,
        grid=(batch, per),
        in_specs=[
            pl.BlockSpec((1, seq_len, tc), lambda b, j: (b, 0, bg_col * per + j)),
            pl.BlockSpec((1, seq_len, tc), lambda b, j: (b, 0, (bg_col + 1) * per + j)),
            pl.BlockSpec((1, seq_len, tc), lambda b, j: (b, 0, (bg_col + 2) * per + j)),
            pl.BlockSpec((conv_w.shape[0], tc), lambda b, j: (0, j)),
        ],
        out_specs=pl.BlockSpec((1, seq_len, tc), lambda b, j: (b, 0, j)),
        out_shape=jax.ShapeDtypeStruct((batch, seq_len, d_conv), BF16),
        compiler_params=_params(("parallel", "parallel"), 40),
        name="short_conv",
    )(proj3, proj3, proj3, conv_w)


def _merge_kernel(x_ref, ya_ref, yb_ref, yc_ref, ga_ref, gb_ref, gc_ref, wb_ref, wo_ref, o_ref):
    r_a = ya_ref.shape[1]
    r_b = r_a + yb_ref.shape[1]
    pa = jnp.dot(ya_ref[...], wb_ref[0:r_a, :], preferred_element_type=F32)
    merged = ga_ref[...].astype(F32) * pa
    pb = jnp.dot(yb_ref[...], wb_ref[r_a:r_b, :], preferred_element_type=F32)
    merged += gb_ref[...].astype(F32) * pb
    pc = jnp.dot(yc_ref[...], wb_ref[r_b:, :], preferred_element_type=F32)
    merged += gc_ref[...].astype(F32) * pc
    o_ref[...] = x_ref[...] + jnp.dot(merged.astype(BF16), wo_ref[...], preferred_element_type=F32)


def _merge(x, ya, yb, yc, proj, wb, wo, *, tm):
    n, d = x.shape
    row = lambda i: (i, 0)
    const = lambda i: (0, 0)
    return pl.pallas_call(
        _merge_kernel,
        grid=(n // tm,),
        in_specs=[
            pl.BlockSpec((tm, d), row),
            pl.BlockSpec((tm, ya.shape[1]), row),
            pl.BlockSpec((tm, yb.shape[1]), row),
            pl.BlockSpec((tm, yc.shape[1]), row),
            pl.BlockSpec((tm, d), lambda i: (i, 0)),
            pl.BlockSpec((tm, d), lambda i: (i, 1)),
            pl.BlockSpec((tm, d), lambda i: (i, 2)),
            pl.BlockSpec(wb.shape, const),
            pl.BlockSpec(wo.shape, const),
        ],
        out_specs=pl.BlockSpec((tm, d), row),
        out_shape=jax.ShapeDtypeStruct((n, d), F32),
        compiler_params=_params(("parallel",), 56),
        name="merge",
    )(x, ya, yb, yc, proj, proj, proj, wb, wo)


def _rope_tables(seq_len, head_dim):
    pos = jnp.arange(seq_len, dtype=F32)
    inv = ROPE_THETA ** (-jnp.arange(0, head_dim, 2, dtype=F32) / head_dim)
    ang = pos[:, None] * inv[None, :]
    cos, sin = jnp.cos(ang), jnp.sin(ang)
    reps = LANES // head_dim
    return (jnp.tile(jnp.concatenate([cos, cos], axis=1), (1, reps)),
            jnp.tile(jnp.concatenate([-sin, sin], axis=1), (1, reps)))


def _block_diag(blocks):
    g, r, c = blocks.shape
    wide = jnp.transpose(blocks, (1, 0, 2)).reshape(r, g * c)
    row_g = jnp.arange(g * r)[:, None] // r
    col_g = jnp.arange(g * c)[None, :] // c
    return jnp.where(row_g == col_g, jnp.tile(wide, (g, 1)), 0.0)


def _s5_tables(lam_re, lam_im, log_dt, b_re, b_im, c_re, c_im):
    bmat = jnp.concatenate([_block_diag(jnp.swapaxes(b_re, 1, 2)), _block_diag(jnp.swapaxes(b_im, 1, 2))], axis=1)
    cmats, tabs = [], []
    row = jnp.arange(SUBLANES)[:, None]
    for d in range(2):
        lr, li = lam_re[d], lam_im[d]
        dt = jnp.exp(log_dt[d])[:, None]
        mag = jnp.exp(dt * lr)
        ar, ai = mag * jnp.cos(dt * li), mag * jnp.sin(dt * li)
        denom = lr * lr + li * li
        nr = ar - 1.0
        kr = (nr * lr + ai * li) / denom
        ki = (ai * lr - nr * li) / denom
        er = c_re * kr[:, None, :] - c_im * ki[:, None, :]
        ei = c_re * ki[:, None, :] + c_im * kr[:, None, :]
        cmats.append(jnp.concatenate([_block_diag(jnp.swapaxes(er, 1, 2)), _block_diag(jnp.swapaxes(-ei, 1, 2))],
                                     axis=0))
        pows = [(ar.reshape(-1), ai.reshape(-1))]
        for _ in range(SUBLANES - 1):
            pr, pi = pows[-1]
            pows.append((pr * pows[0][0] - pi * pows[0][1], pr * pows[0][1] + pi * pows[0][0]))
        rows = []
        for sh in (1, 2, 4):
            keep = (row >= sh) if d == 0 else (row < SUBLANES - sh)
            rows += [jnp.where(keep, pows[sh - 1][0][None, :], 0.0), jnp.where(keep, pows[sh - 1][1][None, :], 0.0)]
        order = jnp.arange(SUBLANES) if d == 0 else jnp.arange(SUBLANES)[::-1]
        rows += [jnp.stack([p[0] for p in pows])[order], jnp.stack([p[1] for p in pows])[order]]
        tabs.append(jnp.stack(rows))
    return bmat.astype(BF16), jnp.stack(tabs), jnp.stack(cmats).astype(BF16)


def kernel(x, norm_w, ffn_w13, ffn_w2, w_in, s5_lambda_re, s5_lambda_im, s5_log_dt, s5_b_re, s5_b_im, s5_c_re, s5_c_im, s5_d, s5_w_glu, s5_b_glu, diff_lambda, diff_subln, conv_w, w_branch, w_gate, b_gate, w_out, final_norm):
    batch, seq_len, d_model = x.shape
    depth = norm_w.shape[0]
    d_ff = ffn_w2.shape[2]
    d_ssm = s5_d.shape[-1]
    d_conv = conv_w.shape[-1]
    d_attn = (w_in.shape[-1] - d_ssm - 3 * d_conv) // 3
    head_dim = diff_lambda.shape[-1]
    n_heads = d_attn // (2 * head_dim)
    n = batch * seq_len
    assert 2 * head_dim == LANES and d_ssm == d_conv and d_attn == 2 * d_ssm

    tm_ffn = _tile(n, 1024)
    tf = min(256, d_ff)
    tm_proj = _tile(seq_len, 1024)
    tn_proj = d_attn
    tm_merge = _tile(n, 256)
    tq = _tile(seq_len, 512)
    t_chunk = _tile(seq_len, 512)

    cos, sin = _rope_tables(seq_len, head_dim)
    xf = x.reshape(n, d_model)
    gate_cols = N_BRANCH * d_model
    u_col = gate_cols // d_ssm
    q_col = (gate_cols + d_ssm) // LANES
    k_col = q_col + d_attn // LANES
    v_col = k_col + d_attn // LANES
    bg_col = (gate_cols + d_ssm + 3 * d_attn) // d_conv
    b_gate3 = b_gate[:, None, :]

    for l in range(depth):
        lambda_init = 0.8 - 0.6 * math.exp(-0.3 * l)
        bmat, tabs, cmat = _s5_tables(s5_lambda_re[l], s5_lambda_im[l], s5_log_dt[l],
                                      s5_b_re[l], s5_b_im[l], s5_c_re[l], s5_c_im[l])

        xf, h_mix = _ffn(xf, norm_w[l, 0][None, :], ffn_w13, ffn_w2, norm_w[l, 1][None, :], layer=l, idx=0,
                         post="next", tm=tm_ffn, tf=tf)

        proj = _proj(h_mix, w_gate, b_gate3, w_in, cos, sin, layer=l, seq_len=seq_len, u_cols=d_ssm,
                     attn_cols=d_attn, q_scale=head_dim ** -0.5 * math.log2(math.e), tm=tm_proj, tn=tn_proj)
        proj3 = proj.reshape(batch, seq_len, proj.shape[1])

        ys = _s5_scan(proj, bmat, tabs, cmat, batch=batch, seq_len=seq_len, u_col=u_col, d_ssm=d_ssm,
                      tc=t_chunk)
        y_a = _s5_glu(ys, proj, s5_d[l][None, :], s5_w_glu[l].astype(BF16), s5_b_glu[l][None, :],
                      u_col=u_col, tm=_tile(n, 1024))
        y_b = _attention(proj3, diff_lambda[l], diff_subln[l][None, :], q_col=q_col, k_col=k_col, v_col=v_col,
                         n_heads=n_heads, head_dim=head_dim, lambda_init=lambda_init, tq=tq)
        y_c = _short_conv(proj3, conv_w[l], bg_col=bg_col, d_conv=d_conv)

        xf = _merge(xf, y_a, y_b.reshape(n, d_attn), y_c.reshape(n, d_conv), proj,
                    w_branch[l].astype(BF16), w_out[l].astype(BF16), tm=tm_merge)

        xf = _ffn(xf, norm_w[l, 2][None, :], ffn_w13, ffn_w2, final_norm[None, :], layer=l, idx=1,
                  post="final" if l == depth - 1 else "none", tm=tm_ffn, tf=tf)
    return xf.reshape(batch, seq_len, d_model)
```

```python
import functools
import math

import jax
import jax.numpy as jnp
from jax import lax
from jax.experimental import pallas as pl
from jax.experimental.pallas import tpu as pltpu

NORM_EPS = 1e-6
ROPE_THETA = 10000.0
N_BRANCH = 3
LANES = 128
SUBLANES = 8
MIB = 1024 * 1024
F32 = jnp.float32
BF16 = jnp.bfloat16


def _rms(x, g):
    return x * lax.rsqrt(jnp.mean(x * x, axis=-1, keepdims=True) + NORM_EPS) * g


def _round_up(a, b):
    return (a + b - 1) // b * b


def _tile(n, want):
    t = min(n, want)
    while n % t:
        t -= 1
    return t


def _params(sem, vmem_mib):
    return pltpu.CompilerParams(dimension_semantics=sem, vmem_limit_bytes=vmem_mib * MIB)


def _ffn_kernel(x_ref, g_ref, wa_ref, *rest, post, d_ff, n_units):
    wb_refs = rest[:n_units]
    w2_refs = rest[n_units:2 * n_units]
    if post == "next":
        fg_ref, o_ref, hn_ref = rest[2 * n_units:]
        h_ref = hn_ref
    else:
        fg_ref, o_ref, h_ref = rest[2 * n_units:]
    j = pl.program_id(1)
    tf = wa_ref.shape[1]

    @pl.when(j == 0)
    def _():
        h_ref[...] = _rms(x_ref[...], g_ref[...]).astype(BF16)
        o_ref[...] = jnp.zeros_like(o_ref)

    h = h_ref[...]
    a = jnp.dot(h, wa_ref[...].astype(BF16), preferred_element_type=F32)
    wb = jnp.concatenate([r[...].astype(BF16) for r in wb_refs], axis=1)
    b = jnp.dot(h, wb, preferred_element_type=F32)
    col = j * tf + lax.broadcasted_iota(jnp.int32, a.shape, 1)
    act = jnp.where(col < d_ff, a * jax.nn.sigmoid(a) * b, 0.0).astype(BF16)
    w2 = jnp.concatenate([r[...].astype(BF16) for r in w2_refs], axis=0)
    o_ref[...] += jnp.dot(act, w2, preferred_element_type=F32)

    @pl.when(j == pl.num_programs(1) - 1)
    def _():
        y = x_ref[...] + 0.5 * o_ref[...]
        if post == "final":
            y = _rms(y, fg_ref[...])
        if post == "next":
            hn_ref[...] = _rms(y, fg_ref[...]).astype(BF16)
        o_ref[...] = y


def _ffn(x, g, w13, w2, fg, *, layer, idx, post, tm, tf):
    n, d = x.shape
    row_blk = pl.BlockSpec((tm, d), lambda i, j: (i, 0))
    out_specs, out_shape = row_blk, jax.ShapeDtypeStruct((n, d), F32)
    if post == "next":
        out_specs, out_shape = [row_blk, row_blk], [out_shape, jax.ShapeDtypeStruct((n, d), BF16)]
    d_ff = w2.shape[2]
    assert d_ff % LANES == 0 and tf % LANES == 0 and 2 * d_ff >= _round_up(d_ff, tf)
    n_units = tf // LANES
    ff_units = d_ff // LANES

    def wb_spec(r):
        return pl.BlockSpec((None, None, d, LANES), lambda i, j: (
            layer, idx, 0, ff_units + jnp.minimum(j * n_units + r, ff_units - 1)))

    def w2_spec(r):
        return pl.BlockSpec((None, None, LANES, d), lambda i, j: (
            layer, idx, jnp.minimum(j * n_units + r, ff_units - 1), 0))

    return pl.pallas_call(
        functools.partial(_ffn_kernel, post=post, d_ff=d_ff, n_units=n_units),
        grid=(n // tm, pl.cdiv(d_ff, tf)),
        in_specs=[
            row_blk,
            pl.BlockSpec((1, d), lambda i, j: (0, 0)),
            pl.BlockSpec((None, None, d, tf), lambda i, j: (layer, idx, 0, j)),
            *[wb_spec(r) for r in range(n_units)],
            *[w2_spec(r) for r in range(n_units)],
            pl.BlockSpec((1, d), lambda i, j: (0, 0)),
        ],
        out_specs=out_specs,
        out_shape=out_shape,
        scratch_shapes=[] if post == "next" else [pltpu.VMEM((tm, d), BF16)],
        compiler_params=_params(("parallel", "arbitrary"), 62),
        name="ffn",
    )(x, g, w13, *([w13] * n_units), *([w2] * n_units), fg)


def _rope(acc, cos, sin, scale):
    lane = lax.broadcasted_iota(jnp.int32, cos.shape, 1)
    first_half = (lane & (LANES // 4)) == 0
    out = []
    for c in range(acc.shape[1] // LANES):
        blk = acc[:, c * LANES:(c + 1) * LANES]
        partner = jnp.where(first_half,
                            pltpu.roll(blk, LANES - LANES // 4, axis=1),
                            pltpu.roll(blk, LANES // 4, axis=1))
        out.append((blk * cos + partner * sin) * scale)
    return jnp.concatenate(out, axis=1)


def _proj_kernel(h_ref, wg_ref, wi_ref, b_ref, cos_ref, sin_ref, o_ref, *,
                 n_gate, n_in, u_cols, attn_cols, q_scale, chunk):
    j = pl.program_id(1)
    tn = o_ref.shape[1]

    def run(w_ref, epilogue):
        for c in range(tn // chunk):
            cols = slice(c * chunk, (c + 1) * chunk)
            acc = jnp.dot(h_ref[...], w_ref[:, cols].astype(BF16), preferred_element_type=F32)
            o_ref[:, cols] = epilogue(acc, c).astype(o_ref.dtype)

    @pl.when(j < n_gate)
    def _():
        run(wg_ref, lambda acc, c: jax.nn.sigmoid(acc + b_ref[:, c * chunk:(c + 1) * chunk]))

    def in_epilogue(blk):
        def epilogue(acc, c):
            col = blk * tn + c * chunk
            if u_cols <= col < u_cols + attn_cols:
                return _rope(acc, cos_ref[...], sin_ref[...], q_scale)
            if u_cols + attn_cols <= col < u_cols + 2 * attn_cols:
                return _rope(acc, cos_ref[...], sin_ref[...], 1.0)
            return acc
        return epilogue

    for blk in range(n_in):
        @pl.when(j == n_gate + blk)
        def _(blk=blk):
            run(wi_ref, in_epilogue(blk))


def _proj(h, w_gate, b_gate, w_in, cos, sin, *, layer, seq_len, u_cols, attn_cols, q_scale, tm, tn):
    n, d = h.shape
    n_gate = w_gate.shape[2] // tn
    n_in = w_in.shape[2] // tn
    chunk = min(u_cols, 2 * LANES)
    assert u_cols % chunk == 0 and attn_cols % chunk == 0 and tn % chunk == 0
    pos_blocks = seq_len // tm
    gate_blk = lambda i, j: (layer, 0, jnp.minimum(j, n_gate - 1))
    return pl.pallas_call(
        functools.partial(_proj_kernel, n_gate=n_gate, n_in=n_in, u_cols=u_cols, attn_cols=attn_cols,
                          q_scale=q_scale, chunk=chunk),
        grid=(n // tm, n_gate + n_in),
        in_specs=[
            pl.BlockSpec((tm, d), lambda i, j: (i, 0)),
            pl.BlockSpec((None, d, tn), gate_blk),
            pl.BlockSpec((None, d, tn), lambda i, j: (layer, 0, jnp.maximum(j - n_gate, 0))),
            pl.BlockSpec((None, 1, tn), gate_blk),
            pl.BlockSpec((tm, LANES), lambda i, j: (i % pos_blocks, 0)),
            pl.BlockSpec((tm, LANES), lambda i, j: (i % pos_blocks, 0)),
        ],
        out_specs=pl.BlockSpec((tm, tn), lambda i, j: (i, j)),
        out_shape=jax.ShapeDtypeStruct((n, (n_gate + n_in) * tn), BF16),
        compiler_params=_params(("parallel", "arbitrary"), 56),
        name="proj",
    )(h, w_gate, w_in, b_gate, cos, sin)


def _scan_chunk(bu_ref, tab_ref, carry_ref, *, reverse, lane_group):
    t_len, s2 = bu_ref.shape
    s_dim = s2 // 2
    n_steps = t_len // SUBLANES
    first, last = (SUBLANES - 1, 0) if reverse else (0, SUBLANES - 1)
    toward_later = SUBLANES - 1 if reverse else 1
    for lg in range(s_dim // lane_group):
        re_cols = pl.ds(lg * lane_group, lane_group)
        im_cols = pl.ds(s_dim + lg * lane_group, lane_group)
        a_re, a_im = tab_ref[0, :, re_cols], tab_ref[1, :, re_cols]
        an_re, an_im = tab_ref[2, :, re_cols], tab_ref[3, :, re_cols]

        def rows_of(it):
            step = (n_steps - 1 - it) if reverse else it
            return pl.ds(pl.multiple_of(step * SUBLANES, SUBLANES), SUBLANES)

        def local_scan(it, state):
            s_re, s_im = state
            rows = rows_of(it)
            s_re, s_im = (a_re * s_re - a_im * s_im + bu_ref[rows, re_cols],
                          a_re * s_im + a_im * s_re + bu_ref[rows, im_cols])
            bu_ref[rows, re_cols] = s_re
            bu_ref[rows, im_cols] = s_im
            return s_re, s_im

        zero = jnp.zeros((SUBLANES, lane_group), F32)
        e_re, e_im = lax.fori_loop(0, n_steps, local_scan, (zero, zero), unroll=4)

        row = lax.broadcasted_iota(jnp.int32, (SUBLANES, lane_group), 0)
        c_re = jnp.where(row == first, carry_ref[0, :, re_cols], 0.0)
        c_im = jnp.where(row == first, carry_ref[1, :, re_cols], 0.0)

        def segment_end(c_re, c_im):
            return e_re + an_re * c_re - an_im * c_im, e_im + an_re * c_im + an_im * c_re

        order = range(SUBLANES - 2, -1, -1) if reverse else range(1, SUBLANES)
        for k in order:
            t_re, t_im = segment_end(c_re, c_im)
            c_re = jnp.where(row == k, pltpu.roll(t_re, toward_later, axis=0), c_re)
            c_im = jnp.where(row == k, pltpu.roll(t_im, toward_later, axis=0), c_im)
        t_re, t_im = segment_end(c_re, c_im)
        carry_ref[0, :, re_cols] = jnp.broadcast_to(t_re[last:last + 1], t_re.shape)
        carry_ref[1, :, re_cols] = jnp.broadcast_to(t_im[last:last + 1], t_im.shape)

        def add_incoming(it, f):
            f_re, f_im = f
            rows = rows_of(it)
            f_re, f_im = a_re * f_re - a_im * f_im, a_re * f_im + a_im * f_re
            bu_ref[rows, re_cols] += f_re
            bu_ref[rows, im_cols] += f_im
            return f_re, f_im

        lax.fori_loop(0, n_steps, add_incoming, (c_re, c_im), unroll=4)


def _s5_kernel(u_ref, perm_ref, bmat_ref, tab_ref, cmat_ref, y_ref, bu_ref, carry_ref, *, lane_group):
    d = pl.program_id(1)
    c = pl.program_id(2)

    @pl.when(c == 0)
    def _():
        carry_ref[...] = jnp.zeros_like(carry_ref)

    u_perm = jnp.dot(perm_ref[...], u_ref[...], preferred_element_type=F32).astype(BF16)

    d_ssm, s2 = bmat_ref.shape
    s_dim = s2 // 2
    halves = [(slice(h * d_ssm // 2, (h + 1) * d_ssm // 2), slice(h * s_dim // 2, (h + 1) * s_dim // 2),
               slice(s_dim + h * s_dim // 2, s_dim + (h + 1) * s_dim // 2)) for h in range(2)]
    for ch, st_re, st_im in halves:
        u_half = u_perm[:, ch]
        bu_ref[:, st_re] = jnp.dot(u_half, bmat_ref[ch, st_re], preferred_element_type=F32)
        bu_ref[:, st_im] = jnp.dot(u_half, bmat_ref[ch, st_im], preferred_element_type=F32)

    @pl.when(d == 0)
    def _():
        _scan_chunk(bu_ref, tab_ref.at[0], carry_ref, reverse=False, lane_group=lane_group)

    @pl.when(d == 1)
    def _():
        _scan_chunk(bu_ref, tab_ref.at[0], carry_ref, reverse=True, lane_group=lane_group)

    for ch, st_re, st_im in halves:
        y_ref[0, :, ch] = (
            jnp.dot(bu_ref[:, st_re].astype(BF16), cmat_ref[0, st_re, ch], preferred_element_type=F32)
            + jnp.dot(bu_ref[:, st_im].astype(BF16), cmat_ref[0, st_im, ch], preferred_element_type=F32))


def _segment_interleave(tc):
    dst = jnp.arange(tc)
    src = (dst % SUBLANES) * (tc // SUBLANES) + dst // SUBLANES
    return (src[:, None] == jnp.arange(tc)[None, :]).astype(BF16)


def _s5_scan(proj, bmat, tabs, cmat, *, batch, seq_len, u_col, d_ssm, tc):
    n = proj.shape[0]
    n_chunks = seq_len // tc
    s2 = bmat.shape[1]
    s_dim = s2 // 2
    lane_group = min(s_dim, 4 * LANES)

    def chunk_row(b, d, c):
        return b * n_chunks + jnp.where(d == 0, c, n_chunks - 1 - c)

    return pl.pallas_call(
        functools.partial(_s5_kernel, lane_group=lane_group),
        grid=(batch, 2, n_chunks),
        in_specs=[
            pl.BlockSpec((tc, d_ssm), lambda b, d, c: (chunk_row(b, d, c), u_col)),
            pl.BlockSpec((tc, tc), lambda b, d, c: (0, 0)),
            pl.BlockSpec((d_ssm, s2), lambda b, d, c: (0, 0)),
            pl.BlockSpec((1, 4, SUBLANES, s_dim), lambda b, d, c: (d, 0, 0, 0)),
            pl.BlockSpec((1, s2, d_ssm), lambda b, d, c: (d, 0, 0)),
        ],
        out_specs=pl.BlockSpec((1, tc, d_ssm), lambda b, d, c: (d, chunk_row(b, d, c), 0)),
        out_shape=jax.ShapeDtypeStruct((2, n, d_ssm), F32),
        scratch_shapes=[pltpu.VMEM((tc, s2), F32), pltpu.VMEM((2, SUBLANES, s_dim), F32)],
        compiler_params=_params(("parallel", "arbitrary", "arbitrary"), 52),
        name="s5_scan",
    )(proj, _segment_interleave(tc), bmat, tabs, cmat)


def _glu_kernel(ys_ref, u_ref, dskip_ref, w_ref, b_ref, o_ref, y_ref, *, tc):
    n_lane_tiles, tm, _ = y_ref.shape
    y_sum = ys_ref[0] + ys_ref[1]
    for j in range(n_lane_tiles):
        y_ref[j] = y_sum[:, j * LANES:(j + 1) * LANES]
    n_steps = tc // SUBLANES
    pieces = [jnp.concatenate([y_ref[j, pl.ds(base + g, n_steps, stride=SUBLANES), :]
                               for j in range(n_lane_tiles)], axis=1)
              for base in range(0, tm, tc) for g in range(SUBLANES)]
    y = jnp.concatenate(pieces, axis=0) + dskip_ref[...] * u_ref[...].astype(F32)
    y = jax.nn.gelu(y)
    z = jnp.dot(y.astype(BF16), w_ref[...], preferred_element_type=F32) + b_ref[...]
    o_ref[...] = (y * jax.nn.sigmoid(z)).astype(o_ref.dtype)


def _s5_glu(ys, proj, dskip, w, b, *, u_col, tm, tc):
    _, n, d_ssm = ys.shape
    assert tm % tc == 0
    return pl.pallas_call(
        functools.partial(_glu_kernel, tc=tc),
        scratch_shapes=[pltpu.VMEM((d_ssm // min(d_ssm, LANES), tm, min(d_ssm, LANES)), F32)],
        grid=(n // tm,),
        in_specs=[
            pl.BlockSpec((2, tm, d_ssm), lambda i: (0, i, 0)),
            pl.BlockSpec((tm, d_ssm), lambda i: (i, u_col)),
            pl.BlockSpec((1, d_ssm), lambda i: (0, 0)),
            pl.BlockSpec((d_ssm, d_ssm), lambda i: (0, 0)),
            pl.BlockSpec((1, d_ssm), lambda i: (0, 0)),
        ],
        out_specs=pl.BlockSpec((tm, d_ssm), lambda i: (i, 0)),
        out_shape=jax.ShapeDtypeStruct((n, d_ssm), BF16),
        compiler_params=_params(("parallel",), 40),
        name="s5_glu",
    )(ys, proj, dskip, w, b)


def _attn_kernel(q_ref, k_ref, v_ref, lam_ref, sg_ref, o_ref, *, lambda_init, head_dim, kc):
    q = q_ref[0]
    tq = q.shape[0]
    lane = lax.broadcasted_iota(jnp.int32, q.shape, 1)
    zero = jnp.zeros_like(q)
    qq = jnp.concatenate([jnp.where(lane < head_dim, q, zero), jnp.where(lane >= head_dim, q, zero)], axis=0)
    n_chunks = k_ref.shape[1] // kc
    sub = 2 * SUBLANES

    def scores(c):
        return lax.dot_general(k_ref[0, pl.ds(c * kc, kc), :], qq, (((1,), (1,)), ((), ())),
                               preferred_element_type=F32)

    def exp_stage(s, m_new):
        pieces, part = [], None
        for r in range(0, kc, sub):
            e_r = jnp.exp2(s[r:r + sub] - m_new)
            pieces.append(e_r.astype(BF16))
            for g in range(sub // SUBLANES):
                tile = e_r[g * SUBLANES:(g + 1) * SUBLANES]
                part = tile if part is None else part + tile
        return jnp.concatenate(pieces, axis=0), jnp.sum(part, axis=0, keepdims=True)

    def value_stage(acc, item):
        c, e_b, alpha = item
        pv = lax.dot_general(v_ref[0, pl.ds(c * kc, kc), :], e_b, (((0,), (0,)), ((), ())),
                             preferred_element_type=F32)
        return pv if acc is None else acc * alpha + pv

    score_lead = 2
    m = l = acc = None
    s_q = [scores(c) for c in range(min(score_lead, n_chunks))]
    pending = None
    for c in range(n_chunks):
        if c + score_lead < n_chunks:
            s_q.append(scores(c + score_lead))
        s = s_q[c]
        m_c = jnp.max(s, axis=0, keepdims=True)
        m_new = m_c if m is None else jnp.maximum(m, m_c)
        e_b, l_c = exp_stage(s, m_new)
        alpha = None if m is None else jnp.exp2(m - m_new)
        l = l_c if m is None else l * alpha + l_c
        if pending is not None:
            acc = value_stage(acc, pending)
        pending = (c, e_b, alpha)
        m = m_new
    acc = value_stage(acc, pending)
    lv = lam_ref[...]
    lam = (jnp.exp(jnp.sum(lv[0:1] * lv[1:2], axis=-1, keepdims=True))
           - jnp.exp(jnp.sum(lv[2:3] * lv[3:4], axis=-1, keepdims=True)) + lambda_init)
    o_t = acc[:, 0:tq] * (1.0 / l[:, 0:tq]) - acc[:, tq:2 * tq] * (lam / l[:, tq:2 * tq])
    o_ref[0] = (_rms(o_t.T, sg_ref[...]) * (1.0 - lambda_init)).astype(o_ref.dtype)


def _attention(proj3, lam_vec, subln, *, q_col, k_col, v_col, n_heads, head_dim, lambda_init, tq):
    batch, seq_len, _ = proj3.shape
    vd = 2 * head_dim
    return pl.pallas_call(
        functools.partial(_attn_kernel, lambda_init=lambda_init, head_dim=head_dim, kc=_tile(seq_len, 512)),
        grid=(batch, n_heads, seq_len // tq),
        in_specs=[
            pl.BlockSpec((1, tq, vd), lambda b, h, i: (b, i, q_col + h)),
            pl.BlockSpec((1, seq_len, vd), lambda b, h, i: (b, 0, k_col + h)),
            pl.BlockSpec((1, seq_len, vd), lambda b, h, i: (b, 0, v_col + h)),
            pl.BlockSpec((4, head_dim), lambda b, h, i: (0, 0)),
            pl.BlockSpec((1, vd), lambda b, h, i: (0, 0)),
        ],
        out_specs=pl.BlockSpec((1, tq, vd), lambda b, h, i: (b, i, h)),
        out_shape=jax.ShapeDtypeStruct((batch, seq_len, n_heads * vd), BF16),
        compiler_params=_params(("parallel", "parallel", "arbitrary"), 52),
        name="diff_attn",
    )(proj3, proj3, proj3, lam_vec, subln)


def _conv_kernel(bg_ref, cg_ref, xv_ref, w_ref, o_ref):
    z = cg_ref[0].astype(F32) * xv_ref[0].astype(F32)
    seq_len = z.shape[0]
    row = lax.broadcasted_iota(jnp.int32, z.shape, 0)
    z_prev = jnp.where(row == 0, 0.0, pltpu.roll(z, 1, axis=0))
    z_next = jnp.where(row == seq_len - 1, 0.0, pltpu.roll(z, seq_len - 1, axis=0))
    w = w_ref[...]
    zc = w[0:1] * z_prev + w[1:2] * z + w[2:3] * z_next
    o_ref[0] = (bg_ref[0].astype(F32) * zc).astype(o_ref.dtype)


def _short_conv(proj3, conv_w, *, bg_col, d_conv):
    batch, seq_len, _ = proj3.shape
    tc = min(d_conv, LANES)
    per = d_conv // tc
    return pl.pallas_call(
        _conv_kernel,
        grid=(batch, per),
        in_specs=[
            pl.BlockSpec((1, seq_len, tc), lambda b, j: (b, 0, bg_col * per + j)),
            pl.BlockSpec((1, seq_len, tc), lambda b, j: (b, 0, (bg_col + 1) * per + j)),
            pl.BlockSpec((1, seq_len, tc), lambda b, j: (b, 0, (bg_col + 2) * per + j)),
            pl.BlockSpec((conv_w.shape[0], tc), lambda b, j: (0, j)),
        ],
        out_specs=pl.BlockSpec((1, seq_len, tc), lambda b, j: (b, 0, j)),
        out_shape=jax.ShapeDtypeStruct((batch, seq_len, d_conv), BF16),
        compiler_params=_params(("parallel", "parallel"), 40),
        name="short_conv",
    )(proj3, proj3, proj3, conv_w)


def _merge_kernel(x_ref, ya_ref, yb_ref, yc_ref, ga_ref, gb_ref, gc_ref, wb_ref, wo_ref, o_ref):
    r_a = ya_ref.shape[1]
    r_b = r_a + yb_ref.shape[1]
    pa = jnp.dot(ya_ref[...], wb_ref[0:r_a, :], preferred_element_type=F32)
    merged = ga_ref[...].astype(F32) * pa
    pb = jnp.dot(yb_ref[...], wb_ref[r_a:r_b, :], preferred_element_type=F32)
    merged += gb_ref[...].astype(F32) * pb
    pc = jnp.dot(yc_ref[...], wb_ref[r_b:, :], preferred_element_type=F32)
    merged += gc_ref[...].astype(F32) * pc
    o_ref[...] = x_ref[...] + jnp.dot(merged.astype(BF16), wo_ref[...], preferred_element_type=F32)


def _merge(x, ya, yb, yc, proj, wb, wo, *, tm):
    n, d = x.shape
    row = lambda i: (i, 0)
    const = lambda i: (0, 0)
    return pl.pallas_call(
        _merge_kernel,
        grid=(n // tm,),
        in_specs=[
            pl.BlockSpec((tm, d), row),
            pl.BlockSpec((tm, ya.shape[1]), row),
            pl.BlockSpec((tm, yb.shape[1]), row),
            pl.BlockSpec((tm, yc.shape[1]), row),
            pl.BlockSpec((tm, d), lambda i: (i, 0)),
            pl.BlockSpec((tm, d), lambda i: (i, 1)),
            pl.BlockSpec((tm, d), lambda i: (i, 2)),
            pl.BlockSpec(wb.shape, const),
            pl.BlockSpec(wo.shape, const),
        ],
        out_specs=pl.BlockSpec((tm, d), row),
        out_shape=jax.ShapeDtypeStruct((n, d), F32),
        compiler_params=_params(("parallel",), 56),
        name="merge",
    )(x, ya, yb, yc, proj, proj, proj, wb, wo)


def _rope_tables(seq_len, head_dim):
    pos = jnp.arange(seq_len, dtype=F32)
    inv = ROPE_THETA ** (-jnp.arange(0, head_dim, 2, dtype=F32) / head_dim)
    ang = pos[:, None] * inv[None, :]
    cos, sin = jnp.cos(ang), jnp.sin(ang)
    reps = LANES // head_dim
    return (jnp.tile(jnp.concatenate([cos, cos], axis=1), (1, reps)),
            jnp.tile(jnp.concatenate([-sin, sin], axis=1), (1, reps)))


def _block_diag(blocks):
    g, r, c = blocks.shape
    wide = jnp.transpose(blocks, (1, 0, 2)).reshape(r, g * c)
    row_g = jnp.arange(g * r)[:, None] // r
    col_g = jnp.arange(g * c)[None, :] // c
    return jnp.where(row_g == col_g, jnp.tile(wide, (g, 1)), 0.0)


def _s5_tables(lam_re, lam_im, log_dt, b_re, b_im, c_re, c_im, n_steps):
    bmat = jnp.concatenate([_block_diag(jnp.swapaxes(b_re, 1, 2)), _block_diag(jnp.swapaxes(b_im, 1, 2))], axis=1)
    cmats, tabs = [], []
    for d in range(2):
        lr, li = lam_re[d], lam_im[d]
        dt = jnp.exp(log_dt[d])[:, None]
        mag = jnp.exp(dt * lr)
        ar, ai = mag * jnp.cos(dt * li), mag * jnp.sin(dt * li)
        denom = lr * lr + li * li
        nr = ar - 1.0
        kr = (nr * lr + ai * li) / denom
        ki = (ai * lr - nr * li) / denom
        er = c_re * kr[:, None, :] - c_im * ki[:, None, :]
        ei = c_re * ki[:, None, :] + c_im * kr[:, None, :]
        cmats.append(jnp.concatenate([_block_diag(jnp.swapaxes(er, 1, 2)), _block_diag(jnp.swapaxes(-ei, 1, 2))],
                                     axis=0))
        ar, ai = ar.reshape(-1), ai.reshape(-1)
        pr, pi, sr, si, e = jnp.ones_like(ar), jnp.zeros_like(ai), ar, ai, n_steps
        while e:
            if e & 1:
                pr, pi = pr * sr - pi * si, pr * si + pi * sr
            sr, si = sr * sr - si * si, 2.0 * sr * si
            e >>= 1
        tabs.append(jnp.stack([jnp.broadcast_to(v[None, :], (SUBLANES, v.shape[0])) for v in (ar, ai, pr, pi)]))
    return bmat.astype(BF16), jnp.stack(tabs), jnp.stack(cmats).astype(BF16)


def kernel(x, norm_w, ffn_w13, ffn_w2, w_in, s5_lambda_re, s5_lambda_im, s5_log_dt, s5_b_re, s5_b_im, s5_c_re, s5_c_im, s5_d, s5_w_glu, s5_b_glu, diff_lambda, diff_subln, conv_w, w_branch, w_gate, b_gate, w_out, final_norm):
    batch, seq_len, d_model = x.shape
    depth = norm_w.shape[0]
    d_ff = ffn_w2.shape[2]
    d_ssm = s5_d.shape[-1]
    d_conv = conv_w.shape[-1]
    d_attn = (w_in.shape[-1] - d_ssm - 3 * d_conv) // 3
    head_dim = diff_lambda.shape[-1]
    n_heads = d_attn // (2 * head_dim)
    n = batch * seq_len
    assert 2 * head_dim == LANES and d_ssm == d_conv and d_attn == 2 * d_ssm

    tm_ffn = _tile(n, 1024)
    tf = min(256, d_ff)
    tm_proj = _tile(seq_len, 1024)
    tn_proj = d_attn
    tm_merge = _tile(n, 256)
    tq = _tile(seq_len, 512)
    t_chunk = _tile(seq_len, 512)

    cos, sin = _rope_tables(seq_len, head_dim)
    xf = x.reshape(n, d_model)
    gate_cols = N_BRANCH * d_model
    u_col = gate_cols // d_ssm
    q_col = (gate_cols + d_ssm) // LANES
    k_col = q_col + d_attn // LANES
    v_col = k_col + d_attn // LANES
    bg_col = (gate_cols + d_ssm + 3 * d_attn) // d_conv
    b_gate3 = b_gate[:, None, :]

    for l in range(depth):
        lambda_init = 0.8 - 0.6 * math.exp(-0.3 * l)
        bmat, tabs, cmat = _s5_tables(s5_lambda_re[l], s5_lambda_im[l], s5_log_dt[l],
                                      s5_b_re[l], s5_b_im[l], s5_c_re[l], s5_c_im[l], t_chunk // SUBLANES)

        xf, h_mix = _ffn(xf, norm_w[l, 0][None, :], ffn_w13, ffn_w2, norm_w[l, 1][None, :], layer=l, idx=0,
                         post="next", tm=tm_ffn, tf=tf)

        proj = _proj(h_mix, w_gate, b_gate3, w_in, cos, sin, layer=l, seq_len=seq_len, u_cols=d_ssm,
                     attn_cols=d_attn, q_scale=head_dim ** -0.5 * math.log2(math.e), tm=tm_proj, tn=tn_proj)
        proj3 = proj.reshape(batch, seq_len, proj.shape[1])

        ys = _s5_scan(proj, bmat, tabs, cmat, batch=batch, seq_len=seq_len, u_col=u_col, d_ssm=d_ssm,
                      tc=t_chunk)
        y_a = _s5_glu(ys, proj, s5_d[l][None, :], s5_w_glu[l].astype(BF16), s5_b_glu[l][None, :],
                      u_col=u_col, tm=_tile(seq_len, 1024), tc=t_chunk)
        y_b = _attention(proj3, diff_lambda[l], diff_subln[l][None, :], q_col=q_col, k_col=k_col, v_col=v_col,
                         n_heads=n_heads, head_dim=head_dim, lambda_init=lambda_init, tq=tq)
        y_c = _short_conv(proj3, conv_w[l], bg_col=bg_col, d_conv=d_conv)

        xf = _merge(xf, y_a, y_b.reshape(n, d_attn), y_c.reshape(n, d_conv), proj,
                    w_branch[l].astype(BF16), w_out[l].astype(BF16), tm=tm_merge)

        xf = _ffn(xf, norm_w[l, 2][None, :], ffn_w13, ffn_w2, final_norm[None, :], layer=l, idx=1,
                  post="final" if l == depth - 1 else "none", tm=tm_ffn, tf=tf)
    return xf.reshape(batch, seq_len, d_model)
```

```python
import functools
import math

import jax
import jax.numpy as jnp
from jax import lax
from jax.experimental import pallas as pl
from jax.experimental.pallas import tpu as pltpu

NORM_EPS = 1e-6
ROPE_THETA = 10000.0
N_BRANCH = 3
LANES = 128
SUBLANES = 8
MIB = 1024 * 1024
F32 = jnp.float32
BF16 = jnp.bfloat16


def _rms(x, g):
    return x * lax.rsqrt(jnp.mean(x * x, axis=-1, keepdims=True) + NORM_EPS) * g


def _round_up(a, b):
    return (a + b - 1) // b * b


def _tile(n, want):
    t = min(n, want)
    while n % t:
        t -= 1
    return t


def _params(sem, vmem_mib):
    return pltpu.CompilerParams(dimension_semantics=sem, vmem_limit_bytes=vmem_mib * MIB)


def _ffn_kernel(x_ref, g_ref, wa_ref, *rest, post, d_ff, n_units):
    wb_refs = rest[:n_units]
    w2_refs = rest[n_units:2 * n_units]
    if post == "next":
        fg_ref, o_ref, hn_ref = rest[2 * n_units:]
        h_ref = hn_ref
    else:
        fg_ref, o_ref, h_ref = rest[2 * n_units:]
    j = pl.program_id(1)
    tf = wa_ref.shape[1]

    @pl.when(j == 0)
    def _():
        h_ref[...] = _rms(x_ref[...], g_ref[...]).astype(BF16)
        o_ref[...] = jnp.zeros_like(o_ref)

    h = h_ref[...]
    a = jnp.dot(h, wa_ref[...].astype(BF16), preferred_element_type=F32)
    wb = jnp.concatenate([r[...].astype(BF16) for r in wb_refs], axis=1)
    b = jnp.dot(h, wb, preferred_element_type=F32)
    col = j * tf + lax.broadcasted_iota(jnp.int32, a.shape, 1)
    act = jnp.where(col < d_ff, a * jax.nn.sigmoid(a) * b, 0.0).astype(BF16)
    w2 = jnp.concatenate([r[...].astype(BF16) for r in w2_refs], axis=0)
    o_ref[...] += jnp.dot(act, w2, preferred_element_type=F32)

    @pl.when(j == pl.num_programs(1) - 1)
    def _():
        y = x_ref[...] + 0.5 * o_ref[...]
        if post == "final":
            y = _rms(y, fg_ref[...])
        if post == "next":
            hn_ref[...] = _rms(y, fg_ref[...]).astype(BF16)
        o_ref[...] = y


def _ffn(x, g, w13, w2, fg, *, layer, idx, post, tm, tf):
    n, d = x.shape
    row_blk = pl.BlockSpec((tm, d), lambda i, j: (i, 0))
    out_specs, out_shape = row_blk, jax.ShapeDtypeStruct((n, d), F32)
    if post == "next":
        out_specs, out_shape = [row_blk, row_blk], [out_shape, jax.ShapeDtypeStruct((n, d), BF16)]
    d_ff = w2.shape[2]
    assert d_ff % LANES == 0 and tf % LANES == 0 and 2 * d_ff >= _round_up(d_ff, tf)
    n_units = tf // LANES
    ff_units = d_ff // LANES

    def wb_spec(r):
        return pl.BlockSpec((None, None, d, LANES), lambda i, j: (
            layer, idx, 0, ff_units + jnp.minimum(j * n_units + r, ff_units - 1)))

    def w2_spec(r):
        return pl.BlockSpec((None, None, LANES, d), lambda i, j: (
            layer, idx, jnp.minimum(j * n_units + r, ff_units - 1), 0))

    return pl.pallas_call(
        functools.partial(_ffn_kernel, post=post, d_ff=d_ff, n_units=n_units),
        grid=(n // tm, pl.cdiv(d_ff, tf)),
        in_specs=[
            row_blk,
            pl.BlockSpec((1, d), lambda i, j: (0, 0)),
            pl.BlockSpec((None, None, d, tf), lambda i, j: (layer, idx, 0, j)),
            *[wb_spec(r) for r in range(n_units)],
            *[w2_spec(r) for r in range(n_units)],
            pl.BlockSpec((1, d), lambda i, j: (0, 0)),
        ],
        out_specs=out_specs,
        out_shape=out_shape,
        scratch_shapes=[] if post == "next" else [pltpu.VMEM((tm, d), BF16)],
        compiler_params=_params(("parallel", "arbitrary"), 62),
        name="ffn",
    )(x, g, w13, *([w13] * n_units), *([w2] * n_units), fg)


def _rope(acc, cos, sin, scale):
    lane = lax.broadcasted_iota(jnp.int32, cos.shape, 1)
    first_half = (lane & (LANES // 4)) == 0
    out = []
    for c in range(acc.shape[1] // LANES):
        blk = acc[:, c * LANES:(c + 1) * LANES]
        partner = jnp.where(first_half,
                            pltpu.roll(blk, LANES - LANES // 4, axis=1),
                            pltpu.roll(blk, LANES // 4, axis=1))
        out.append((blk * cos + partner * sin) * scale)
    return jnp.concatenate(out, axis=1)


def _proj_kernel(h_ref, wg_ref, wi_ref, b_ref, cos_ref, sin_ref, o_ref, *,
                 n_gate, n_in, u_cols, attn_cols, q_scale, chunk):
    j = pl.program_id(0)
    tn = o_ref.shape[1]

    def run(w_ref, epilogue):
        for c in range(tn // chunk):
            cols = slice(c * chunk, (c + 1) * chunk)
            acc = jnp.dot(h_ref[...], w_ref[:, cols].astype(BF16), preferred_element_type=F32)
            o_ref[:, cols] = epilogue(acc, c).astype(o_ref.dtype)

    @pl.when(j < n_gate)
    def _():
        run(wg_ref, lambda acc, c: jax.nn.sigmoid(acc + b_ref[:, c * chunk:(c + 1) * chunk]))

    def in_epilogue(blk):
        def epilogue(acc, c):
            col = blk * tn + c * chunk
            if u_cols <= col < u_cols + attn_cols:
                return _rope(acc, cos_ref[...], sin_ref[...], q_scale)
            if u_cols + attn_cols <= col < u_cols + 2 * attn_cols:
                return _rope(acc, cos_ref[...], sin_ref[...], 1.0)
            return acc
        return epilogue

    for blk in range(n_in):
        @pl.when(j == n_gate + blk)
        def _(blk=blk):
            run(wi_ref, in_epilogue(blk))


def _proj(h, w_gate, b_gate, w_in, cos, sin, *, layer, seq_len, u_cols, attn_cols, q_scale, tm, tn):
    n, d = h.shape
    n_gate = w_gate.shape[2] // tn
    n_in = w_in.shape[2] // tn
    chunk = min(u_cols, 2 * LANES)
    assert u_cols % chunk == 0 and attn_cols % chunk == 0 and tn % chunk == 0
    pos_blocks = seq_len // tm
    gate_blk = lambda j, i: (layer, 0, jnp.minimum(j, n_gate - 1))
    return pl.pallas_call(
        functools.partial(_proj_kernel, n_gate=n_gate, n_in=n_in, u_cols=u_cols, attn_cols=attn_cols,
                          q_scale=q_scale, chunk=chunk),
        grid=(n_gate + n_in, n // tm),
        in_specs=[
            pl.BlockSpec((tm, d), lambda j, i: (i, 0)),
            pl.BlockSpec((None, d, tn), gate_blk),
            pl.BlockSpec((None, d, tn), lambda j, i: (layer, 0, jnp.maximum(j - n_gate, 0))),
            pl.BlockSpec((None, 1, tn), gate_blk),
            pl.BlockSpec((tm, LANES), lambda j, i: (i % pos_blocks, 0)),
            pl.BlockSpec((tm, LANES), lambda j, i: (i % pos_blocks, 0)),
        ],
        out_specs=pl.BlockSpec((tm, tn), lambda j, i: (i, j)),
        out_shape=jax.ShapeDtypeStruct((n, (n_gate + n_in) * tn), BF16),
        compiler_params=_params(("arbitrary", "arbitrary"), 56),
        name="proj",
    )(h, w_gate, w_in, b_gate, cos, sin)


def _scan_chunk(bu_ref, tab_ref, carry_ref, *, reverse, lane_group):
    t_len, s2 = bu_ref.shape
    s_dim = s2 // 2
    n_steps = t_len // SUBLANES
    first, last = (SUBLANES - 1, 0) if reverse else (0, SUBLANES - 1)
    toward_later = SUBLANES - 1 if reverse else 1
    for lg in range(s_dim // lane_group):
        re_cols = pl.ds(lg * lane_group, lane_group)
        im_cols = pl.ds(s_dim + lg * lane_group, lane_group)
        a_re, a_im = tab_ref[0, :, re_cols], tab_ref[1, :, re_cols]
        an_re, an_im = tab_ref[2, :, re_cols], tab_ref[3, :, re_cols]

        def rows_of(it):
            step = (n_steps - 1 - it) if reverse else it
            return pl.ds(pl.multiple_of(step * SUBLANES, SUBLANES), SUBLANES)

        def local_scan(it, state):
            s_re, s_im = state
            rows = rows_of(it)
            s_re, s_im = (a_re * s_re - a_im * s_im + bu_ref[rows, re_cols],
                          a_re * s_im + a_im * s_re + bu_ref[rows, im_cols])
            bu_ref[rows, re_cols] = s_re
            bu_ref[rows, im_cols] = s_im
            return s_re, s_im

        zero = jnp.zeros((SUBLANES, lane_group), F32)
        e_re, e_im = lax.fori_loop(0, n_steps, local_scan, (zero, zero), unroll=4)

        row = lax.broadcasted_iota(jnp.int32, (SUBLANES, lane_group), 0)
        c_re = jnp.where(row == first, carry_ref[0, :, re_cols], 0.0)
        c_im = jnp.where(row == first, carry_ref[1, :, re_cols], 0.0)

        def segment_end(c_re, c_im):
            return e_re + an_re * c_re - an_im * c_im, e_im + an_re * c_im + an_im * c_re

        order = range(SUBLANES - 2, -1, -1) if reverse else range(1, SUBLANES)
        for k in order:
            t_re, t_im = segment_end(c_re, c_im)
            c_re = jnp.where(row == k, pltpu.roll(t_re, toward_later, axis=0), c_re)
            c_im = jnp.where(row == k, pltpu.roll(t_im, toward_later, axis=0), c_im)
        t_re, t_im = segment_end(c_re, c_im)
        carry_ref[0, :, re_cols] = jnp.broadcast_to(t_re[last:last + 1], t_re.shape)
        carry_ref[1, :, re_cols] = jnp.broadcast_to(t_im[last:last + 1], t_im.shape)

        def add_incoming(it, f):
            f_re, f_im = f
            rows = rows_of(it)
            f_re, f_im = a_re * f_re - a_im * f_im, a_re * f_im + a_im * f_re
            bu_ref[rows, re_cols] += f_re
            bu_ref[rows, im_cols] += f_im
            return f_re, f_im

        lax.fori_loop(0, n_steps, add_incoming, (c_re, c_im), unroll=4)


def _s5_kernel(u_ref, perm_ref, bmat_ref, tab_ref, cmat_ref, y_ref, bu_ref, carry_ref, *, lane_group):
    d = pl.program_id(1)
    c = pl.program_id(2)

    @pl.when(c == 0)
    def _():
        carry_ref[...] = jnp.zeros_like(carry_ref)

    u_perm = jnp.dot(perm_ref[...], u_ref[...], preferred_element_type=F32).astype(BF16)

    d_ssm, s2 = bmat_ref.shape
    s_dim = s2 // 2
    halves = [(slice(h * d_ssm // 2, (h + 1) * d_ssm // 2), slice(h * s_dim // 2, (h + 1) * s_dim // 2),
               slice(s_dim + h * s_dim // 2, s_dim + (h + 1) * s_dim // 2)) for h in range(2)]
    for ch, st_re, st_im in halves:
        u_half = u_perm[:, ch]
        bu_ref[:, st_re] = jnp.dot(u_half, bmat_ref[ch, st_re], preferred_element_type=F32)
        bu_ref[:, st_im] = jnp.dot(u_half, bmat_ref[ch, st_im], preferred_element_type=F32)

    @pl.when(d == 0)
    def _():
        _scan_chunk(bu_ref, tab_ref.at[0], carry_ref, reverse=False, lane_group=lane_group)

    @pl.when(d == 1)
    def _():
        _scan_chunk(bu_ref, tab_ref.at[0], carry_ref, reverse=True, lane_group=lane_group)

    for ch, st_re, st_im in halves:
        y_ref[0, :, ch] = (
            jnp.dot(bu_ref[:, st_re].astype(BF16), cmat_ref[0, st_re, ch], preferred_element_type=F32)
            + jnp.dot(bu_ref[:, st_im].astype(BF16), cmat_ref[0, st_im, ch], preferred_element_type=F32))


def _segment_interleave(tc):
    dst = jnp.arange(tc)
    src = (dst % SUBLANES) * (tc // SUBLANES) + dst // SUBLANES
    return (src[:, None] == jnp.arange(tc)[None, :]).astype(BF16)


def _s5_scan(proj, bmat, tabs, cmat, *, batch, seq_len, u_col, d_ssm, tc):
    n = proj.shape[0]
    n_chunks = seq_len // tc
    s2 = bmat.shape[1]
    s_dim = s2 // 2
    lane_group = min(s_dim, 4 * LANES)

    def chunk_row(b, d, c):
        return b * n_chunks + jnp.where(d == 0, c, n_chunks - 1 - c)

    return pl.pallas_call(
        functools.partial(_s5_kernel, lane_group=lane_group),
        grid=(batch, 2, n_chunks),
        in_specs=[
            pl.BlockSpec((tc, d_ssm), lambda b, d, c: (chunk_row(b, d, c), u_col)),
            pl.BlockSpec((tc, tc), lambda b, d, c: (0, 0)),
            pl.BlockSpec((d_ssm, s2), lambda b, d, c: (0, 0)),
            pl.BlockSpec((1, 4, SUBLANES, s_dim), lambda b, d, c: (d, 0, 0, 0)),
            pl.BlockSpec((1, s2, d_ssm), lambda b, d, c: (d, 0, 0)),
        ],
        out_specs=pl.BlockSpec((1, tc, d_ssm), lambda b, d, c: (d, chunk_row(b, d, c), 0)),
        out_shape=jax.ShapeDtypeStruct((2, n, d_ssm), F32),
        scratch_shapes=[pltpu.VMEM((tc, s2), F32), pltpu.VMEM((2, SUBLANES, s_dim), F32)],
        compiler_params=_params(("parallel", "arbitrary", "arbitrary"), 52),
        name="s5_scan",
    )(proj, _segment_interleave(tc), bmat, tabs, cmat)


def _glu_kernel(ys_ref, u_ref, dskip_ref, w_ref, b_ref, o_ref, y_ref, *, tc):
    n_lane_tiles, tm, _ = y_ref.shape
    y_sum = ys_ref[0] + ys_ref[1]
    for j in range(n_lane_tiles):
        y_ref[j] = y_sum[:, j * LANES:(j + 1) * LANES]
    n_steps = tc // SUBLANES
    pieces = [jnp.concatenate([y_ref[j, pl.ds(base + g, n_steps, stride=SUBLANES), :]
                               for j in range(n_lane_tiles)], axis=1)
              for base in range(0, tm, tc) for g in range(SUBLANES)]
    y = jnp.concatenate(pieces, axis=0) + dskip_ref[...] * u_ref[...].astype(F32)
    y = jax.nn.gelu(y)
    z = jnp.dot(y.astype(BF16), w_ref[...], preferred_element_type=F32) + b_ref[...]
    o_ref[...] = (y * jax.nn.sigmoid(z)).astype(o_ref.dtype)


def _s5_glu(ys, proj, dskip, w, b, *, u_col, tm, tc):
    _, n, d_ssm = ys.shape
    assert tm % tc == 0
    return pl.pallas_call(
        functools.partial(_glu_kernel, tc=tc),
        scratch_shapes=[pltpu.VMEM((d_ssm // min(d_ssm, LANES), tm, min(d_ssm, LANES)), F32)],
        grid=(n // tm,),
        in_specs=[
            pl.BlockSpec((2, tm, d_ssm), lambda i: (0, i, 0)),
            pl.BlockSpec((tm, d_ssm), lambda i: (i, u_col)),
            pl.BlockSpec((1, d_ssm), lambda i: (0, 0)),
            pl.BlockSpec((d_ssm, d_ssm), lambda i: (0, 0)),
            pl.BlockSpec((1, d_ssm), lambda i: (0, 0)),
        ],
        out_specs=pl.BlockSpec((tm, d_ssm), lambda i: (i, 0)),
        out_shape=jax.ShapeDtypeStruct((n, d_ssm), BF16),
        compiler_params=_params(("parallel",), 40),
        name="s5_glu",
    )(ys, proj, dskip, w, b)


def _attn_kernel(q_ref, k_ref, v_ref, lam_ref, sg_ref, o_ref, *, lambda_init, head_dim, kc):
    q = q_ref[0]
    tq = q.shape[0]
    lane = lax.broadcasted_iota(jnp.int32, q.shape, 1)
    zero = jnp.zeros_like(q)
    qq = jnp.concatenate([jnp.where(lane < head_dim, q, zero), jnp.where(lane >= head_dim, q, zero)], axis=0)
    n_chunks = k_ref.shape[1] // kc
    sub = 2 * SUBLANES

    def scores(c):
        return lax.dot_general(k_ref[0, pl.ds(c * kc, kc), :], qq, (((1,), (1,)), ((), ())),
                               preferred_element_type=F32)

    def exp_stage(s, m_new):
        pieces, part = [], None
        for r in range(0, kc, sub):
            e_r = jnp.exp2(s[r:r + sub] - m_new)
            pieces.append(e_r.astype(BF16))
            for g in range(sub // SUBLANES):
                tile = e_r[g * SUBLANES:(g + 1) * SUBLANES]
                part = tile if part is None else part + tile
        return jnp.concatenate(pieces, axis=0), jnp.sum(part, axis=0, keepdims=True)

    def value_stage(acc, item):
        c, e_b, alpha = item
        pv = lax.dot_general(v_ref[0, pl.ds(c * kc, kc), :], e_b, (((0,), (0,)), ((), ())),
                             preferred_element_type=F32)
        return pv if acc is None else acc * alpha + pv

    score_lead = 2
    m = l = acc = None
    s_q = [scores(c) for c in range(min(score_lead, n_chunks))]
    pending = None
    for c in range(n_chunks):
        if c + score_lead < n_chunks:
            s_q.append(scores(c + score_lead))
        s = s_q[c]
        m_c = jnp.max(s, axis=0, keepdims=True)
        m_new = m_c if m is None else jnp.maximum(m, m_c)
        e_b, l_c = exp_stage(s, m_new)
        alpha = None if m is None else jnp.exp2(m - m_new)
        l = l_c if m is None else l * alpha + l_c
        if pending is not None:
            acc = value_stage(acc, pending)
        pending = (c, e_b, alpha)
        m = m_new
    acc = value_stage(acc, pending)
    lv = lam_ref[...]
    lam = (jnp.exp(jnp.sum(lv[0:1] * lv[1:2], axis=-1, keepdims=True))
           - jnp.exp(jnp.sum(lv[2:3] * lv[3:4], axis=-1, keepdims=True)) + lambda_init)
    o_t = acc[:, 0:tq] * (1.0 / l[:, 0:tq]) - acc[:, tq:2 * tq] * (lam / l[:, tq:2 * tq])
    o_ref[0] = (_rms(o_t.T, sg_ref[...]) * (1.0 - lambda_init)).astype(o_ref.dtype)


def _attention(proj3, lam_vec, subln, *, q_col, k_col, v_col, n_heads, head_dim, lambda_init, tq):
    batch, seq_len, _ = proj3.shape
    vd = 2 * head_dim
    return pl.pallas_call(
        functools.partial(_attn_kernel, lambda_init=lambda_init, head_dim=head_dim, kc=_tile(seq_len, 512)),
        grid=(batch, n_heads, seq_len // tq),
        in_specs=[
            pl.BlockSpec((1, tq, vd), lambda b, h, i: (b, i, q_col + h)),
            pl.BlockSpec((1, seq_len, vd), lambda b, h, i: (b, 0, k_col + h)),
            pl.BlockSpec((1, seq_len, vd), lambda b, h, i: (b, 0, v_col + h)),
            pl.BlockSpec((4, head_dim), lambda b, h, i: (0, 0)),
            pl.BlockSpec((1, vd), lambda b, h, i: (0, 0)),
        ],
        out_specs=pl.BlockSpec((1, tq, vd), lambda b, h, i: (b, i, h)),
        out_shape=jax.ShapeDtypeStruct((batch, seq_len, n_heads * vd), BF16),
        compiler_params=_params(("parallel", "parallel", "arbitrary"), 52),
        name="diff_attn",
    )(proj3, proj3, proj3, lam_vec, subln)


def _conv_kernel(bg_ref, cg_ref, xv_ref, w_ref, o_ref):
    z = cg_ref[0].astype(F32) * xv_ref[0].astype(F32)
    seq_len = z.shape[0]
    row = lax.broadcasted_iota(jnp.int32, z.shape, 0)
    z_prev = jnp.where(row == 0, 0.0, pltpu.roll(z, 1, axis=0))
    z_next = jnp.where(row == seq_len - 1, 0.0, pltpu.roll(z, seq_len - 1, axis=0))
    w = w_ref[...]
    zc = w[0:1] * z_prev + w[1:2] * z + w[2:3] * z_next
    o_ref[0] = (bg_ref[0].astype(F32) * zc).astype(o_ref.dtype)


def _short_conv(proj3, conv_w, *, bg_col, d_conv):
    batch, seq_len, _ = proj3.shape
    tc = min(d_conv, LANES)
    per = d_conv // tc
    return pl.pallas_call(
        _conv_kernel,
        grid=(batch, per),
        in_specs=[
            pl.BlockSpec((1, seq_len, tc), lambda b, j: (b, 0, bg_col * per + j)),
            pl.BlockSpec((1, seq_len, tc), lambda b, j: (b, 0, (bg_col + 1) * per + j)),
            pl.BlockSpec((1, seq_len, tc), lambda b, j: (b, 0, (bg_col + 2) * per + j)),
            pl.BlockSpec((conv_w.shape[0], tc), lambda b, j: (0, j)),
        ],
        out_specs=pl.BlockSpec((1, seq_len, tc), lambda b, j: (b, 0, j)),
        out_shape=jax.ShapeDtypeStruct((batch, seq_len, d_conv), BF16),
        compiler_params=_params(("parallel", "parallel"), 40),
        name="short_conv",
    )(proj3, proj3, proj3, conv_w)


def _merge_kernel(x_ref, ya_ref, yb_ref, yc_ref, ga_ref, gb_ref, gc_ref, wb_ref, wo_ref, o_ref):
    r_a = ya_ref.shape[1]
    r_b = r_a + yb_ref.shape[1]
    pa = jnp.dot(ya_ref[...], wb_ref[0:r_a, :], preferred_element_type=F32)
    merged = ga_ref[...].astype(F32) * pa
    pb = jnp.dot(yb_ref[...], wb_ref[r_a:r_b, :], preferred_element_type=F32)
    merged += gb_ref[...].astype(F32) * pb
    pc = jnp.dot(yc_ref[...], wb_ref[r_b:, :], preferred_element_type=F32)
    merged += gc_ref[...].astype(F32) * pc
    o_ref[...] = x_ref[...] + jnp.dot(merged.astype(BF16), wo_ref[...], preferred_element_type=F32)


def _merge(x, ya, yb, yc, proj, wb, wo, *, tm):
    n, d = x.shape
    row = lambda i: (i, 0)
    const = lambda i: (0, 0)
    return pl.pallas_call(
        _merge_kernel,
        grid=(n // tm,),
        in_specs=[
            pl.BlockSpec((tm, d), row),
            pl.BlockSpec((tm, ya.shape[1]), row),
            pl.BlockSpec((tm, yb.shape[1]), row),
            pl.BlockSpec((tm, yc.shape[1]), row),
            pl.BlockSpec((tm, d), lambda i: (i, 0)),
            pl.BlockSpec((tm, d), lambda i: (i, 1)),
            pl.BlockSpec((tm, d), lambda i: (i, 2)),
            pl.BlockSpec(wb.shape, const, pipeline_mode=pl.Buffered(1)),
            pl.BlockSpec(wo.shape, const, pipeline_mode=pl.Buffered(1)),
        ],
        out_specs=pl.BlockSpec((tm, d), row),
        out_shape=jax.ShapeDtypeStruct((n, d), F32),
        compiler_params=_params(("parallel",), 56),
        name="merge",
    )(x, ya, yb, yc, proj, proj, proj, wb, wo)


def _rope_tables(seq_len, head_dim):
    pos = jnp.arange(seq_len, dtype=F32)
    inv = ROPE_THETA ** (-jnp.arange(0, head_dim, 2, dtype=F32) / head_dim)
    ang = pos[:, None] * inv[None, :]
    cos, sin = jnp.cos(ang), jnp.sin(ang)
    reps = LANES // head_dim
    return (jnp.tile(jnp.concatenate([cos, cos], axis=1), (1, reps)),
            jnp.tile(jnp.concatenate([-sin, sin], axis=1), (1, reps)))


def _block_diag(blocks):
    g, r, c = blocks.shape
    wide = jnp.transpose(blocks, (1, 0, 2)).reshape(r, g * c)
    row_g = jnp.arange(g * r)[:, None] // r
    col_g = jnp.arange(g * c)[None, :] // c
    return jnp.where(row_g == col_g, jnp.tile(wide, (g, 1)), 0.0)


def _s5_tables(lam_re, lam_im, log_dt, b_re, b_im, c_re, c_im, n_steps):
    bmat = jnp.concatenate([_block_diag(jnp.swapaxes(b_re, 1, 2)), _block_diag(jnp.swapaxes(b_im, 1, 2))], axis=1)
    cmats, tabs = [], []
    for d in range(2):
        lr, li = lam_re[d], lam_im[d]
        dt = jnp.exp(log_dt[d])[:, None]
        mag = jnp.exp(dt * lr)
        ar, ai = mag * jnp.cos(dt * li), mag * jnp.sin(dt * li)
        denom = lr * lr + li * li
        nr = ar - 1.0
        kr = (nr * lr + ai * li) / denom
        ki = (ai * lr - nr * li) / denom
        er = c_re * kr[:, None, :] - c_im * ki[:, None, :]
        ei = c_re * ki[:, None, :] + c_im * kr[:, None, :]
        cmats.append(jnp.concatenate([_block_diag(jnp.swapaxes(er, 1, 2)), _block_diag(jnp.swapaxes(-ei, 1, 2))],
                                     axis=0))
        ar, ai = ar.reshape(-1), ai.reshape(-1)
        pr, pi, sr, si, e = jnp.ones_like(ar), jnp.zeros_like(ai), ar, ai, n_steps
        while e:
            if e & 1:
                pr, pi = pr * sr - pi * si, pr * si + pi * sr
            sr, si = sr * sr - si * si, 2.0 * sr * si
            e >>= 1
        tabs.append(jnp.stack([jnp.broadcast_to(v[None, :], (SUBLANES, v.shape[0])) for v in (ar, ai, pr, pi)]))
    return bmat.astype(BF16), jnp.stack(tabs), jnp.stack(cmats).astype(BF16)


def kernel(x, norm_w, ffn_w13, ffn_w2, w_in, s5_lambda_re, s5_lambda_im, s5_log_dt, s5_b_re, s5_b_im, s5_c_re, s5_c_im, s5_d, s5_w_glu, s5_b_glu, diff_lambda, diff_subln, conv_w, w_branch, w_gate, b_gate, w_out, final_norm):
    batch, seq_len, d_model = x.shape
    depth = norm_w.shape[0]
    d_ff = ffn_w2.shape[2]
    d_ssm = s5_d.shape[-1]
    d_conv = conv_w.shape[-1]
    d_attn = (w_in.shape[-1] - d_ssm - 3 * d_conv) // 3
    head_dim = diff_lambda.shape[-1]
    n_heads = d_attn // (2 * head_dim)
    n = batch * seq_len
    assert 2 * head_dim == LANES and d_ssm == d_conv and d_attn == 2 * d_ssm

    tm_ffn = _tile(n, 1024)
    tf = min(256, d_ff)
    tm_proj = _tile(seq_len, 1024)
    tn_proj = d_attn
    tm_merge = _tile(n, 512)
    tq = _tile(seq_len, 512)
    t_chunk = _tile(seq_len, 512)

    cos, sin = _rope_tables(seq_len, head_dim)
    xf = x.reshape(n, d_model)
    gate_cols = N_BRANCH * d_model
    u_col = gate_cols // d_ssm
    q_col = (gate_cols + d_ssm) // LANES
    k_col = q_col + d_attn // LANES
    v_col = k_col + d_attn // LANES
    bg_col = (gate_cols + d_ssm + 3 * d_attn) // d_conv
    b_gate3 = b_gate[:, None, :]

    for l in range(depth):
        lambda_init = 0.8 - 0.6 * math.exp(-0.3 * l)
        bmat, tabs, cmat = _s5_tables(s5_lambda_re[l], s5_lambda_im[l], s5_log_dt[l],
                                      s5_b_re[l], s5_b_im[l], s5_c_re[l], s5_c_im[l], t_chunk // SUBLANES)

        xf, h_mix = _ffn(xf, norm_w[l, 0][None, :], ffn_w13, ffn_w2, norm_w[l, 1][None, :], layer=l, idx=0,
                         post="next", tm=tm_ffn, tf=tf)

        proj = _proj(h_mix, w_gate, b_gate3, w_in, cos, sin, layer=l, seq_len=seq_len, u_cols=d_ssm,
                     attn_cols=d_attn, q_scale=head_dim ** -0.5 * math.log2(math.e), tm=tm_proj, tn=tn_proj)
        proj3 = proj.reshape(batch, seq_len, proj.shape[1])

        ys = _s5_scan(proj, bmat, tabs, cmat, batch=batch, seq_len=seq_len, u_col=u_col, d_ssm=d_ssm,
                      tc=t_chunk)
        y_a = _s5_glu(ys, proj, s5_d[l][None, :], s5_w_glu[l].astype(BF16), s5_b_glu[l][None, :],
                      u_col=u_col, tm=_tile(seq_len, 1024), tc=t_chunk)
        y_b = _attention(proj3, diff_lambda[l], diff_subln[l][None, :], q_col=q_col, k_col=k_col, v_col=v_col,
                         n_heads=n_heads, head_dim=head_dim, lambda_init=lambda_init, tq=tq)
        y_c = _short_conv(proj3, conv_w[l], bg_col=bg_col, d_conv=d_conv)

        xf = _merge(xf, y_a, y_b.reshape(n, d_attn), y_c.reshape(n, d_conv), proj,
                    w_branch[l].astype(BF16), w_out[l].astype(BF16), tm=tm_merge)

        xf = _ffn(xf, norm_w[l, 2][None, :], ffn_w13, ffn_w2, final_norm[None, :], layer=l, idx=1,
                  post="final" if l == depth - 1 else "none", tm=tm_ffn, tf=tf)
    return xf.reshape(batch, seq_len, d_model)
```

```python
import functools
import math

import jax
import jax.numpy as jnp
from jax import lax
from jax.experimental import pallas as pl
from jax.experimental.pallas import tpu as pltpu

NORM_EPS = 1e-6
ROPE_THETA = 10000.0
N_BRANCH = 3
LANES = 128
SUBLANES = 8
MIB = 1024 * 1024
F32 = jnp.float32
BF16 = jnp.bfloat16


def _rms(x, g):
    return x * lax.rsqrt(jnp.mean(x * x, axis=-1, keepdims=True) + NORM_EPS) * g


def _round_up(a, b):
    return (a + b - 1) // b * b


def _tile(n, want):
    t = min(n, want)
    while n % t:
        t -= 1
    return t


def _params(sem, vmem_mib):
    return pltpu.CompilerParams(dimension_semantics=sem, vmem_limit_bytes=vmem_mib * MIB)


def _ffn_kernel(x_ref, g_ref, wa_ref, *rest, post, d_ff, n_units):
    wb_refs = rest[:n_units]
    w2_refs = rest[n_units:2 * n_units]
    if post == "next":
        fg_ref, o_ref, hn_ref = rest[2 * n_units:]
        h_ref = hn_ref
    else:
        fg_ref, o_ref, h_ref = rest[2 * n_units:]
    j = pl.program_id(1)
    tf = wa_ref.shape[1]

    @pl.when(j == 0)
    def _():
        h_ref[...] = _rms(x_ref[...], g_ref[...]).astype(BF16)
        o_ref[...] = jnp.zeros_like(o_ref)

    h = h_ref[...]
    a = jnp.dot(h, wa_ref[...].astype(BF16), preferred_element_type=F32)
    wb = jnp.concatenate([r[...].astype(BF16) for r in wb_refs], axis=1)
    b = jnp.dot(h, wb, preferred_element_type=F32)
    col = j * tf + lax.broadcasted_iota(jnp.int32, a.shape, 1)
    act = jnp.where(col < d_ff, a * jax.nn.sigmoid(a) * b, 0.0).astype(BF16)
    w2 = jnp.concatenate([r[...].astype(BF16) for r in w2_refs], axis=0)
    o_ref[...] += jnp.dot(act, w2, preferred_element_type=F32)

    @pl.when(j == pl.num_programs(1) - 1)
    def _():
        y = x_ref[...] + 0.5 * o_ref[...]
        if post == "final":
            y = _rms(y, fg_ref[...])
        if post == "next":
            hn_ref[...] = _rms(y, fg_ref[...]).astype(BF16)
        o_ref[...] = y


def _ffn(x, g, w13, w2, fg, *, layer, idx, post, tm, tf):
    n, d = x.shape
    row_blk = pl.BlockSpec((tm, d), lambda i, j: (i, 0))
    out_specs, out_shape = row_blk, jax.ShapeDtypeStruct((n, d), F32)
    if post == "next":
        out_specs, out_shape = [row_blk, row_blk], [out_shape, jax.ShapeDtypeStruct((n, d), BF16)]
    d_ff = w2.shape[2]
    assert d_ff % LANES == 0 and tf % LANES == 0 and 2 * d_ff >= _round_up(d_ff, tf)
    n_units = tf // LANES
    ff_units = d_ff // LANES

    def wb_spec(r):
        return pl.BlockSpec((None, None, d, LANES), lambda i, j: (
            layer, idx, 0, ff_units + jnp.minimum(j * n_units + r, ff_units - 1)))

    def w2_spec(r):
        return pl.BlockSpec((None, None, LANES, d), lambda i, j: (
            layer, idx, jnp.minimum(j * n_units + r, ff_units - 1), 0))

    return pl.pallas_call(
        functools.partial(_ffn_kernel, post=post, d_ff=d_ff, n_units=n_units),
        grid=(n // tm, pl.cdiv(d_ff, tf)),
        in_specs=[
            row_blk,
            pl.BlockSpec((1, d), lambda i, j: (0, 0)),
            pl.BlockSpec((None, None, d, tf), lambda i, j: (layer, idx, 0, j)),
            *[wb_spec(r) for r in range(n_units)],
            *[w2_spec(r) for r in range(n_units)],
            pl.BlockSpec((1, d), lambda i, j: (0, 0)),
        ],
        out_specs=out_specs,
        out_shape=out_shape,
        scratch_shapes=[] if post == "next" else [pltpu.VMEM((tm, d), BF16)],
        compiler_params=_params(("parallel", "arbitrary"), 62),
        name="ffn",
    )(x, g, w13, *([w13] * n_units), *([w2] * n_units), fg)


def _rope(acc, cos, sin, scale):
    lane = lax.broadcasted_iota(jnp.int32, cos.shape, 1)
    first_half = (lane & (LANES // 4)) == 0
    out = []
    for c in range(acc.shape[1] // LANES):
        blk = acc[:, c * LANES:(c + 1) * LANES]
        partner = jnp.where(first_half,
                            pltpu.roll(blk, LANES - LANES // 4, axis=1),
                            pltpu.roll(blk, LANES // 4, axis=1))
        out.append((blk * cos + partner * sin) * scale)
    return jnp.concatenate(out, axis=1)


def _proj_kernel(h_ref, wg_ref, wi_ref, b_ref, cos_ref, sin_ref, o_ref, *,
                 n_gate, n_in, u_cols, attn_cols, q_scale, chunk):
    j = pl.program_id(0)
    tn = o_ref.shape[1]

    def run(w_ref, epilogue):
        for c in range(tn // chunk):
            cols = slice(c * chunk, (c + 1) * chunk)
            acc = jnp.dot(h_ref[...], w_ref[:, cols].astype(BF16), preferred_element_type=F32)
            o_ref[:, cols] = epilogue(acc, c).astype(o_ref.dtype)

    @pl.when(j < n_gate)
    def _():
        run(wg_ref, lambda acc, c: jax.nn.sigmoid(acc + b_ref[:, c * chunk:(c + 1) * chunk]))

    def in_epilogue(blk):
        def epilogue(acc, c):
            col = blk * tn + c * chunk
            if u_cols <= col < u_cols + attn_cols:
                return _rope(acc, cos_ref[...], sin_ref[...], q_scale)
            if u_cols + attn_cols <= col < u_cols + 2 * attn_cols:
                return _rope(acc, cos_ref[...], sin_ref[...], 1.0)
            return acc
        return epilogue

    for blk in range(n_in):
        @pl.when(j == n_gate + blk)
        def _(blk=blk):
            run(wi_ref, in_epilogue(blk))


def _proj(h, w_gate, b_gate, w_in, cos, sin, *, layer, seq_len, u_cols, attn_cols, q_scale, tm, tn):
    n, d = h.shape
    n_gate = w_gate.shape[2] // tn
    n_in = w_in.shape[2] // tn
    chunk = min(u_cols, 2 * LANES)
    assert u_cols % chunk == 0 and attn_cols % chunk == 0 and tn % chunk == 0
    pos_blocks = seq_len // tm
    gate_blk = lambda j, i: (layer, 0, jnp.minimum(j, n_gate - 1))
    return pl.pallas_call(
        functools.partial(_proj_kernel, n_gate=n_gate, n_in=n_in, u_cols=u_cols, attn_cols=attn_cols,
                          q_scale=q_scale, chunk=chunk),
        grid=(n_gate + n_in, n // tm),
        in_specs=[
            pl.BlockSpec((tm, d), lambda j, i: (i, 0)),
            pl.BlockSpec((None, d, tn), gate_blk),
            pl.BlockSpec((None, d, tn), lambda j, i: (layer, 0, jnp.maximum(j - n_gate, 0))),
            pl.BlockSpec((None, 1, tn), gate_blk),
            pl.BlockSpec((tm, LANES), lambda j, i: (i % pos_blocks, 0)),
            pl.BlockSpec((tm, LANES), lambda j, i: (i % pos_blocks, 0)),
        ],
        out_specs=pl.BlockSpec((tm, tn), lambda j, i: (i, j)),
        out_shape=jax.ShapeDtypeStruct((n, (n_gate + n_in) * tn), BF16),
        compiler_params=_params(("arbitrary", "arbitrary"), 56),
        name="proj",
    )(h, w_gate, w_in, b_gate, cos, sin)


def _scan_chunk(bu_ref, tab_ref, carry_ref, *, reverse, lane_group):
    t_len, s2 = bu_ref.shape
    s_dim = s2 // 2
    n_steps = t_len // SUBLANES
    first, last = (SUBLANES - 1, 0) if reverse else (0, SUBLANES - 1)
    toward_later = SUBLANES - 1 if reverse else 1
    for lg in range(s_dim // lane_group):
        re_cols = pl.ds(lg * lane_group, lane_group)
        im_cols = pl.ds(s_dim + lg * lane_group, lane_group)
        a_re, a_im = tab_ref[0, :, re_cols], tab_ref[1, :, re_cols]
        an_re, an_im = tab_ref[2, :, re_cols], tab_ref[3, :, re_cols]

        def rows_of(it):
            step = (n_steps - 1 - it) if reverse else it
            return pl.ds(pl.multiple_of(step * SUBLANES, SUBLANES), SUBLANES)

        def local_scan(it, state):
            s_re, s_im = state
            rows = rows_of(it)
            s_re, s_im = (a_re * s_re - a_im * s_im + bu_ref[rows, re_cols],
                          a_re * s_im + a_im * s_re + bu_ref[rows, im_cols])
            bu_ref[rows, re_cols] = s_re
            bu_ref[rows, im_cols] = s_im
            return s_re, s_im

        zero = jnp.zeros((SUBLANES, lane_group), F32)
        e_re, e_im = lax.fori_loop(0, n_steps, local_scan, (zero, zero), unroll=4)

        row = lax.broadcasted_iota(jnp.int32, (SUBLANES, lane_group), 0)
        c_re = jnp.where(row == first, carry_ref[0, :, re_cols], 0.0)
        c_im = jnp.where(row == first, carry_ref[1, :, re_cols], 0.0)

        def segment_end(c_re, c_im):
            return e_re + an_re * c_re - an_im * c_im, e_im + an_re * c_im + an_im * c_re

        order = range(SUBLANES - 2, -1, -1) if reverse else range(1, SUBLANES)
        for k in order:
            t_re, t_im = segment_end(c_re, c_im)
            c_re = jnp.where(row == k, pltpu.roll(t_re, toward_later, axis=0), c_re)
            c_im = jnp.where(row == k, pltpu.roll(t_im, toward_later, axis=0), c_im)
        t_re, t_im = segment_end(c_re, c_im)
        carry_ref[0, :, re_cols] = jnp.broadcast_to(t_re[last:last + 1], t_re.shape)
        carry_ref[1, :, re_cols] = jnp.broadcast_to(t_im[last:last + 1], t_im.shape)

        def add_incoming(it, f):
            f_re, f_im = f
            rows = rows_of(it)
            f_re, f_im = a_re * f_re - a_im * f_im, a_re * f_im + a_im * f_re
            bu_ref[rows, re_cols] += f_re
            bu_ref[rows, im_cols] += f_im
            return f_re, f_im

        lax.fori_loop(0, n_steps, add_incoming, (c_re, c_im), unroll=4)


def _s5_kernel(u_ref, perm_ref, bmat_ref, tab_ref, cmat_ref, y_ref, bu_ref, carry_ref, *, lane_group):
    d = pl.program_id(1)
    c = pl.program_id(2)

    @pl.when(c == 0)
    def _():
        carry_ref[...] = jnp.zeros_like(carry_ref)

    u_perm = jnp.dot(perm_ref[...], u_ref[...], preferred_element_type=F32).astype(BF16)

    d_ssm, s2 = bmat_ref.shape
    s_dim = s2 // 2
    halves = [(slice(h * d_ssm // 2, (h + 1) * d_ssm // 2), slice(h * s_dim // 2, (h + 1) * s_dim // 2),
               slice(s_dim + h * s_dim // 2, s_dim + (h + 1) * s_dim // 2)) for h in range(2)]
    for ch, st_re, st_im in halves:
        u_half = u_perm[:, ch]
        bu_ref[:, st_re] = jnp.dot(u_half, bmat_ref[ch, st_re], preferred_element_type=F32)
        bu_ref[:, st_im] = jnp.dot(u_half, bmat_ref[ch, st_im], preferred_element_type=F32)

    @pl.when(d == 0)
    def _():
        _scan_chunk(bu_ref, tab_ref.at[0], carry_ref, reverse=False, lane_group=lane_group)

    @pl.when(d == 1)
    def _():
        _scan_chunk(bu_ref, tab_ref.at[0], carry_ref, reverse=True, lane_group=lane_group)

    for ch, st_re, st_im in halves:
        y_ref[0, :, ch] = (
            jnp.dot(bu_ref[:, st_re].astype(BF16), cmat_ref[0, st_re, ch], preferred_element_type=F32)
            + jnp.dot(bu_ref[:, st_im].astype(BF16), cmat_ref[0, st_im, ch], preferred_element_type=F32))


def _segment_interleave(tc):
    dst = jnp.arange(tc)
    src = (dst % SUBLANES) * (tc // SUBLANES) + dst // SUBLANES
    return (src[:, None] == jnp.arange(tc)[None, :]).astype(BF16)


def _s5_scan(proj, bmat, tabs, cmat, *, batch, seq_len, u_col, d_ssm, tc):
    n = proj.shape[0]
    n_chunks = seq_len // tc
    s2 = bmat.shape[1]
    s_dim = s2 // 2
    lane_group = min(s_dim, 4 * LANES)

    def chunk_row(b, d, c):
        return b * n_chunks + jnp.where(d == 0, c, n_chunks - 1 - c)

    return pl.pallas_call(
        functools.partial(_s5_kernel, lane_group=lane_group),
        grid=(batch, 2, n_chunks),
        in_specs=[
            pl.BlockSpec((tc, d_ssm), lambda b, d, c: (chunk_row(b, d, c), u_col)),
            pl.BlockSpec((tc, tc), lambda b, d, c: (0, 0)),
            pl.BlockSpec((d_ssm, s2), lambda b, d, c: (0, 0)),
            pl.BlockSpec((1, 4, SUBLANES, s_dim), lambda b, d, c: (d, 0, 0, 0)),
            pl.BlockSpec((1, s2, d_ssm), lambda b, d, c: (d, 0, 0)),
        ],
        out_specs=pl.BlockSpec((1, tc, d_ssm), lambda b, d, c: (d, chunk_row(b, d, c), 0)),
        out_shape=jax.ShapeDtypeStruct((2, n, d_ssm), F32),
        scratch_shapes=[pltpu.VMEM((tc, s2), F32), pltpu.VMEM((2, SUBLANES, s_dim), F32)],
        compiler_params=_params(("parallel", "arbitrary", "arbitrary"), 52),
        name="s5_scan",
    )(proj, _segment_interleave(tc), bmat, tabs, cmat)


def _glu_kernel(ys_ref, u_ref, dskip_ref, w_ref, b_ref, o_ref, y_ref, *, tc):
    n_lane_tiles, tm, _ = y_ref.shape
    y_sum = ys_ref[0] + ys_ref[1]
    for j in range(n_lane_tiles):
        y_ref[j] = y_sum[:, j * LANES:(j + 1) * LANES]
    n_steps = tc // SUBLANES
    pieces = [jnp.concatenate([y_ref[j, pl.ds(base + g, n_steps, stride=SUBLANES), :]
                               for j in range(n_lane_tiles)], axis=1)
              for base in range(0, tm, tc) for g in range(SUBLANES)]
    y = jnp.concatenate(pieces, axis=0) + dskip_ref[...] * u_ref[...].astype(F32)
    y = jax.nn.gelu(y)
    z = jnp.dot(y.astype(BF16), w_ref[...], preferred_element_type=F32) + b_ref[...]
    o_ref[...] = (y * jax.nn.sigmoid(z)).astype(o_ref.dtype)


def _s5_glu(ys, proj, dskip, w, b, *, u_col, tm, tc):
    _, n, d_ssm = ys.shape
    assert tm % tc == 0
    return pl.pallas_call(
        functools.partial(_glu_kernel, tc=tc),
        scratch_shapes=[pltpu.VMEM((d_ssm // min(d_ssm, LANES), tm, min(d_ssm, LANES)), F32)],
        grid=(n // tm,),
        in_specs=[
            pl.BlockSpec((2, tm, d_ssm), lambda i: (0, i, 0)),
            pl.BlockSpec((tm, d_ssm), lambda i: (i, u_col)),
            pl.BlockSpec((1, d_ssm), lambda i: (0, 0)),
            pl.BlockSpec((d_ssm, d_ssm), lambda i: (0, 0)),
            pl.BlockSpec((1, d_ssm), lambda i: (0, 0)),
        ],
        out_specs=pl.BlockSpec((tm, d_ssm), lambda i: (i, 0)),
        out_shape=jax.ShapeDtypeStruct((n, d_ssm), BF16),
        compiler_params=_params(("parallel",), 40),
        name="s5_glu",
    )(ys, proj, dskip, w, b)


def _attn_kernel(q_ref, k_ref, v_ref, lam_ref, sg_ref, o_ref, *, lambda_init, head_dim, kc):
    q = q_ref[0]
    tq = q.shape[0]
    lane = lax.broadcasted_iota(jnp.int32, q.shape, 1)
    zero = jnp.zeros_like(q)
    qq = jnp.concatenate([jnp.where(lane < head_dim, q, zero), jnp.where(lane >= head_dim, q, zero)], axis=0)
    n_chunks = k_ref.shape[1] // kc
    sub = 2 * SUBLANES

    def scores(c):
        return lax.dot_general(k_ref[0, pl.ds(c * kc, kc), :], qq, (((1,), (1,)), ((), ())),
                               preferred_element_type=F32)

    def exp_stage(s, m_new):
        pieces, part = [], None
        for r in range(0, kc, sub):
            e_r = jnp.exp2(s[r:r + sub] - m_new)
            pieces.append(e_r.astype(BF16))
            for g in range(sub // SUBLANES):
                tile = e_r[g * SUBLANES:(g + 1) * SUBLANES]
                part = tile if part is None else part + tile
        return jnp.concatenate(pieces, axis=0), jnp.sum(part, axis=0, keepdims=True)

    def value_stage(acc, item):
        c, e_b, alpha = item
        pv = lax.dot_general(v_ref[0, pl.ds(c * kc, kc), :], e_b, (((0,), (0,)), ((), ())),
                             preferred_element_type=F32)
        return pv if acc is None else acc * alpha + pv

    score_lead = 2
    m = l = acc = None
    s_q = [scores(c) for c in range(min(score_lead, n_chunks))]
    pending = None
    for c in range(n_chunks):
        if c + score_lead < n_chunks:
            s_q.append(scores(c + score_lead))
        s = s_q[c]
        m_c = jnp.max(s, axis=0, keepdims=True)
        m_new = m_c if m is None else jnp.maximum(m, m_c)
        e_b, l_c = exp_stage(s, m_new)
        alpha = None if m is None else jnp.exp2(m - m_new)
        l = l_c if m is None else l * alpha + l_c
        if pending is not None:
            acc = value_stage(acc, pending)
        pending = (c, e_b, alpha)
        m = m_new
    acc = value_stage(acc, pending)
    lv = lam_ref[...]
    lam = (jnp.exp(jnp.sum(lv[0:1] * lv[1:2], axis=-1, keepdims=True))
           - jnp.exp(jnp.sum(lv[2:3] * lv[3:4], axis=-1, keepdims=True)) + lambda_init)
    o_t = acc[:, 0:tq] * (1.0 / l[:, 0:tq]) - acc[:, tq:2 * tq] * (lam / l[:, tq:2 * tq])
    o_ref[0] = (_rms(o_t.T, sg_ref[...]) * (1.0 - lambda_init)).astype(o_ref.dtype)


def _attention(proj3, lam_vec, subln, *, q_col, k_col, v_col, n_heads, head_dim, lambda_init, tq):
    batch, seq_len, _ = proj3.shape
    vd = 2 * head_dim
    return pl.pallas_call(
        functools.partial(_attn_kernel, lambda_init=lambda_init, head_dim=head_dim, kc=_tile(seq_len, 512)),
        grid=(batch, n_heads, seq_len // tq),
        in_specs=[
            pl.BlockSpec((1, tq, vd), lambda b, h, i: (b, i, q_col + h)),
            pl.BlockSpec((1, seq_len, vd), lambda b, h, i: (b, 0, k_col + h)),
            pl.BlockSpec((1, seq_len, vd), lambda b, h, i: (b, 0, v_col + h)),
            pl.BlockSpec((4, head_dim), lambda b, h, i: (0, 0)),
            pl.BlockSpec((1, vd), lambda b, h, i: (0, 0)),
        ],
        out_specs=pl.BlockSpec((1, tq, vd), lambda b, h, i: (b, i, h)),
        out_shape=jax.ShapeDtypeStruct((batch, seq_len, n_heads * vd), BF16),
        compiler_params=_params(("parallel", "parallel", "arbitrary"), 52),
        name="diff_attn",
    )(proj3, proj3, proj3, lam_vec, subln)


def _conv_kernel(bg_ref, cg_ref, xv_ref, w_ref, o_ref):
    z = cg_ref[0].astype(F32) * xv_ref[0].astype(F32)
    seq_len = z.shape[0]
    row = lax.broadcasted_iota(jnp.int32, z.shape, 0)
    z_prev = jnp.where(row == 0, 0.0, pltpu.roll(z, 1, axis=0))
    z_next = jnp.where(row == seq_len - 1, 0.0, pltpu.roll(z, seq_len - 1, axis=0))
    w = w_ref[...]
    zc = w[0:1] * z_prev + w[1:2] * z + w[2:3] * z_next
    o_ref[0] = (bg_ref[0].astype(F32) * zc).astype(o_ref.dtype)


def _short_conv(proj3, conv_w, *, bg_col, d_conv):
    batch, seq_len, _ = proj3.shape
    tc = min(d_conv, LANES)
    per = d_conv // tc
    return pl.pallas_call(
        _conv_kernel,
        grid=(batch, per),
        in_specs=[
            pl.BlockSpec((1, seq_len, tc), lambda b, j: (b, 0, bg_col * per + j)),
            pl.BlockSpec((1, seq_len, tc), lambda b, j: (b, 0, (bg_col + 1) * per + j)),
            pl.BlockSpec((1, seq_len, tc), lambda b, j: (b, 0, (bg_col + 2) * per + j)),
            pl.BlockSpec((conv_w.shape[0], tc), lambda b, j: (0, j)),
        ],
        out_specs=pl.BlockSpec((1, seq_len, tc), lambda b, j: (b, 0, j)),
        out_shape=jax.ShapeDtypeStruct((batch, seq_len, d_conv), BF16),
        compiler_params=_params(("parallel", "parallel"), 40),
        name="short_conv",
    )(proj3, proj3, proj3, conv_w)


def _merge_kernel(x_ref, ya_ref, yb_ref, yc_ref, ga_ref, gb_ref, gc_ref, wb_ref, wo_ref, o_ref):
    r_a = ya_ref.shape[1]
    r_b = r_a + yb_ref.shape[1]
    pa = jnp.dot(ya_ref[...], wb_ref[0:r_a, :], preferred_element_type=F32)
    merged = ga_ref[...].astype(F32) * pa
    pb = jnp.dot(yb_ref[...], wb_ref[r_a:r_b, :], preferred_element_type=F32)
    merged += gb_ref[...].astype(F32) * pb
    pc = jnp.dot(yc_ref[...], wb_ref[r_b:, :], preferred_element_type=F32)
    merged += gc_ref[...].astype(F32) * pc
    o_ref[...] = x_ref[...] + jnp.dot(merged.astype(BF16), wo_ref[...], preferred_element_type=F32)


def _merge(x, ya, yb, yc, proj, wb, wo, *, tm):
    n, d = x.shape
    row = lambda i: (i, 0)
    const = lambda i: (0, 0)
    return pl.pallas_call(
        _merge_kernel,
        grid=(n // tm,),
        in_specs=[
            pl.BlockSpec((tm, d), row),
            pl.BlockSpec((tm, ya.shape[1]), row),
            pl.BlockSpec((tm, yb.shape[1]), row),
            pl.BlockSpec((tm, yc.shape[1]), row),
            pl.BlockSpec((tm, d), lambda i: (i, 0)),
            pl.BlockSpec((tm, d), lambda i: (i, 1)),
            pl.BlockSpec((tm, d), lambda i: (i, 2)),
            pl.BlockSpec(wb.shape, const, pipeline_mode=pl.Buffered(1)),
            pl.BlockSpec(wo.shape, const, pipeline_mode=pl.Buffered(1)),
        ],
        out_specs=pl.BlockSpec((tm, d), row),
        out_shape=jax.ShapeDtypeStruct((n, d), F32),
        compiler_params=_params(("parallel",), 56),
        name="merge",
    )(x, ya, yb, yc, proj, proj, proj, wb, wo)


def _rope_tables(seq_len, head_dim):
    pos = jnp.arange(seq_len, dtype=F32)
    inv = ROPE_THETA ** (-jnp.arange(0, head_dim, 2, dtype=F32) / head_dim)
    ang = pos[:, None] * inv[None, :]
    cos, sin = jnp.cos(ang), jnp.sin(ang)
    reps = LANES // head_dim
    return (jnp.tile(jnp.concatenate([cos, cos], axis=1), (1, reps)),
            jnp.tile(jnp.concatenate([-sin, sin], axis=1), (1, reps)))


def _block_diag(blocks):
    g, r, c = blocks.shape
    wide = jnp.transpose(blocks, (1, 0, 2)).reshape(r, g * c)
    row_g = jnp.arange(g * r)[:, None] // r
    col_g = jnp.arange(g * c)[None, :] // c
    return jnp.where(row_g == col_g, jnp.tile(wide, (g, 1)), 0.0)


def _s5_tables(lam_re, lam_im, log_dt, b_re, b_im, c_re, c_im, n_steps):
    bmat = jnp.concatenate([_block_diag(jnp.swapaxes(b_re, 1, 2)), _block_diag(jnp.swapaxes(b_im, 1, 2))], axis=1)
    cmats, tabs = [], []
    for d in range(2):
        lr, li = lam_re[d], lam_im[d]
        dt = jnp.exp(log_dt[d])[:, None]
        mag = jnp.exp(dt * lr)
        ar, ai = mag * jnp.cos(dt * li), mag * jnp.sin(dt * li)
        denom = lr * lr + li * li
        nr = ar - 1.0
        kr = (nr * lr + ai * li) / denom
        ki = (ai * lr - nr * li) / denom
        er = c_re * kr[:, None, :] - c_im * ki[:, None, :]
        ei = c_re * ki[:, None, :] + c_im * kr[:, None, :]
        cmats.append(jnp.concatenate([_block_diag(jnp.swapaxes(er, 1, 2)), _block_diag(jnp.swapaxes(-ei, 1, 2))],
                                     axis=0))
        ar, ai = ar.reshape(-1), ai.reshape(-1)
        pr, pi, sr, si, e = jnp.ones_like(ar), jnp.zeros_like(ai), ar, ai, n_steps
        while e:
            if e & 1:
                pr, pi = pr * sr - pi * si, pr * si + pi * sr
            sr, si = sr * sr - si * si, 2.0 * sr * si
            e >>= 1
        tabs.append(jnp.stack([jnp.broadcast_to(v[None, :], (SUBLANES, v.shape[0])) for v in (ar, ai, pr, pi)]))
    return bmat.astype(BF16), jnp.stack(tabs), jnp.stack(cmats).astype(BF16)


def kernel(x, norm_w, ffn_w13, ffn_w2, w_in, s5_lambda_re, s5_lambda_im, s5_log_dt, s5_b_re, s5_b_im, s5_c_re, s5_c_im, s5_d, s5_w_glu, s5_b_glu, diff_lambda, diff_subln, conv_w, w_branch, w_gate, b_gate, w_out, final_norm):
    batch, seq_len, d_model = x.shape
    depth = norm_w.shape[0]
    d_ff = ffn_w2.shape[2]
    d_ssm = s5_d.shape[-1]
    d_conv = conv_w.shape[-1]
    d_attn = (w_in.shape[-1] - d_ssm - 3 * d_conv) // 3
    head_dim = diff_lambda.shape[-1]
    n_heads = d_attn // (2 * head_dim)
    n = batch * seq_len
    assert 2 * head_dim == LANES and d_ssm == d_conv and d_attn == 2 * d_ssm

    tm_ffn = _tile(n, 1024)
    tf = min(256, d_ff)
    tm_proj = _tile(seq_len, 1024)
    tn_proj = d_attn
    tm_merge = _tile(n, 512)
    tq = _tile(seq_len, 1024)
    t_chunk = _tile(seq_len, 512)

    cos, sin = _rope_tables(seq_len, head_dim)
    xf = x.reshape(n, d_model)
    gate_cols = N_BRANCH * d_model
    u_col = gate_cols // d_ssm
    q_col = (gate_cols + d_ssm) // LANES
    k_col = q_col + d_attn // LANES
    v_col = k_col + d_attn // LANES
    bg_col = (gate_cols + d_ssm + 3 * d_attn) // d_conv
    b_gate3 = b_gate[:, None, :]

    for l in range(depth):
        lambda_init = 0.8 - 0.6 * math.exp(-0.3 * l)
        bmat, tabs, cmat = _s5_tables(s5_lambda_re[l], s5_lambda_im[l], s5_log_dt[l],
                                      s5_b_re[l], s5_b_im[l], s5_c_re[l], s5_c_im[l], t_chunk // SUBLANES)

        xf, h_mix = _ffn(xf, norm_w[l, 0][None, :], ffn_w13, ffn_w2, norm_w[l, 1][None, :], layer=l, idx=0,
                         post="next", tm=tm_ffn, tf=tf)

        proj = _proj(h_mix, w_gate, b_gate3, w_in, cos, sin, layer=l, seq_len=seq_len, u_cols=d_ssm,
                     attn_cols=d_attn, q_scale=head_dim ** -0.5 * math.log2(math.e), tm=tm_proj, tn=tn_proj)
        proj3 = proj.reshape(batch, seq_len, proj.shape[1])

        ys = _s5_scan(proj, bmat, tabs, cmat, batch=batch, seq_len=seq_len, u_col=u_col, d_ssm=d_ssm,
                      tc=t_chunk)
        y_a = _s5_glu(ys, proj, s5_d[l][None, :], s5_w_glu[l].astype(BF16), s5_b_glu[l][None, :],
                      u_col=u_col, tm=_tile(seq_len, 1024), tc=t_chunk)
        y_b = _attention(proj3, diff_lambda[l], diff_subln[l][None, :], q_col=q_col, k_col=k_col, v_col=v_col,
                         n_heads=n_heads, head_dim=head_dim, lambda_init=lambda_init, tq=tq)
        y_c = _short_conv(proj3, conv_w[l], bg_col=bg_col, d_conv=d_conv)

        xf = _merge(xf, y_a, y_b.reshape(n, d_attn), y_c.reshape(n, d_conv), proj,
                    w_branch[l].astype(BF16), w_out[l].astype(BF16), tm=tm_merge)

        xf = _ffn(xf, norm_w[l, 2][None, :], ffn_w13, ffn_w2, final_norm[None, :], layer=l, idx=1,
                  post="final" if l == depth - 1 else "none", tm=tm_ffn, tf=tf)
    return xf.reshape(batch, seq_len, d_model)
```

```python
import functools
import math

import jax
import jax.numpy as jnp
from jax import lax
from jax.experimental import pallas as pl
from jax.experimental.pallas import tpu as pltpu

NORM_EPS = 1e-6
ROPE_THETA = 10000.0
N_BRANCH = 3
LANES = 128
SUBLANES = 8
MIB = 1024 * 1024
F32 = jnp.float32
BF16 = jnp.bfloat16


def _rms(x, g):
    return x * lax.rsqrt(jnp.mean(x * x, axis=-1, keepdims=True) + NORM_EPS) * g


def _round_up(a, b):
    return (a + b - 1) // b * b


def _tile(n, want):
    t = min(n, want)
    while n % t:
        t -= 1
    return t


VMEM_LIMIT_MIB = {"ffn": 62, "proj": 56, "s5_scan": 52, "s5_glu": 40, "diff_attn": 52, "short_conv": 40,
                  "merge": 56}


def _params(name, sem):
    return pltpu.CompilerParams(dimension_semantics=sem, vmem_limit_bytes=VMEM_LIMIT_MIB[name] * MIB)


def _ffn_kernel(x_ref, g_ref, wa_ref, *rest, post, d_ff, n_units):
    wb_refs = rest[:n_units]
    w2_refs = rest[n_units:2 * n_units]
    if post == "next":
        fg_ref, o_ref, hn_ref = rest[2 * n_units:]
        h_ref = hn_ref
    else:
        fg_ref, o_ref, h_ref = rest[2 * n_units:]
    j = pl.program_id(1)
    tf = wa_ref.shape[1]

    @pl.when(j == 0)
    def _():
        h_ref[...] = _rms(x_ref[...], g_ref[...]).astype(BF16)
        o_ref[...] = jnp.zeros_like(o_ref)

    h = h_ref[...]
    a = jnp.dot(h, wa_ref[...].astype(BF16), preferred_element_type=F32)
    wb = jnp.concatenate([r[...].astype(BF16) for r in wb_refs], axis=1)
    b = jnp.dot(h, wb, preferred_element_type=F32)
    col = j * tf + lax.broadcasted_iota(jnp.int32, a.shape, 1)
    act = jnp.where(col < d_ff, a * jax.nn.sigmoid(a) * b, 0.0).astype(BF16)
    w2 = jnp.concatenate([r[...].astype(BF16) for r in w2_refs], axis=0)
    o_ref[...] += jnp.dot(act, w2, preferred_element_type=F32)

    @pl.when(j == pl.num_programs(1) - 1)
    def _():
        y = x_ref[...] + 0.5 * o_ref[...]
        if post == "final":
            y = _rms(y, fg_ref[...])
        if post == "next":
            hn_ref[...] = _rms(y, fg_ref[...]).astype(BF16)
        o_ref[...] = y


def _ffn(x, g, w13, w2, fg, *, layer, idx, post, tm, tf):
    n, d = x.shape
    row_blk = pl.BlockSpec((tm, d), lambda i, j: (i, 0))
    out_specs, out_shape = row_blk, jax.ShapeDtypeStruct((n, d), F32)
    if post == "next":
        out_specs, out_shape = [row_blk, row_blk], [out_shape, jax.ShapeDtypeStruct((n, d), BF16)]
    d_ff = w2.shape[2]
    assert d_ff % LANES == 0 and tf % LANES == 0 and 2 * d_ff >= _round_up(d_ff, tf)
    n_units = tf // LANES
    ff_units = d_ff // LANES

    def wb_spec(r):
        return pl.BlockSpec((None, None, d, LANES), lambda i, j: (
            layer, idx, 0, ff_units + jnp.minimum(j * n_units + r, ff_units - 1)))

    def w2_spec(r):
        return pl.BlockSpec((None, None, LANES, d), lambda i, j: (
            layer, idx, jnp.minimum(j * n_units + r, ff_units - 1), 0))

    return pl.pallas_call(
        functools.partial(_ffn_kernel, post=post, d_ff=d_ff, n_units=n_units),
        grid=(n // tm, pl.cdiv(d_ff, tf)),
        in_specs=[
            row_blk,
            pl.BlockSpec((1, d), lambda i, j: (0, 0)),
            pl.BlockSpec((None, None, d, tf), lambda i, j: (layer, idx, 0, j)),
            *[wb_spec(r) for r in range(n_units)],
            *[w2_spec(r) for r in range(n_units)],
            pl.BlockSpec((1, d), lambda i, j: (0, 0)),
        ],
        out_specs=out_specs,
        out_shape=out_shape,
        scratch_shapes=[] if post == "next" else [pltpu.VMEM((tm, d), BF16)],
        compiler_params=_params("ffn", ("parallel", "arbitrary")),
        name="ffn",
    )(x, g, w13, *([w13] * n_units), *([w2] * n_units), fg)


def _rope(acc, cos, sin, scale):
    lane = lax.broadcasted_iota(jnp.int32, cos.shape, 1)
    first_half = (lane & (LANES // 4)) == 0
    out = []
    for c in range(acc.shape[1] // LANES):
        blk = acc[:, c * LANES:(c + 1) * LANES]
        partner = jnp.where(first_half,
                            pltpu.roll(blk, LANES - LANES // 4, axis=1),
                            pltpu.roll(blk, LANES // 4, axis=1))
        out.append((blk * cos + partner * sin) * scale)
    return jnp.concatenate(out, axis=1)


def _proj_kernel(h_ref, wg_ref, wi_ref, b_ref, cos_ref, sin_ref, o_ref, *,
                 n_gate, n_in, u_cols, attn_cols, q_scale, chunk):
    j = pl.program_id(0)
    tn = o_ref.shape[1]

    def run(w_ref, epilogue):
        for c in range(tn // chunk):
            cols = slice(c * chunk, (c + 1) * chunk)
            acc = jnp.dot(h_ref[...], w_ref[:, cols].astype(BF16), preferred_element_type=F32)
            o_ref[:, cols] = epilogue(acc, c).astype(o_ref.dtype)

    @pl.when(j < n_gate)
    def _():
        run(wg_ref, lambda acc, c: jax.nn.sigmoid(acc + b_ref[:, c * chunk:(c + 1) * chunk]))

    def in_epilogue(blk):
        def epilogue(acc, c):
            col = blk * tn + c * chunk
            if u_cols <= col < u_cols + attn_cols:
                return _rope(acc, cos_ref[...], sin_ref[...], q_scale)
            if u_cols + attn_cols <= col < u_cols + 2 * attn_cols:
                return _rope(acc, cos_ref[...], sin_ref[...], 1.0)
            return acc
        return epilogue

    for blk in range(n_in):
        @pl.when(j == n_gate + blk)
        def _(blk=blk):
            run(wi_ref, in_epilogue(blk))


def _proj(h, w_gate, b_gate, w_in, cos, sin, *, layer, seq_len, u_cols, attn_cols, q_scale, tm, tn):
    n, d = h.shape
    n_gate = w_gate.shape[2] // tn
    n_in = w_in.shape[2] // tn
    chunk = min(u_cols, 2 * LANES)
    assert u_cols % chunk == 0 and attn_cols % chunk == 0 and tn % chunk == 0
    pos_blocks = seq_len // tm
    gate_blk = lambda j, i: (layer, 0, jnp.minimum(j, n_gate - 1))
    return pl.pallas_call(
        functools.partial(_proj_kernel, n_gate=n_gate, n_in=n_in, u_cols=u_cols, attn_cols=attn_cols,
                          q_scale=q_scale, chunk=chunk),
        grid=(n_gate + n_in, n // tm),
        in_specs=[
            pl.BlockSpec((tm, d), lambda j, i: (i, 0)),
            pl.BlockSpec((None, d, tn), gate_blk),
            pl.BlockSpec((None, d, tn), lambda j, i: (layer, 0, jnp.maximum(j - n_gate, 0))),
            pl.BlockSpec((None, 1, tn), gate_blk),
            pl.BlockSpec((tm, LANES), lambda j, i: (i % pos_blocks, 0)),
            pl.BlockSpec((tm, LANES), lambda j, i: (i % pos_blocks, 0)),
        ],
        out_specs=pl.BlockSpec((tm, tn), lambda j, i: (i, j)),
        out_shape=jax.ShapeDtypeStruct((n, (n_gate + n_in) * tn), BF16),
        compiler_params=_params("proj", ("arbitrary", "arbitrary")),
        name="proj",
    )(h, w_gate, w_in, b_gate, cos, sin)


def _scan_chunk(bu_ref, tab_ref, carry_ref, *, reverse, lane_group):
    t_len, s2 = bu_ref.shape
    s_dim = s2 // 2
    n_steps = t_len // SUBLANES
    first, last = (SUBLANES - 1, 0) if reverse else (0, SUBLANES - 1)
    toward_later = SUBLANES - 1 if reverse else 1
    for lg in range(s_dim // lane_group):
        re_cols = pl.ds(lg * lane_group, lane_group)
        im_cols = pl.ds(s_dim + lg * lane_group, lane_group)
        a_re, a_im = tab_ref[0, :, re_cols], tab_ref[1, :, re_cols]
        an_re, an_im = tab_ref[2, :, re_cols], tab_ref[3, :, re_cols]

        def rows_of(it):
            step = (n_steps - 1 - it) if reverse else it
            return pl.ds(pl.multiple_of(step * SUBLANES, SUBLANES), SUBLANES)

        def local_scan(it, state):
            s_re, s_im = state
            rows = rows_of(it)
            s_re, s_im = (a_re * s_re - a_im * s_im + bu_ref[rows, re_cols],
                          a_re * s_im + a_im * s_re + bu_ref[rows, im_cols])
            bu_ref[rows, re_cols] = s_re
            bu_ref[rows, im_cols] = s_im
            return s_re, s_im

        zero = jnp.zeros((SUBLANES, lane_group), F32)
        e_re, e_im = lax.fori_loop(0, n_steps, local_scan, (zero, zero), unroll=4)

        row = lax.broadcasted_iota(jnp.int32, (SUBLANES, lane_group), 0)
        c_re = jnp.where(row == first, carry_ref[0, :, re_cols], 0.0)
        c_im = jnp.where(row == first, carry_ref[1, :, re_cols], 0.0)

        def segment_end(c_re, c_im):
            return e_re + an_re * c_re - an_im * c_im, e_im + an_re * c_im + an_im * c_re

        order = range(SUBLANES - 2, -1, -1) if reverse else range(1, SUBLANES)
        for k in order:
            t_re, t_im = segment_end(c_re, c_im)
            c_re = jnp.where(row == k, pltpu.roll(t_re, toward_later, axis=0), c_re)
            c_im = jnp.where(row == k, pltpu.roll(t_im, toward_later, axis=0), c_im)
        t_re, t_im = segment_end(c_re, c_im)
        carry_ref[0, :, re_cols] = jnp.broadcast_to(t_re[last:last + 1], t_re.shape)
        carry_ref[1, :, re_cols] = jnp.broadcast_to(t_im[last:last + 1], t_im.shape)

        def add_incoming(it, f):
            f_re, f_im = f
            rows = rows_of(it)
            f_re, f_im = a_re * f_re - a_im * f_im, a_re * f_im + a_im * f_re
            bu_ref[rows, re_cols] += f_re
            bu_ref[rows, im_cols] += f_im
            return f_re, f_im

        lax.fori_loop(0, n_steps, add_incoming, (c_re, c_im), unroll=4)


def _s5_kernel(u_ref, perm_ref, bmat_ref, tab_ref, cmat_ref, y_ref, bu_ref, carry_ref, *, lane_group):
    d = pl.program_id(1)
    c = pl.program_id(2)

    @pl.when(c == 0)
    def _():
        carry_ref[...] = jnp.zeros_like(carry_ref)

    u_perm = jnp.dot(perm_ref[...], u_ref[...], preferred_element_type=F32).astype(BF16)

    d_ssm, s2 = bmat_ref.shape
    s_dim = s2 // 2
    halves = [(slice(h * d_ssm // 2, (h + 1) * d_ssm // 2), slice(h * s_dim // 2, (h + 1) * s_dim // 2),
               slice(s_dim + h * s_dim // 2, s_dim + (h + 1) * s_dim // 2)) for h in range(2)]
    for ch, st_re, st_im in halves:
        u_half = u_perm[:, ch]
        bu_ref[:, st_re] = jnp.dot(u_half, bmat_ref[ch, st_re], preferred_element_type=F32)
        bu_ref[:, st_im] = jnp.dot(u_half, bmat_ref[ch, st_im], preferred_element_type=F32)

    @pl.when(d == 0)
    def _():
        _scan_chunk(bu_ref, tab_ref.at[0], carry_ref, reverse=False, lane_group=lane_group)

    @pl.when(d == 1)
    def _():
        _scan_chunk(bu_ref, tab_ref.at[0], carry_ref, reverse=True, lane_group=lane_group)

    for ch, st_re, st_im in halves:
        y_ref[0, :, ch] = (
            jnp.dot(bu_ref[:, st_re].astype(BF16), cmat_ref[0, st_re, ch], preferred_element_type=F32)
            + jnp.dot(bu_ref[:, st_im].astype(BF16), cmat_ref[0, st_im, ch], preferred_element_type=F32))


def _segment_interleave(tc):
    dst = jnp.arange(tc)
    src = (dst % SUBLANES) * (tc // SUBLANES) + dst // SUBLANES
    return (src[:, None] == jnp.arange(tc)[None, :]).astype(BF16)


def _s5_scan(proj, bmat, tabs, cmat, *, batch, seq_len, u_col, d_ssm, tc):
    n = proj.shape[0]
    n_chunks = seq_len // tc
    s2 = bmat.shape[1]
    s_dim = s2 // 2
    lane_group = min(s_dim, 8 * LANES)

    def chunk_row(b, d, c):
        return b * n_chunks + jnp.where(d == 0, c, n_chunks - 1 - c)

    return pl.pallas_call(
        functools.partial(_s5_kernel, lane_group=lane_group),
        grid=(batch, 2, n_chunks),
        in_specs=[
            pl.BlockSpec((tc, d_ssm), lambda b, d, c: (chunk_row(b, d, c), u_col)),
            pl.BlockSpec((tc, tc), lambda b, d, c: (0, 0)),
            pl.BlockSpec((d_ssm, s2), lambda b, d, c: (0, 0)),
            pl.BlockSpec((1, 4, SUBLANES, s_dim), lambda b, d, c: (d, 0, 0, 0)),
            pl.BlockSpec((1, s2, d_ssm), lambda b, d, c: (d, 0, 0)),
        ],
        out_specs=pl.BlockSpec((1, tc, d_ssm), lambda b, d, c: (d, chunk_row(b, d, c), 0)),
        out_shape=jax.ShapeDtypeStruct((2, n, d_ssm), F32),
        scratch_shapes=[pltpu.VMEM((tc, s2), F32), pltpu.VMEM((2, SUBLANES, s_dim), F32)],
        compiler_params=_params("s5_scan", ("parallel", "arbitrary", "arbitrary")),
        name="s5_scan",
    )(proj, _segment_interleave(tc), bmat, tabs, cmat)


def _glu_kernel(ys_ref, u_ref, dskip_ref, w_ref, b_ref, o_ref, y_ref, *, tc):
    n_lane_tiles, tm, _ = y_ref.shape
    y_sum = ys_ref[0] + ys_ref[1]
    for j in range(n_lane_tiles):
        y_ref[j] = y_sum[:, j * LANES:(j + 1) * LANES]
    n_steps = tc // SUBLANES
    pieces = [jnp.concatenate([y_ref[j, pl.ds(base + g, n_steps, stride=SUBLANES), :]
                               for j in range(n_lane_tiles)], axis=1)
              for base in range(0, tm, tc) for g in range(SUBLANES)]
    y = jnp.concatenate(pieces, axis=0) + dskip_ref[...] * u_ref[...].astype(F32)
    y = jax.nn.gelu(y)
    z = jnp.dot(y.astype(BF16), w_ref[...], preferred_element_type=F32) + b_ref[...]
    o_ref[...] = (y * jax.nn.sigmoid(z)).astype(o_ref.dtype)


def _s5_glu(ys, proj, dskip, w, b, *, u_col, tm, tc):
    _, n, d_ssm = ys.shape
    assert tm % tc == 0
    return pl.pallas_call(
        functools.partial(_glu_kernel, tc=tc),
        scratch_shapes=[pltpu.VMEM((d_ssm // min(d_ssm, LANES), tm, min(d_ssm, LANES)), F32)],
        grid=(n // tm,),
        in_specs=[
            pl.BlockSpec((2, tm, d_ssm), lambda i: (0, i, 0)),
            pl.BlockSpec((tm, d_ssm), lambda i: (i, u_col)),
            pl.BlockSpec((1, d_ssm), lambda i: (0, 0)),
            pl.BlockSpec((d_ssm, d_ssm), lambda i: (0, 0)),
            pl.BlockSpec((1, d_ssm), lambda i: (0, 0)),
        ],
        out_specs=pl.BlockSpec((tm, d_ssm), lambda i: (i, 0)),
        out_shape=jax.ShapeDtypeStruct((n, d_ssm), BF16),
        compiler_params=_params("s5_glu", ("parallel",)),
        name="s5_glu",
    )(ys, proj, dskip, w, b)


def _attn_kernel(q_ref, k_ref, v_ref, lam_ref, sg_ref, o_ref, *, lambda_init, head_dim, kc):
    q = q_ref[0]
    tq = q.shape[0]
    lane = lax.broadcasted_iota(jnp.int32, q.shape, 1)
    zero = jnp.zeros_like(q)
    qq = jnp.concatenate([jnp.where(lane < head_dim, q, zero), jnp.where(lane >= head_dim, q, zero)], axis=0)
    n_chunks = k_ref.shape[1] // kc
    sub = 2 * SUBLANES

    def scores(c):
        return lax.dot_general(k_ref[0, pl.ds(c * kc, kc), :], qq, (((1,), (1,)), ((), ())),
                               preferred_element_type=F32)

    def exp_stage(s, m_new):
        pieces, part = [], None
        for r in range(0, kc, sub):
            e_r = jnp.exp2(s[r:r + sub] - m_new)
            pieces.append(e_r.astype(BF16))
            for g in range(sub // SUBLANES):
                tile = e_r[g * SUBLANES:(g + 1) * SUBLANES]
                part = tile if part is None else part + tile
        return jnp.concatenate(pieces, axis=0), jnp.sum(part, axis=0, keepdims=True)

    def value_stage(acc, item):
        c, e_b, alpha = item
        pv = lax.dot_general(v_ref[0, pl.ds(c * kc, kc), :], e_b, (((0,), (0,)), ((), ())),
                             preferred_element_type=F32)
        return pv if acc is None else acc * alpha + pv

    score_lead = 2
    m = l = acc = None
    s_q = [scores(c) for c in range(min(score_lead, n_chunks))]
    pending = None
    for c in range(n_chunks):
        if c + score_lead < n_chunks:
            s_q.append(scores(c + score_lead))
        s = s_q[c]
        m_c = jnp.max(s, axis=0, keepdims=True)
        m_new = m_c if m is None else jnp.maximum(m, m_c)
        e_b, l_c = exp_stage(s, m_new)
        alpha = None if m is None else jnp.exp2(m - m_new)
        l = l_c if m is None else l * alpha + l_c
        if pending is not None:
            acc = value_stage(acc, pending)
        pending = (c, e_b, alpha)
        m = m_new
    acc = value_stage(acc, pending)
    lv = lam_ref[...]
    lam = (jnp.exp(jnp.sum(lv[0:1] * lv[1:2], axis=-1, keepdims=True))
           - jnp.exp(jnp.sum(lv[2:3] * lv[3:4], axis=-1, keepdims=True)) + lambda_init)
    o_t = acc[:, 0:tq] * (1.0 / l[:, 0:tq]) - acc[:, tq:2 * tq] * (lam / l[:, tq:2 * tq])
    o_ref[0] = (_rms(o_t.T, sg_ref[...]) * (1.0 - lambda_init)).astype(o_ref.dtype)


def _attention(proj3, lam_vec, subln, *, q_col, k_col, v_col, n_heads, head_dim, lambda_init, tq, kc):
    batch, seq_len, _ = proj3.shape
    vd = 2 * head_dim
    return pl.pallas_call(
        functools.partial(_attn_kernel, lambda_init=lambda_init, head_dim=head_dim, kc=kc),
        grid=(batch, n_heads, seq_len // tq),
        in_specs=[
            pl.BlockSpec((1, tq, vd), lambda b, h, i: (b, i, q_col + h)),
            pl.BlockSpec((1, seq_len, vd), lambda b, h, i: (b, 0, k_col + h)),
            pl.BlockSpec((1, seq_len, vd), lambda b, h, i: (b, 0, v_col + h)),
            pl.BlockSpec((4, head_dim), lambda b, h, i: (0, 0)),
            pl.BlockSpec((1, vd), lambda b, h, i: (0, 0)),
        ],
        out_specs=pl.BlockSpec((1, tq, vd), lambda b, h, i: (b, i, h)),
        out_shape=jax.ShapeDtypeStruct((batch, seq_len, n_heads * vd), BF16),
        compiler_params=_params("diff_attn", ("parallel", "parallel", "arbitrary")),
        name="diff_attn",
    )(proj3, proj3, proj3, lam_vec, subln)


def _conv_kernel(bg_ref, cg_ref, xv_ref, w_ref, o_ref):
    z = cg_ref[0].astype(F32) * xv_ref[0].astype(F32)
    seq_len = z.shape[0]
    row = lax.broadcasted_iota(jnp.int32, z.shape, 0)
    z_prev = jnp.where(row == 0, 0.0, pltpu.roll(z, 1, axis=0))
    z_next = jnp.where(row == seq_len - 1, 0.0, pltpu.roll(z, seq_len - 1, axis=0))
    w = w_ref[...]
    zc = w[0:1] * z_prev + w[1:2] * z + w[2:3] * z_next
    o_ref[0] = (bg_ref[0].astype(F32) * zc).astype(o_ref.dtype)


def _short_conv(proj3, conv_w, *, bg_col, d_conv):
    batch, seq_len, _ = proj3.shape
    tc = min(d_conv, LANES)
    per = d_conv // tc
    return pl.pallas_call(
        _conv_kernel,
        grid=(batch, per),
        in_specs=[
            pl.BlockSpec((1, seq_len, tc), lambda b, j: (b, 0, bg_col * per + j)),
            pl.BlockSpec((1, seq_len, tc), lambda b, j: (b, 0, (bg_col + 1) * per + j)),
            pl.BlockSpec((1, seq_len, tc), lambda b, j: (b, 0, (bg_col + 2) * per + j)),
            pl.BlockSpec((conv_w.shape[0], tc), lambda b, j: (0, j)),
        ],
        out_specs=pl.BlockSpec((1, seq_len, tc), lambda b, j: (b, 0, j)),
        out_shape=jax.ShapeDtypeStruct((batch, seq_len, d_conv), BF16),
        compiler_params=_params("short_conv", ("parallel", "parallel")),
        name="short_conv",
    )(proj3, proj3, proj3, conv_w)


def _merge_kernel(x_ref, ya_ref, yb_ref, yc_ref, ga_ref, gb_ref, gc_ref, wb_ref, wo_ref, o_ref):
    r_a = ya_ref.shape[1]
    r_b = r_a + yb_ref.shape[1]
    pa = jnp.dot(ya_ref[...], wb_ref[0:r_a, :], preferred_element_type=F32)
    merged = ga_ref[...].astype(F32) * pa
    pb = jnp.dot(yb_ref[...], wb_ref[r_a:r_b, :], preferred_element_type=F32)
    merged += gb_ref[...].astype(F32) * pb
    pc = jnp.dot(yc_ref[...], wb_ref[r_b:, :], preferred_element_type=F32)
    merged += gc_ref[...].astype(F32) * pc
    o_ref[...] = x_ref[...] + jnp.dot(merged.astype(BF16), wo_ref[...], preferred_element_type=F32)


def _merge(x, ya, yb, yc, proj, wb, wo, *, tm):
    n, d = x.shape
    row = lambda i: (i, 0)
    const = lambda i: (0, 0)
    return pl.pallas_call(
        _merge_kernel,
        grid=(n // tm,),
        in_specs=[
            pl.BlockSpec((tm, d), row),
            pl.BlockSpec((tm, ya.shape[1]), row),
            pl.BlockSpec((tm, yb.shape[1]), row),
            pl.BlockSpec((tm, yc.shape[1]), row),
            pl.BlockSpec((tm, d), lambda i: (i, 0)),
            pl.BlockSpec((tm, d), lambda i: (i, 1)),
            pl.BlockSpec((tm, d), lambda i: (i, 2)),
            pl.BlockSpec(wb.shape, const, pipeline_mode=pl.Buffered(1)),
            pl.BlockSpec(wo.shape, const, pipeline_mode=pl.Buffered(1)),
        ],
        out_specs=pl.BlockSpec((tm, d), row),
        out_shape=jax.ShapeDtypeStruct((n, d), F32),
        compiler_params=_params("merge", ("parallel",)),
        name="merge",
    )(x, ya, yb, yc, proj, proj, proj, wb, wo)


def _rope_tables(seq_len, head_dim):
    pos = jnp.arange(seq_len, dtype=F32)
    inv = ROPE_THETA ** (-jnp.arange(0, head_dim, 2, dtype=F32) / head_dim)
    ang = pos[:, None] * inv[None, :]
    cos, sin = jnp.cos(ang), jnp.sin(ang)
    reps = LANES // head_dim
    return (jnp.tile(jnp.concatenate([cos, cos], axis=1), (1, reps)),
            jnp.tile(jnp.concatenate([-sin, sin], axis=1), (1, reps)))


def _block_diag(blocks):
    g, r, c = blocks.shape
    wide = jnp.transpose(blocks, (1, 0, 2)).reshape(r, g * c)
    row_g = jnp.arange(g * r)[:, None] // r
    col_g = jnp.arange(g * c)[None, :] // c
    return jnp.where(row_g == col_g, jnp.tile(wide, (g, 1)), 0.0)


def _s5_tables(lam_re, lam_im, log_dt, b_re, b_im, c_re, c_im, n_steps):
    bmat = jnp.concatenate([_block_diag(jnp.swapaxes(b_re, 1, 2)), _block_diag(jnp.swapaxes(b_im, 1, 2))], axis=1)
    cmats, tabs = [], []
    for d in range(2):
        lr, li = lam_re[d], lam_im[d]
        dt = jnp.exp(log_dt[d])[:, None]
        mag = jnp.exp(dt * lr)
        ar, ai = mag * jnp.cos(dt * li), mag * jnp.sin(dt * li)
        denom = lr * lr + li * li
        nr = ar - 1.0
        kr = (nr * lr + ai * li) / denom
        ki = (ai * lr - nr * li) / denom
        er = c_re * kr[:, None, :] - c_im * ki[:, None, :]
        ei = c_re * ki[:, None, :] + c_im * kr[:, None, :]
        cmats.append(jnp.concatenate([_block_diag(jnp.swapaxes(er, 1, 2)), _block_diag(jnp.swapaxes(-ei, 1, 2))],
                                     axis=0))
        ar, ai = ar.reshape(-1), ai.reshape(-1)
        pr, pi, sr, si, e = jnp.ones_like(ar), jnp.zeros_like(ai), ar, ai, n_steps
        while e:
            if e & 1:
                pr, pi = pr * sr - pi * si, pr * si + pi * sr
            sr, si = sr * sr - si * si, 2.0 * sr * si
            e >>= 1
        tabs.append(jnp.stack([jnp.broadcast_to(v[None, :], (SUBLANES, v.shape[0])) for v in (ar, ai, pr, pi)]))
    return bmat.astype(BF16), jnp.stack(tabs), jnp.stack(cmats).astype(BF16)


def kernel(x, norm_w, ffn_w13, ffn_w2, w_in, s5_lambda_re, s5_lambda_im, s5_log_dt, s5_b_re, s5_b_im, s5_c_re, s5_c_im, s5_d, s5_w_glu, s5_b_glu, diff_lambda, diff_subln, conv_w, w_branch, w_gate, b_gate, w_out, final_norm):
    batch, seq_len, d_model = x.shape
    depth = norm_w.shape[0]
    d_ff = ffn_w2.shape[2]
    d_ssm = s5_d.shape[-1]
    d_conv = conv_w.shape[-1]
    d_attn = (w_in.shape[-1] - d_ssm - 3 * d_conv) // 3
    head_dim = diff_lambda.shape[-1]
    n_heads = d_attn // (2 * head_dim)
    n = batch * seq_len
    assert 2 * head_dim == LANES and d_ssm == d_conv and d_attn == 2 * d_ssm

    tm_ffn = _tile(n, 1024)
    tf = min(2 * LANES, d_ff)
    tm_proj = _tile(seq_len, 1024)
    tn_proj = d_attn
    tm_merge = _tile(n, 512)
    tm_glu = _tile(seq_len, 1024)
    tq = _tile(seq_len, 1024)
    kc_attn = _tile(seq_len, 512)
    t_chunk = _tile(seq_len, 512)

    cos, sin = _rope_tables(seq_len, head_dim)
    xf = x.reshape(n, d_model)
    gate_cols = N_BRANCH * d_model
    u_col = gate_cols // d_ssm
    q_col = (gate_cols + d_ssm) // LANES
    k_col = q_col + d_attn // LANES
    v_col = k_col + d_attn // LANES
    bg_col = (gate_cols + d_ssm + 3 * d_attn) // d_conv
    b_gate3 = b_gate[:, None, :]

    for l in range(depth):
        lambda_init = 0.8 - 0.6 * math.exp(-0.3 * l)
        bmat, tabs, cmat = _s5_tables(s5_lambda_re[l], s5_lambda_im[l], s5_log_dt[l],
                                      s5_b_re[l], s5_b_im[l], s5_c_re[l], s5_c_im[l], t_chunk // SUBLANES)

        xf, h_mix = _ffn(xf, norm_w[l, 0][None, :], ffn_w13, ffn_w2, norm_w[l, 1][None, :], layer=l, idx=0,
                         post="next", tm=tm_ffn, tf=tf)

        proj = _proj(h_mix, w_gate, b_gate3, w_in, cos, sin, layer=l, seq_len=seq_len, u_cols=d_ssm,
                     attn_cols=d_attn, q_scale=head_dim ** -0.5 * math.log2(math.e), tm=tm_proj, tn=tn_proj)
        proj3 = proj.reshape(batch, seq_len, proj.shape[1])

        ys = _s5_scan(proj, bmat, tabs, cmat, batch=batch, seq_len=seq_len, u_col=u_col, d_ssm=d_ssm,
                      tc=t_chunk)
        y_a = _s5_glu(ys, proj, s5_d[l][None, :], s5_w_glu[l].astype(BF16), s5_b_glu[l][None, :],
                      u_col=u_col, tm=tm_glu, tc=t_chunk)
        y_b = _attention(proj3, diff_lambda[l], diff_subln[l][None, :], q_col=q_col, k_col=k_col, v_col=v_col,
                         n_heads=n_heads, head_dim=head_dim, lambda_init=lambda_init, tq=tq, kc=kc_attn)
        y_c = _short_conv(proj3, conv_w[l], bg_col=bg_col, d_conv=d_conv)

        xf = _merge(xf, y_a, y_b.reshape(n, d_attn), y_c.reshape(n, d_conv), proj,
                    w_branch[l].astype(BF16), w_out[l].astype(BF16), tm=tm_merge)

        xf = _ffn(xf, norm_w[l, 2][None, :], ffn_w13, ffn_w2, final_norm[None, :], layer=l, idx=1,
                  post="final" if l == depth - 1 else "none", tm=tm_ffn, tf=tf)
    return xf.reshape(batch, seq_len, d_model)
```

```python
import functools
import math

import jax
import jax.numpy as jnp
from jax import lax
from jax.experimental import pallas as pl
from jax.experimental.pallas import tpu as pltpu

NORM_EPS = 1e-6
ROPE_THETA = 10000.0
N_BRANCH = 3
LANES = 128
SUBLANES = 8
MIB = 1024 * 1024
F32 = jnp.float32
BF16 = jnp.bfloat16


def _rms(x, g):
    return x * lax.rsqrt(jnp.mean(x * x, axis=-1, keepdims=True) + NORM_EPS) * g


def _round_up(a, b):
    return (a + b - 1) // b * b


def _tile(n, want):
    t = min(n, want)
    while n % t:
        t -= 1
    return t


VMEM_LIMIT_MIB = {"ffn": 62, "proj": 56, "s5_scan": 52, "s5_glu": 40, "diff_attn": 52, "short_conv": 40,
                  "merge": 56}


def _params(name, sem):
    return pltpu.CompilerParams(dimension_semantics=sem, vmem_limit_bytes=VMEM_LIMIT_MIB[name] * MIB)


def _ffn_kernel(x_ref, g_ref, wa_ref, *rest, post, d_ff, n_units):
    wb_refs = rest[:n_units]
    w2_refs = rest[n_units:2 * n_units]
    if post == "next":
        fg_ref, o_ref, hn_ref, act_ref = rest[2 * n_units:]
        h_ref = hn_ref
    else:
        fg_ref, o_ref, act_ref, h_ref = rest[2 * n_units:]
    j = pl.program_id(1)
    last = pl.num_programs(1) - 1
    tf = wa_ref.shape[1]

    def up():
        h = h_ref[...]
        a = jnp.dot(h, wa_ref[...].astype(BF16), preferred_element_type=F32)
        wb = jnp.concatenate([r[...].astype(BF16) for r in wb_refs], axis=1)
        b = jnp.dot(h, wb, preferred_element_type=F32)
        col = j * tf + lax.broadcasted_iota(jnp.int32, a.shape, 1)
        act_ref[...] = jnp.where(col < d_ff, a * jax.nn.sigmoid(a) * b, 0.0).astype(BF16)

    def down():
        w2 = jnp.concatenate([r[...].astype(BF16) for r in w2_refs], axis=0)
        return jnp.dot(act_ref[...], w2, preferred_element_type=F32)

    @pl.when(j == 0)
    def _():
        h_ref[...] = _rms(x_ref[...], g_ref[...]).astype(BF16)
        up()

    @pl.when(j == 1)
    def _():
        o_ref[...] = down()
        up()

    @pl.when((j > 1) & (j < last))
    def _():
        o_ref[...] += down()
        up()

    @pl.when(j == last)
    def _():
        y = x_ref[...] + 0.5 * (o_ref[...] + down())
        if post == "final":
            y = _rms(y, fg_ref[...])
        if post == "next":
            hn_ref[...] = _rms(y, fg_ref[...]).astype(BF16)
        o_ref[...] = y


def _ffn(x, g, w13, w2, fg, *, layer, idx, post, tm, tf):
    n, d = x.shape
    row_blk = pl.BlockSpec((tm, d), lambda i, j: (i, 0))
    out_specs, out_shape = row_blk, jax.ShapeDtypeStruct((n, d), F32)
    if post == "next":
        out_specs, out_shape = [row_blk, row_blk], [out_shape, jax.ShapeDtypeStruct((n, d), BF16)]
    d_ff = w2.shape[2]
    assert d_ff % LANES == 0 and tf % LANES == 0 and 2 * d_ff >= _round_up(d_ff, tf)
    n_units = tf // LANES
    ff_units = d_ff // LANES
    n_tiles = pl.cdiv(d_ff, tf)
    assert n_tiles >= 2

    def up_tile(j):
        return jnp.minimum(j, n_tiles - 1)

    def down_tile(j):
        return jnp.maximum(j - 1, 0)

    def wb_spec(r):
        return pl.BlockSpec((None, None, d, LANES), lambda i, j: (
            layer, idx, 0, ff_units + jnp.minimum(up_tile(j) * n_units + r, ff_units - 1)))

    def w2_spec(r):
        return pl.BlockSpec((None, None, LANES, d), lambda i, j: (
            layer, idx, jnp.minimum(down_tile(j) * n_units + r, ff_units - 1), 0))

    return pl.pallas_call(
        functools.partial(_ffn_kernel, post=post, d_ff=d_ff, n_units=n_units),
        grid=(n // tm, n_tiles + 1),
        in_specs=[
            row_blk,
            pl.BlockSpec((1, d), lambda i, j: (0, 0)),
            pl.BlockSpec((None, None, d, tf), lambda i, j: (layer, idx, 0, up_tile(j))),
            *[wb_spec(r) for r in range(n_units)],
            *[w2_spec(r) for r in range(n_units)],
            pl.BlockSpec((1, d), lambda i, j: (0, 0)),
        ],
        out_specs=out_specs,
        out_shape=out_shape,
        scratch_shapes=[pltpu.VMEM((tm, tf), BF16)] + ([] if post == "next" else [pltpu.VMEM((tm, d), BF16)]),
        compiler_params=_params("ffn", ("parallel", "arbitrary")),
        name="ffn",
    )(x, g, w13, *([w13] * n_units), *([w2] * n_units), fg)


def _rope(acc, cos, sin, scale):
    lane = lax.broadcasted_iota(jnp.int32, cos.shape, 1)
    first_half = (lane & (LANES // 4)) == 0
    out = []
    for c in range(acc.shape[1] // LANES):
        blk = acc[:, c * LANES:(c + 1) * LANES]
        partner = jnp.where(first_half,
                            pltpu.roll(blk, LANES - LANES // 4, axis=1),
                            pltpu.roll(blk, LANES // 4, axis=1))
        out.append((blk * cos + partner * sin) * scale)
    return jnp.concatenate(out, axis=1)


def _proj_kernel(h_ref, wg_ref, wi_ref, b_ref, cos_ref, sin_ref, o_ref, *,
                 n_gate, n_in, u_cols, attn_cols, q_scale, chunk):
    j = pl.program_id(0)
    tn = o_ref.shape[1]

    def run(w_ref, epilogue):
        for c in range(tn // chunk):
            cols = slice(c * chunk, (c + 1) * chunk)
            acc = jnp.dot(h_ref[...], w_ref[:, cols].astype(BF16), preferred_element_type=F32)
            o_ref[:, cols] = epilogue(acc, c).astype(o_ref.dtype)

    @pl.when(j < n_gate)
    def _():
        run(wg_ref, lambda acc, c: jax.nn.sigmoid(acc + b_ref[:, c * chunk:(c + 1) * chunk]))

    def in_epilogue(blk):
        def epilogue(acc, c):
            col = blk * tn + c * chunk
            if u_cols <= col < u_cols + attn_cols:
                return _rope(acc, cos_ref[...], sin_ref[...], q_scale)
            if u_cols + attn_cols <= col < u_cols + 2 * attn_cols:
                return _rope(acc, cos_ref[...], sin_ref[...], 1.0)
            return acc
        return epilogue

    for blk in range(n_in):
        @pl.when(j == n_gate + blk)
        def _(blk=blk):
            run(wi_ref, in_epilogue(blk))


def _proj(h, w_gate, b_gate, w_in, cos, sin, *, layer, seq_len, u_cols, attn_cols, q_scale, tm, tn):
    n, d = h.shape
    n_gate = w_gate.shape[2] // tn
    n_in = w_in.shape[2] // tn
    chunk = min(u_cols, 2 * LANES)
    assert u_cols % chunk == 0 and attn_cols % chunk == 0 and tn % chunk == 0
    pos_blocks = seq_len // tm
    gate_blk = lambda j, i: (layer, 0, jnp.minimum(j, n_gate - 1))
    return pl.pallas_call(
        functools.partial(_proj_kernel, n_gate=n_gate, n_in=n_in, u_cols=u_cols, attn_cols=attn_cols,
                          q_scale=q_scale, chunk=chunk),
        grid=(n_gate + n_in, n // tm),
        in_specs=[
            pl.BlockSpec((tm, d), lambda j, i: (i, 0)),
            pl.BlockSpec((None, d, tn), gate_blk),
            pl.BlockSpec((None, d, tn), lambda j, i: (layer, 0, jnp.maximum(j - n_gate, 0))),
            pl.BlockSpec((None, 1, tn), gate_blk),
            pl.BlockSpec((tm, LANES), lambda j, i: (i % pos_blocks, 0)),
            pl.BlockSpec((tm, LANES), lambda j, i: (i % pos_blocks, 0)),
        ],
        out_specs=pl.BlockSpec((tm, tn), lambda j, i: (i, j)),
        out_shape=jax.ShapeDtypeStruct((n, (n_gate + n_in) * tn), BF16),
        compiler_params=_params("proj", ("arbitrary", "arbitrary")),
        name="proj",
    )(h, w_gate, w_in, b_gate, cos, sin)


def _scan_chunk(bu_ref, tab_ref, carry_ref, *, reverse, lane_group):
    t_len, s2 = bu_ref.shape
    s_dim = s2 // 2
    n_steps = t_len // SUBLANES
    first, last = (SUBLANES - 1, 0) if reverse else (0, SUBLANES - 1)
    toward_later = SUBLANES - 1 if reverse else 1
    for lg in range(s_dim // lane_group):
        re_cols = pl.ds(lg * lane_group, lane_group)
        im_cols = pl.ds(s_dim + lg * lane_group, lane_group)
        a_re, a_im = tab_ref[0, :, re_cols], tab_ref[1, :, re_cols]
        an_re, an_im = tab_ref[2, :, re_cols], tab_ref[3, :, re_cols]

        def rows_of(it):
            step = (n_steps - 1 - it) if reverse else it
            return pl.ds(pl.multiple_of(step * SUBLANES, SUBLANES), SUBLANES)

        def local_scan(it, state):
            s_re, s_im = state
            rows = rows_of(it)
            s_re, s_im = (a_re * s_re - a_im * s_im + bu_ref[rows, re_cols],
                          a_re * s_im + a_im * s_re + bu_ref[rows, im_cols])
            bu_ref[rows, re_cols] = s_re
            bu_ref[rows, im_cols] = s_im
            return s_re, s_im

        zero = jnp.zeros((SUBLANES, lane_group), F32)
        e_re, e_im = lax.fori_loop(0, n_steps, local_scan, (zero, zero), unroll=4)

        row = lax.broadcasted_iota(jnp.int32, (SUBLANES, lane_group), 0)
        c_re = jnp.where(row == first, carry_ref[0, :, re_cols], 0.0)
        c_im = jnp.where(row == first, carry_ref[1, :, re_cols], 0.0)

        def segment_end(c_re, c_im):
            return e_re + an_re * c_re - an_im * c_im, e_im + an_re * c_im + an_im * c_re

        order = range(SUBLANES - 2, -1, -1) if reverse else range(1, SUBLANES)
        for k in order:
            t_re, t_im = segment_end(c_re, c_im)
            c_re = jnp.where(row == k, pltpu.roll(t_re, toward_later, axis=0), c_re)
            c_im = jnp.where(row == k, pltpu.roll(t_im, toward_later, axis=0), c_im)
        t_re, t_im = segment_end(c_re, c_im)
        carry_ref[0, :, re_cols] = jnp.broadcast_to(t_re[last:last + 1], t_re.shape)
        carry_ref[1, :, re_cols] = jnp.broadcast_to(t_im[last:last + 1], t_im.shape)

        def add_incoming(it, f):
            f_re, f_im = f
            rows = rows_of(it)
            f_re, f_im = a_re * f_re - a_im * f_im, a_re * f_im + a_im * f_re
            bu_ref[rows, re_cols] += f_re
            bu_ref[rows, im_cols] += f_im
            return f_re, f_im

        lax.fori_loop(0, n_steps, add_incoming, (c_re, c_im), unroll=4)


def _s5_kernel(u_ref, perm_ref, bmat_ref, tab_ref, cmat_ref, y_ref, bu_ref, carry_ref, *, lane_group):
    d = pl.program_id(1)
    c = pl.program_id(2)

    @pl.when(c == 0)
    def _():
        carry_ref[...] = jnp.zeros_like(carry_ref)

    u_perm = jnp.dot(perm_ref[...], u_ref[...], preferred_element_type=F32).astype(BF16)

    d_ssm, s2 = bmat_ref.shape
    s_dim = s2 // 2
    halves = [(slice(h * d_ssm // 2, (h + 1) * d_ssm // 2), slice(h * s_dim // 2, (h + 1) * s_dim // 2),
               slice(s_dim + h * s_dim // 2, s_dim + (h + 1) * s_dim // 2)) for h in range(2)]
    for ch, st_re, st_im in halves:
        u_half = u_perm[:, ch]
        bu_ref[:, st_re] = jnp.dot(u_half, bmat_ref[ch, st_re], preferred_element_type=F32)
        bu_ref[:, st_im] = jnp.dot(u_half, bmat_ref[ch, st_im], preferred_element_type=F32)

    @pl.when(d == 0)
    def _():
        _scan_chunk(bu_ref, tab_ref.at[0], carry_ref, reverse=False, lane_group=lane_group)

    @pl.when(d == 1)
    def _():
        _scan_chunk(bu_ref, tab_ref.at[0], carry_ref, reverse=True, lane_group=lane_group)

    for ch, st_re, st_im in halves:
        y_ref[0, :, ch] = (
            jnp.dot(bu_ref[:, st_re].astype(BF16), cmat_ref[0, st_re, ch], preferred_element_type=F32)
            + jnp.dot(bu_ref[:, st_im].astype(BF16), cmat_ref[0, st_im, ch], preferred_element_type=F32))


def _segment_interleave(tc):
    dst = jnp.arange(tc)
    src = (dst % SUBLANES) * (tc // SUBLANES) + dst // SUBLANES
    return (src[:, None] == jnp.arange(tc)[None, :]).astype(BF16)


def _s5_scan(proj, bmat, tabs, cmat, *, batch, seq_len, u_col, d_ssm, tc):
    n = proj.shape[0]
    n_chunks = seq_len // tc
    s2 = bmat.shape[1]
    s_dim = s2 // 2
    lane_group = min(s_dim, 8 * LANES)

    def chunk_row(b, d, c):
        return b * n_chunks + jnp.where(d == 0, c, n_chunks - 1 - c)

    return pl.pallas_call(
        functools.partial(_s5_kernel, lane_group=lane_group),
        grid=(batch, 2, n_chunks),
        in_specs=[
            pl.BlockSpec((tc, d_ssm), lambda b, d, c: (chunk_row(b, d, c), u_col)),
            pl.BlockSpec((tc, tc), lambda b, d, c: (0, 0)),
            pl.BlockSpec((d_ssm, s2), lambda b, d, c: (0, 0)),
            pl.BlockSpec((1, 4, SUBLANES, s_dim), lambda b, d, c: (d, 0, 0, 0)),
            pl.BlockSpec((1, s2, d_ssm), lambda b, d, c: (d, 0, 0)),
        ],
        out_specs=pl.BlockSpec((1, tc, d_ssm), lambda b, d, c: (d, chunk_row(b, d, c), 0)),
        out_shape=jax.ShapeDtypeStruct((2, n, d_ssm), F32),
        scratch_shapes=[pltpu.VMEM((tc, s2), F32), pltpu.VMEM((2, SUBLANES, s_dim), F32)],
        compiler_params=_params("s5_scan", ("parallel", "arbitrary", "arbitrary")),
        name="s5_scan",
    )(proj, _segment_interleave(tc), bmat, tabs, cmat)


def _glu_kernel(ys_ref, u_ref, dskip_ref, w_ref, b_ref, o_ref, y_ref, *, tc):
    n_lane_tiles, tm, _ = y_ref.shape
    y_sum = ys_ref[0] + ys_ref[1]
    for j in range(n_lane_tiles):
        y_ref[j] = y_sum[:, j * LANES:(j + 1) * LANES]
    n_steps = tc // SUBLANES
    pieces = [jnp.concatenate([y_ref[j, pl.ds(base + g, n_steps, stride=SUBLANES), :]
                               for j in range(n_lane_tiles)], axis=1)
              for base in range(0, tm, tc) for g in range(SUBLANES)]
    y = jnp.concatenate(pieces, axis=0) + dskip_ref[...] * u_ref[...].astype(F32)
    y = jax.nn.gelu(y)
    z = jnp.dot(y.astype(BF16), w_ref[...], preferred_element_type=F32) + b_ref[...]
    o_ref[...] = (y * jax.nn.sigmoid(z)).astype(o_ref.dtype)


def _s5_glu(ys, proj, dskip, w, b, *, u_col, tm, tc):
    _, n, d_ssm = ys.shape
    assert tm % tc == 0
    return pl.pallas_call(
        functools.partial(_glu_kernel, tc=tc),
        scratch_shapes=[pltpu.VMEM((d_ssm // min(d_ssm, LANES), tm, min(d_ssm, LANES)), F32)],
        grid=(n // tm,),
        in_specs=[
            pl.BlockSpec((2, tm, d_ssm), lambda i: (0, i, 0)),
            pl.BlockSpec((tm, d_ssm), lambda i: (i, u_col)),
            pl.BlockSpec((1, d_ssm), lambda i: (0, 0)),
            pl.BlockSpec((d_ssm, d_ssm), lambda i: (0, 0)),
            pl.BlockSpec((1, d_ssm), lambda i: (0, 0)),
        ],
        out_specs=pl.BlockSpec((tm, d_ssm), lambda i: (i, 0)),
        out_shape=jax.ShapeDtypeStruct((n, d_ssm), BF16),
        compiler_params=_params("s5_glu", ("parallel",)),
        name="s5_glu",
    )(ys, proj, dskip, w, b)


def _attn_kernel(q_ref, k_ref, v_ref, lam_ref, sg_ref, o_ref, *, lambda_init, head_dim, kc):
    q = q_ref[0]
    tq = q.shape[0]
    lane = lax.broadcasted_iota(jnp.int32, q.shape, 1)
    zero = jnp.zeros_like(q)
    qq = jnp.concatenate([jnp.where(lane < head_dim, q, zero), jnp.where(lane >= head_dim, q, zero)], axis=0)
    n_chunks = k_ref.shape[1] // kc
    sub = 2 * SUBLANES

    def scores(c):
        return lax.dot_general(k_ref[0, pl.ds(c * kc, kc), :], qq, (((1,), (1,)), ((), ())),
                               preferred_element_type=F32)

    def exp_stage(s, m_new):
        pieces, part = [], None
        for r in range(0, kc, sub):
            e_r = jnp.exp2(s[r:r + sub] - m_new)
            pieces.append(e_r.astype(BF16))
            for g in range(sub // SUBLANES):
                tile = e_r[g * SUBLANES:(g + 1) * SUBLANES]
                part = tile if part is None else part + tile
        return jnp.concatenate(pieces, axis=0), jnp.sum(part, axis=0, keepdims=True)

    def value_stage(acc, item):
        c, e_b, alpha = item
        pv = lax.dot_general(v_ref[0, pl.ds(c * kc, kc), :], e_b, (((0,), (0,)), ((), ())),
                             preferred_element_type=F32)
        return pv if acc is None else acc * alpha + pv

    score_lead = 2
    m = l = acc = None
    s_q = [scores(c) for c in range(min(score_lead, n_chunks))]
    pending = None
    for c in range(n_chunks):
        if c + score_lead < n_chunks:
            s_q.append(scores(c + score_lead))
        s = s_q[c]
        m_c = jnp.max(s, axis=0, keepdims=True)
        m_new = m_c if m is None else jnp.maximum(m, m_c)
        e_b, l_c = exp_stage(s, m_new)
        alpha = None if m is None else jnp.exp2(m - m_new)
        l = l_c if m is None else l * alpha + l_c
        if pending is not None:
            acc = value_stage(acc, pending)
        pending = (c, e_b, alpha)
        m = m_new
    acc = value_stage(acc, pending)
    lv = lam_ref[...]
    lam = (jnp.exp(jnp.sum(lv[0:1] * lv[1:2], axis=-1, keepdims=True))
           - jnp.exp(jnp.sum(lv[2:3] * lv[3:4], axis=-1, keepdims=True)) + lambda_init)
    o_t = acc[:, 0:tq] * (1.0 / l[:, 0:tq]) - acc[:, tq:2 * tq] * (lam / l[:, tq:2 * tq])
    o_ref[0] = (_rms(o_t.T, sg_ref[...]) * (1.0 - lambda_init)).astype(o_ref.dtype)


def _attention(proj3, lam_vec, subln, *, q_col, k_col, v_col, n_heads, head_dim, lambda_init, tq, kc):
    batch, seq_len, _ = proj3.shape
    vd = 2 * head_dim
    return pl.pallas_call(
        functools.partial(_attn_kernel, lambda_init=lambda_init, head_dim=head_dim, kc=kc),
        grid=(batch, n_heads, seq_len // tq),
        in_specs=[
            pl.BlockSpec((1, tq, vd), lambda b, h, i: (b, i, q_col + h)),
            pl.BlockSpec((1, seq_len, vd), lambda b, h, i: (b, 0, k_col + h)),
            pl.BlockSpec((1, seq_len, vd), lambda b, h, i: (b, 0, v_col + h)),
            pl.BlockSpec((4, head_dim), lambda b, h, i: (0, 0)),
            pl.BlockSpec((1, vd), lambda b, h, i: (0, 0)),
        ],
        out_specs=pl.BlockSpec((1, tq, vd), lambda b, h, i: (b, i, h)),
        out_shape=jax.ShapeDtypeStruct((batch, seq_len, n_heads * vd), BF16),
        compiler_params=_params("diff_attn", ("parallel", "parallel", "arbitrary")),
        name="diff_attn",
    )(proj3, proj3, proj3, lam_vec, subln)


def _conv_kernel(bg_ref, cg_ref, xv_ref, w_ref, o_ref):
    z = cg_ref[0].astype(F32) * xv_ref[0].astype(F32)
    seq_len = z.shape[0]
    row = lax.broadcasted_iota(jnp.int32, z.shape, 0)
    z_prev = jnp.where(row == 0, 0.0, pltpu.roll(z, 1, axis=0))
    z_next = jnp.where(row == seq_len - 1, 0.0, pltpu.roll(z, seq_len - 1, axis=0))
    w = w_ref[...]
    zc = w[0:1] * z_prev + w[1:2] * z + w[2:3] * z_next
    o_ref[0] = (bg_ref[0].astype(F32) * zc).astype(o_ref.dtype)


def _short_conv(proj3, conv_w, *, bg_col, d_conv):
    batch, seq_len, _ = proj3.shape
    tc = min(d_conv, LANES)
    per = d_conv // tc
    return pl.pallas_call(
        _conv_kernel,
        grid=(batch, per),
        in_specs=[
            pl.BlockSpec((1, seq_len, tc), lambda b, j: (b, 0, bg_col * per + j)),
            pl.BlockSpec((1, seq_len, tc), lambda b, j: (b, 0, (bg_col + 1) * per + j)),
            pl.BlockSpec((1, seq_len, tc), lambda b, j: (b, 0, (bg_col + 2) * per + j)),
            pl.BlockSpec((conv_w.shape[0], tc), lambda b, j: (0, j)),
        ],
        out_specs=pl.BlockSpec((1, seq_len, tc), lambda b, j: (b, 0, j)),
        out_shape=jax.ShapeDtypeStruct((batch, seq_len, d_conv), BF16),
        compiler_params=_params("short_conv", ("parallel", "parallel")),
        name="short_conv",
    )(proj3, proj3, proj3, conv_w)


def _merge_kernel(x_ref, ya_ref, yb_ref, yc_ref, ga_ref, gb_ref, gc_ref, wb_ref, wo_ref, o_ref):
    r_a = ya_ref.shape[1]
    r_b = r_a + yb_ref.shape[1]
    pa = jnp.dot(ya_ref[...], wb_ref[0:r_a, :], preferred_element_type=F32)
    merged = ga_ref[...].astype(F32) * pa
    pb = jnp.dot(yb_ref[...], wb_ref[r_a:r_b, :], preferred_element_type=F32)
    merged += gb_ref[...].astype(F32) * pb
    pc = jnp.dot(yc_ref[...], wb_ref[r_b:, :], preferred_element_type=F32)
    merged += gc_ref[...].astype(F32) * pc
    o_ref[...] = x_ref[...] + jnp.dot(merged.astype(BF16), wo_ref[...], preferred_element_type=F32)


def _merge(x, ya, yb, yc, proj, wb, wo, *, tm):
    n, d = x.shape
    row = lambda i: (i, 0)
    const = lambda i: (0, 0)
    return pl.pallas_call(
        _merge_kernel,
        grid=(n // tm,),
        in_specs=[
            pl.BlockSpec((tm, d), row),
            pl.BlockSpec((tm, ya.shape[1]), row),
            pl.BlockSpec((tm, yb.shape[1]), row),
            pl.BlockSpec((tm, yc.shape[1]), row),
            pl.BlockSpec((tm, d), lambda i: (i, 0)),
            pl.BlockSpec((tm, d), lambda i: (i, 1)),
            pl.BlockSpec((tm, d), lambda i: (i, 2)),
            pl.BlockSpec(wb.shape, const, pipeline_mode=pl.Buffered(1)),
            pl.BlockSpec(wo.shape, const, pipeline_mode=pl.Buffered(1)),
        ],
        out_specs=pl.BlockSpec((tm, d), row),
        out_shape=jax.ShapeDtypeStruct((n, d), F32),
        compiler_params=_params("merge", ("parallel",)),
        name="merge",
    )(x, ya, yb, yc, proj, proj, proj, wb, wo)


def _rope_tables(seq_len, head_dim):
    pos = jnp.arange(seq_len, dtype=F32)
    inv = ROPE_THETA ** (-jnp.arange(0, head_dim, 2, dtype=F32) / head_dim)
    ang = pos[:, None] * inv[None, :]
    cos, sin = jnp.cos(ang), jnp.sin(ang)
    reps = LANES // head_dim
    return (jnp.tile(jnp.concatenate([cos, cos], axis=1), (1, reps)),
            jnp.tile(jnp.concatenate([-sin, sin], axis=1), (1, reps)))


def _block_diag(blocks):
    g, r, c = blocks.shape
    wide = jnp.transpose(blocks, (1, 0, 2)).reshape(r, g * c)
    row_g = jnp.arange(g * r)[:, None] // r
    col_g = jnp.arange(g * c)[None, :] // c
    return jnp.where(row_g == col_g, jnp.tile(wide, (g, 1)), 0.0)


def _s5_tables(lam_re, lam_im, log_dt, b_re, b_im, c_re, c_im, n_steps):
    bmat = jnp.concatenate([_block_diag(jnp.swapaxes(b_re, 1, 2)), _block_diag(jnp.swapaxes(b_im, 1, 2))], axis=1)
    cmats, tabs = [], []
    for d in range(2):
        lr, li = lam_re[d], lam_im[d]
        dt = jnp.exp(log_dt[d])[:, None]
        mag = jnp.exp(dt * lr)
        ar, ai = mag * jnp.cos(dt * li), mag * jnp.sin(dt * li)
        denom = lr * lr + li * li
        nr = ar - 1.0
        kr = (nr * lr + ai * li) / denom
        ki = (ai * lr - nr * li) / denom
        er = c_re * kr[:, None, :] - c_im * ki[:, None, :]
        ei = c_re * ki[:, None, :] + c_im * kr[:, None, :]
        cmats.append(jnp.concatenate([_block_diag(jnp.swapaxes(er, 1, 2)), _block_diag(jnp.swapaxes(-ei, 1, 2))],
                                     axis=0))
        ar, ai = ar.reshape(-1), ai.reshape(-1)
        pr, pi, sr, si, e = jnp.ones_like(ar), jnp.zeros_like(ai), ar, ai, n_steps
        while e:
            if e & 1:
                pr, pi = pr * sr - pi * si, pr * si + pi * sr
            sr, si = sr * sr - si * si, 2.0 * sr * si
            e >>= 1
        tabs.append(jnp.stack([jnp.broadcast_to(v[None, :], (SUBLANES, v.shape[0])) for v in (ar, ai, pr, pi)]))
    return bmat.astype(BF16), jnp.stack(tabs), jnp.stack(cmats).astype(BF16)


def kernel(x, norm_w, ffn_w13, ffn_w2, w_in, s5_lambda_re, s5_lambda_im, s5_log_dt, s5_b_re, s5_b_im, s5_c_re, s5_c_im, s5_d, s5_w_glu, s5_b_glu, diff_lambda, diff_subln, conv_w, w_branch, w_gate, b_gate, w_out, final_norm):
    batch, seq_len, d_model = x.shape
    depth = norm_w.shape[0]
    d_ff = ffn_w2.shape[2]
    d_ssm = s5_d.shape[-1]
    d_conv = conv_w.shape[-1]
    d_attn = (w_in.shape[-1] - d_ssm - 3 * d_conv) // 3
    head_dim = diff_lambda.shape[-1]
    n_heads = d_attn // (2 * head_dim)
    n = batch * seq_len
    assert 2 * head_dim == LANES and d_ssm == d_conv and d_attn == 2 * d_ssm

    tm_ffn = _tile(n, 1024)
    tf = min(2 * LANES, d_ff)
    tm_proj = _tile(seq_len, 1024)
    tn_proj = d_attn
    tm_merge = _tile(n, 512)
    tm_glu = _tile(seq_len, 1024)
    tq = _tile(seq_len, 1024)
    kc_attn = _tile(seq_len, 512)
    t_chunk = _tile(seq_len, 512)

    cos, sin = _rope_tables(seq_len, head_dim)
    xf = x.reshape(n, d_model)
    gate_cols = N_BRANCH * d_model
    u_col = gate_cols // d_ssm
    q_col = (gate_cols + d_ssm) // LANES
    k_col = q_col + d_attn // LANES
    v_col = k_col + d_attn // LANES
    bg_col = (gate_cols + d_ssm + 3 * d_attn) // d_conv
    b_gate3 = b_gate[:, None, :]

    for l in range(depth):
        lambda_init = 0.8 - 0.6 * math.exp(-0.3 * l)
        bmat, tabs, cmat = _s5_tables(s5_lambda_re[l], s5_lambda_im[l], s5_log_dt[l],
                                      s5_b_re[l], s5_b_im[l], s5_c_re[l], s5_c_im[l], t_chunk // SUBLANES)

        xf, h_mix = _ffn(xf, norm_w[l, 0][None, :], ffn_w13, ffn_w2, norm_w[l, 1][None, :], layer=l, idx=0,
                         post="next", tm=tm_ffn, tf=tf)

        proj = _proj(h_mix, w_gate, b_gate3, w_in, cos, sin, layer=l, seq_len=seq_len, u_cols=d_ssm,
                     attn_cols=d_attn, q_scale=head_dim ** -0.5 * math.log2(math.e), tm=tm_proj, tn=tn_proj)
        proj3 = proj.reshape(batch, seq_len, proj.shape[1])

        ys = _s5_scan(proj, bmat, tabs, cmat, batch=batch, seq_len=seq_len, u_col=u_col, d_ssm=d_ssm,
                      tc=t_chunk)
        y_a = _s5_glu(ys, proj, s5_d[l][None, :], s5_w_glu[l].astype(BF16), s5_b_glu[l][None, :],
                      u_col=u_col, tm=tm_glu, tc=t_chunk)
        y_b = _attention(proj3, diff_lambda[l], diff_subln[l][None, :], q_col=q_col, k_col=k_col, v_col=v_col,
                         n_heads=n_heads, head_dim=head_dim, lambda_init=lambda_init, tq=tq, kc=kc_attn)
        y_c = _short_conv(proj3, conv_w[l], bg_col=bg_col, d_conv=d_conv)

        xf = _merge(xf, y_a, y_b.reshape(n, d_attn), y_c.reshape(n, d_conv), proj,
                    w_branch[l].astype(BF16), w_out[l].astype(BF16), tm=tm_merge)

        xf = _ffn(xf, norm_w[l, 2][None, :], ffn_w13, ffn_w2, final_norm[None, :], layer=l, idx=1,
                  post="final" if l == depth - 1 else "none", tm=tm_ffn, tf=tf)
    return xf.reshape(batch, seq_len, d_model)
```

```python
import functools
import math

import jax
import jax.numpy as jnp
from jax import lax
from jax.experimental import pallas as pl
from jax.experimental.pallas import tpu as pltpu

NORM_EPS = 1e-6
ROPE_THETA = 10000.0
N_BRANCH = 3
LANES = 128
SUBLANES = 8
MIB = 1024 * 1024
F32 = jnp.float32
BF16 = jnp.bfloat16


def _rms(x, g):
    return x * lax.rsqrt(jnp.mean(x * x, axis=-1, keepdims=True) + NORM_EPS) * g


def _round_up(a, b):
    return (a + b - 1) // b * b


def _tile(n, want):
    t = min(n, want)
    while n % t:
        t -= 1
    return t


VMEM_LIMIT_MIB = {"ffn": 62, "proj": 56, "s5_scan": 52, "s5_glu": 40, "diff_attn": 52, "short_conv": 40,
                  "merge": 56}


def _params(name, sem):
    return pltpu.CompilerParams(dimension_semantics=sem, vmem_limit_bytes=VMEM_LIMIT_MIB[name] * MIB)


def _ff_tile_start(j, tf, d_ff):
    return jnp.minimum(j * (tf // LANES), (d_ff - tf) // LANES) * LANES


def _ffn_kernel(x_ref, g_ref, wa_ref, wb_ref, w2_ref, fg_ref, o_ref, *rest, post, d_ff):
    if post == "next":
        hn_ref, = rest
        h_ref = hn_ref
    else:
        h_ref, = rest
    j = pl.program_id(1)
    tf = wa_ref.shape[-1]

    @pl.when(j == 0)
    def _():
        h_ref[...] = _rms(x_ref[...], g_ref[...]).astype(BF16)
        o_ref[...] = jnp.zeros_like(o_ref)

    h = h_ref[...]
    a = jnp.dot(h, wa_ref[0, 0].astype(BF16), preferred_element_type=F32)
    b = jnp.dot(h, wb_ref[0, 0].astype(BF16), preferred_element_type=F32)
    col = _ff_tile_start(j, tf, d_ff) + lax.broadcasted_iota(jnp.int32, a.shape, 1)
    act = jnp.where(col >= j * tf, a * jax.nn.sigmoid(a) * b, 0.0).astype(BF16)
    o_ref[...] += jnp.dot(act, w2_ref[0, 0].astype(BF16), preferred_element_type=F32)

    @pl.when(j == pl.num_programs(1) - 1)
    def _():
        y = x_ref[...] + 0.5 * o_ref[...]
        if post == "final":
            y = _rms(y, fg_ref[...])
        if post == "next":
            hn_ref[...] = _rms(y, fg_ref[...]).astype(BF16)
        o_ref[...] = y


def _ffn(x, g, w13, w2, fg, *, layer, idx, post, tm, tf):
    n, d = x.shape
    row_blk = pl.BlockSpec((tm, d), lambda i, j: (i, 0))
    out_specs, out_shape = row_blk, jax.ShapeDtypeStruct((n, d), F32)
    if post == "next":
        out_specs, out_shape = [row_blk, row_blk], [out_shape, jax.ShapeDtypeStruct((n, d), BF16)]
    d_ff = w2.shape[2]
    assert d_ff % LANES == 0 and tf % LANES == 0 and tf <= d_ff
    start = lambda j: _ff_tile_start(j, tf, d_ff)
    elems = lambda *shape: tuple(pl.Element(s) for s in shape)

    return pl.pallas_call(
        functools.partial(_ffn_kernel, post=post, d_ff=d_ff),
        grid=(n // tm, pl.cdiv(d_ff, tf)),
        in_specs=[
            row_blk,
            pl.BlockSpec((1, d), lambda i, j: (0, 0)),
            pl.BlockSpec(elems(1, 1, d, tf), lambda i, j: (layer, idx, 0, start(j))),
            pl.BlockSpec(elems(1, 1, d, tf), lambda i, j: (layer, idx, 0, (d_ff // LANES + start(j) // LANES) * LANES)),
            pl.BlockSpec(elems(1, 1, tf, d), lambda i, j: (layer, idx, start(j), 0)),
            pl.BlockSpec((1, d), lambda i, j: (0, 0)),
        ],
        out_specs=out_specs,
        out_shape=out_shape,
        scratch_shapes=[] if post == "next" else [pltpu.VMEM((tm, d), BF16)],
        compiler_params=_params("ffn", ("parallel", "arbitrary")),
        name="ffn",
    )(x, g, w13, w13, w2, fg)


def _rope(acc, cos, sin, scale):
    lane = lax.broadcasted_iota(jnp.int32, cos.shape, 1)
    first_half = (lane & (LANES // 4)) == 0
    out = []
    for c in range(acc.shape[1] // LANES):
        blk = acc[:, c * LANES:(c + 1) * LANES]
        partner = jnp.where(first_half,
                            pltpu.roll(blk, LANES - LANES // 4, axis=1),
                            pltpu.roll(blk, LANES // 4, axis=1))
        out.append((blk * cos + partner * sin) * scale)
    return jnp.concatenate(out, axis=1)


def _proj_kernel(h_ref, wg_ref, wi_ref, b_ref, cos_ref, sin_ref, o_ref, *,
                 n_gate, n_in, u_cols, attn_cols, q_scale, chunk):
    j = pl.program_id(0)
    tn = o_ref.shape[1]

    def run(w_ref, epilogue):
        for c in range(tn // chunk):
            cols = slice(c * chunk, (c + 1) * chunk)
            acc = jnp.dot(h_ref[...], w_ref[:, cols].astype(BF16), preferred_element_type=F32)
            o_ref[:, cols] = epilogue(acc, c).astype(o_ref.dtype)

    @pl.when(j < n_gate)
    def _():
        run(wg_ref, lambda acc, c: jax.nn.sigmoid(acc + b_ref[:, c * chunk:(c + 1) * chunk]))

    def in_epilogue(blk):
        def epilogue(acc, c):
            col = blk * tn + c * chunk
            if u_cols <= col < u_cols + attn_cols:
                return _rope(acc, cos_ref[...], sin_ref[...], q_scale)
            if u_cols + attn_cols <= col < u_cols + 2 * attn_cols:
                return _rope(acc, cos_ref[...], sin_ref[...], 1.0)
            return acc
        return epilogue

    for blk in range(n_in):
        @pl.when(j == n_gate + blk)
        def _(blk=blk):
            run(wi_ref, in_epilogue(blk))


def _proj(h, w_gate, b_gate, w_in, cos, sin, *, layer, seq_len, u_cols, attn_cols, q_scale, tm, tn):
    n, d = h.shape
    n_gate = w_gate.shape[2] // tn
    n_in = w_in.shape[2] // tn
    chunk = min(u_cols, 2 * LANES)
    assert u_cols % chunk == 0 and attn_cols % chunk == 0 and tn % chunk == 0
    pos_blocks = seq_len // tm
    gate_blk = lambda j, i: (layer, 0, jnp.minimum(j, n_gate - 1))
    return pl.pallas_call(
        functools.partial(_proj_kernel, n_gate=n_gate, n_in=n_in, u_cols=u_cols, attn_cols=attn_cols,
                          q_scale=q_scale, chunk=chunk),
        grid=(n_gate + n_in, n // tm),
        in_specs=[
            pl.BlockSpec((tm, d), lambda j, i: (i, 0)),
            pl.BlockSpec((None, d, tn), gate_blk),
            pl.BlockSpec((None, d, tn), lambda j, i: (layer, 0, jnp.maximum(j - n_gate, 0))),
            pl.BlockSpec((None, 1, tn), gate_blk),
            pl.BlockSpec((tm, LANES), lambda j, i: (i % pos_blocks, 0)),
            pl.BlockSpec((tm, LANES), lambda j, i: (i % pos_blocks, 0)),
        ],
        out_specs=pl.BlockSpec((tm, tn), lambda j, i: (i, j)),
        out_shape=jax.ShapeDtypeStruct((n, (n_gate + n_in) * tn), BF16),
        compiler_params=_params("proj", ("arbitrary", "arbitrary")),
        name="proj",
    )(h, w_gate, w_in, b_gate, cos, sin)


def _scan_chunk(bu_ref, tab_ref, carry_ref, *, reverse, lane_group):
    t_len, s2 = bu_ref.shape
    s_dim = s2 // 2
    n_steps = t_len // SUBLANES
    first, last = (SUBLANES - 1, 0) if reverse else (0, SUBLANES - 1)
    toward_later = SUBLANES - 1 if reverse else 1
    for lg in range(s_dim // lane_group):
        re_cols = pl.ds(lg * lane_group, lane_group)
        im_cols = pl.ds(s_dim + lg * lane_group, lane_group)
        a_re, a_im = tab_ref[0, :, re_cols], tab_ref[1, :, re_cols]
        an_re, an_im = tab_ref[2, :, re_cols], tab_ref[3, :, re_cols]

        def rows_of(it):
            step = (n_steps - 1 - it) if reverse else it
            return pl.ds(pl.multiple_of(step * SUBLANES, SUBLANES), SUBLANES)

        def local_scan(it, state):
            s_re, s_im = state
            rows = rows_of(it)
            s_re, s_im = (a_re * s_re - a_im * s_im + bu_ref[rows, re_cols],
                          a_re * s_im + a_im * s_re + bu_ref[rows, im_cols])
            bu_ref[rows, re_cols] = s_re
            bu_ref[rows, im_cols] = s_im
            return s_re, s_im

        zero = jnp.zeros((SUBLANES, lane_group), F32)
        e_re, e_im = lax.fori_loop(0, n_steps, local_scan, (zero, zero), unroll=4)

        row = lax.broadcasted_iota(jnp.int32, (SUBLANES, lane_group), 0)
        c_re = jnp.where(row == first, carry_ref[0, :, re_cols], 0.0)
        c_im = jnp.where(row == first, carry_ref[1, :, re_cols], 0.0)

        def segment_end(c_re, c_im):
            return e_re + an_re * c_re - an_im * c_im, e_im + an_re * c_im + an_im * c_re

        order = range(SUBLANES - 2, -1, -1) if reverse else range(1, SUBLANES)
        for k in order:
            t_re, t_im = segment_end(c_re, c_im)
            c_re = jnp.where(row == k, pltpu.roll(t_re, toward_later, axis=0), c_re)
            c_im = jnp.where(row == k, pltpu.roll(t_im, toward_later, axis=0), c_im)
        t_re, t_im = segment_end(c_re, c_im)
        carry_ref[0, :, re_cols] = jnp.broadcast_to(t_re[last:last + 1], t_re.shape)
        carry_ref[1, :, re_cols] = jnp.broadcast_to(t_im[last:last + 1], t_im.shape)

        def add_incoming(it, f):
            f_re, f_im = f
            rows = rows_of(it)
            f_re, f_im = a_re * f_re - a_im * f_im, a_re * f_im + a_im * f_re
            bu_ref[rows, re_cols] += f_re
            bu_ref[rows, im_cols] += f_im
            return f_re, f_im

        lax.fori_loop(0, n_steps, add_incoming, (c_re, c_im), unroll=4)


def _s5_kernel(u_ref, perm_ref, bmat_ref, tab_ref, cmat_ref, y_ref, bu_ref, carry_ref, *, lane_group):
    d = pl.program_id(1)
    c = pl.program_id(2)

    @pl.when(c == 0)
    def _():
        carry_ref[...] = jnp.zeros_like(carry_ref)

    u_perm = jnp.dot(perm_ref[...], u_ref[...], preferred_element_type=F32).astype(BF16)

    d_ssm, s2 = bmat_ref.shape
    s_dim = s2 // 2
    halves = [(slice(h * d_ssm // 2, (h + 1) * d_ssm // 2), slice(h * s_dim // 2, (h + 1) * s_dim // 2),
               slice(s_dim + h * s_dim // 2, s_dim + (h + 1) * s_dim // 2)) for h in range(2)]
    for ch, st_re, st_im in halves:
        u_half = u_perm[:, ch]
        bu_ref[:, st_re] = jnp.dot(u_half, bmat_ref[ch, st_re], preferred_element_type=F32)
        bu_ref[:, st_im] = jnp.dot(u_half, bmat_ref[ch, st_im], preferred_element_type=F32)

    @pl.when(d == 0)
    def _():
        _scan_chunk(bu_ref, tab_ref.at[0], carry_ref, reverse=False, lane_group=lane_group)

    @pl.when(d == 1)
    def _():
        _scan_chunk(bu_ref, tab_ref.at[0], carry_ref, reverse=True, lane_group=lane_group)

    for ch, st_re, st_im in halves:
        y_ref[0, :, ch] = (
            jnp.dot(bu_ref[:, st_re].astype(BF16), cmat_ref[0, st_re, ch], preferred_element_type=F32)
            + jnp.dot(bu_ref[:, st_im].astype(BF16), cmat_ref[0, st_im, ch], preferred_element_type=F32))


def _segment_interleave(tc):
    dst = jnp.arange(tc)
    src = (dst % SUBLANES) * (tc // SUBLANES) + dst // SUBLANES
    return (src[:, None] == jnp.arange(tc)[None, :]).astype(BF16)


def _s5_scan(proj, bmat, tabs, cmat, *, batch, seq_len, u_col, d_ssm, tc):
    n = proj.shape[0]
    n_chunks = seq_len // tc
    s2 = bmat.shape[1]
    s_dim = s2 // 2
    lane_group = min(s_dim, 8 * LANES)

    def chunk_row(b, d, c):
        return b * n_chunks + jnp.where(d == 0, c, n_chunks - 1 - c)

    return pl.pallas_call(
        functools.partial(_s5_kernel, lane_group=lane_group),
        grid=(batch, 2, n_chunks),
        in_specs=[
            pl.BlockSpec((tc, d_ssm), lambda b, d, c: (chunk_row(b, d, c), u_col)),
            pl.BlockSpec((tc, tc), lambda b, d, c: (0, 0)),
            pl.BlockSpec((d_ssm, s2), lambda b, d, c: (0, 0)),
            pl.BlockSpec((1, 4, SUBLANES, s_dim), lambda b, d, c: (d, 0, 0, 0)),
            pl.BlockSpec((1, s2, d_ssm), lambda b, d, c: (d, 0, 0)),
        ],
        out_specs=pl.BlockSpec((1, tc, d_ssm), lambda b, d, c: (d, chunk_row(b, d, c), 0)),
        out_shape=jax.ShapeDtypeStruct((2, n, d_ssm), F32),
        scratch_shapes=[pltpu.VMEM((tc, s2), F32), pltpu.VMEM((2, SUBLANES, s_dim), F32)],
        compiler_params=_params("s5_scan", ("parallel", "arbitrary", "arbitrary")),
        name="s5_scan",
    )(proj, _segment_interleave(tc), bmat, tabs, cmat)


def _glu_kernel(ys_ref, u_ref, dskip_ref, w_ref, b_ref, o_ref, y_ref, *, tc):
    n_lane_tiles, tm, _ = y_ref.shape
    y_sum = ys_ref[0] + ys_ref[1]
    for j in range(n_lane_tiles):
        y_ref[j] = y_sum[:, j * LANES:(j + 1) * LANES]
    n_steps = tc // SUBLANES
    pieces = [jnp.concatenate([y_ref[j, pl.ds(base + g, n_steps, stride=SUBLANES), :]
                               for j in range(n_lane_tiles)], axis=1)
              for base in range(0, tm, tc) for g in range(SUBLANES)]
    y = jnp.concatenate(pieces, axis=0) + dskip_ref[...] * u_ref[...].astype(F32)
    y = jax.nn.gelu(y)
    z = jnp.dot(y.astype(BF16), w_ref[...], preferred_element_type=F32) + b_ref[...]
    o_ref[...] = (y * jax.nn.sigmoid(z)).astype(o_ref.dtype)


def _s5_glu(ys, proj, dskip, w, b, *, u_col, tm, tc):
    _, n, d_ssm = ys.shape
    assert tm % tc == 0
    return pl.pallas_call(
        functools.partial(_glu_kernel, tc=tc),
        scratch_shapes=[pltpu.VMEM((d_ssm // min(d_ssm, LANES), tm, min(d_ssm, LANES)), F32)],
        grid=(n // tm,),
        in_specs=[
            pl.BlockSpec((2, tm, d_ssm), lambda i: (0, i, 0)),
            pl.BlockSpec((tm, d_ssm), lambda i: (i, u_col)),
            pl.BlockSpec((1, d_ssm), lambda i: (0, 0)),
            pl.BlockSpec((d_ssm, d_ssm), lambda i: (0, 0)),
            pl.BlockSpec((1, d_ssm), lambda i: (0, 0)),
        ],
        out_specs=pl.BlockSpec((tm, d_ssm), lambda i: (i, 0)),
        out_shape=jax.ShapeDtypeStruct((n, d_ssm), BF16),
        compiler_params=_params("s5_glu", ("parallel",)),
        name="s5_glu",
    )(ys, proj, dskip, w, b)


def _attn_kernel(q_ref, k_ref, v_ref, lam_ref, sg_ref, o_ref, *, lambda_init, head_dim, kc):
    q = q_ref[0]
    tq = q.shape[0]
    lane = lax.broadcasted_iota(jnp.int32, q.shape, 1)
    zero = jnp.zeros_like(q)
    qq = jnp.concatenate([jnp.where(lane < head_dim, q, zero), jnp.where(lane >= head_dim, q, zero)], axis=0)
    n_chunks = k_ref.shape[1] // kc
    sub = 2 * SUBLANES

    def scores(c):
        return lax.dot_general(k_ref[0, pl.ds(c * kc, kc), :], qq, (((1,), (1,)), ((), ())),
                               preferred_element_type=F32)

    def exp_stage(s, m_new):
        pieces, part = [], None
        for r in range(0, kc, sub):
            e_r = jnp.exp2(s[r:r + sub] - m_new)
            pieces.append(e_r.astype(BF16))
            for g in range(sub // SUBLANES):
                tile = e_r[g * SUBLANES:(g + 1) * SUBLANES]
                part = tile if part is None else part + tile
        return jnp.concatenate(pieces, axis=0), jnp.sum(part, axis=0, keepdims=True)

    def value_stage(acc, item):
        c, e_b, alpha = item
        pv = lax.dot_general(v_ref[0, pl.ds(c * kc, kc), :], e_b, (((0,), (0,)), ((), ())),
                             preferred_element_type=F32)
        return pv if acc is None else acc * alpha + pv

    score_lead = 2
    m = l = acc = None
    s_q = [scores(c) for c in range(min(score_lead, n_chunks))]
    pending = None
    for c in range(n_chunks):
        if c + score_lead < n_chunks:
            s_q.append(scores(c + score_lead))
        s = s_q[c]
        m_c = jnp.max(s, axis=0, keepdims=True)
        m_new = m_c if m is None else jnp.maximum(m, m_c)
        e_b, l_c = exp_stage(s, m_new)
        alpha = None if m is None else jnp.exp2(m - m_new)
        l = l_c if m is None else l * alpha + l_c
        if pending is not None:
            acc = value_stage(acc, pending)
        pending = (c, e_b, alpha)
        m = m_new
    acc = value_stage(acc, pending)
    lv = lam_ref[...]
    lam = (jnp.exp(jnp.sum(lv[0:1] * lv[1:2], axis=-1, keepdims=True))
           - jnp.exp(jnp.sum(lv[2:3] * lv[3:4], axis=-1, keepdims=True)) + lambda_init)
    o_t = acc[:, 0:tq] * (1.0 / l[:, 0:tq]) - acc[:, tq:2 * tq] * (lam / l[:, tq:2 * tq])
    o_ref[0] = (_rms(o_t.T, sg_ref[...]) * (1.0 - lambda_init)).astype(o_ref.dtype)


def _attention(proj3, lam_vec, subln, *, q_col, k_col, v_col, n_heads, head_dim, lambda_init, tq, kc):
    batch, seq_len, _ = proj3.shape
    vd = 2 * head_dim
    return pl.pallas_call(
        functools.partial(_attn_kernel, lambda_init=lambda_init, head_dim=head_dim, kc=kc),
        grid=(batch, n_heads, seq_len // tq),
        in_specs=[
            pl.BlockSpec((1, tq, vd), lambda b, h, i: (b, i, q_col + h)),
            pl.BlockSpec((1, seq_len, vd), lambda b, h, i: (b, 0, k_col + h)),
            pl.BlockSpec((1, seq_len, vd), lambda b, h, i: (b, 0, v_col + h)),
            pl.BlockSpec((4, head_dim), lambda b, h, i: (0, 0)),
            pl.BlockSpec((1, vd), lambda b, h, i: (0, 0)),
        ],
        out_specs=pl.BlockSpec((1, tq, vd), lambda b, h, i: (b, i, h)),
        out_shape=jax.ShapeDtypeStruct((batch, seq_len, n_heads * vd), BF16),
        compiler_params=_params("diff_attn", ("parallel", "parallel", "arbitrary")),
        name="diff_attn",
    )(proj3, proj3, proj3, lam_vec, subln)


def _conv_kernel(bg_ref, cg_ref, xv_ref, w_ref, o_ref):
    z = cg_ref[0].astype(F32) * xv_ref[0].astype(F32)
    seq_len = z.shape[0]
    row = lax.broadcasted_iota(jnp.int32, z.shape, 0)
    z_prev = jnp.where(row == 0, 0.0, pltpu.roll(z, 1, axis=0))
    z_next = jnp.where(row == seq_len - 1, 0.0, pltpu.roll(z, seq_len - 1, axis=0))
    w = w_ref[...]
    zc = w[0:1] * z_prev + w[1:2] * z + w[2:3] * z_next
    o_ref[0] = (bg_ref[0].astype(F32) * zc).astype(o_ref.dtype)


def _short_conv(proj3, conv_w, *, bg_col, d_conv):
    batch, seq_len, _ = proj3.shape
    tc = min(d_conv, LANES)
    per = d_conv // tc
    return pl.pallas_call(
        _conv_kernel,
        grid=(batch, per),
        in_specs=[
            pl.BlockSpec((1, seq_len, tc), lambda b, j: (b, 0, bg_col * per + j)),
            pl.BlockSpec((1, seq_len, tc), lambda b, j: (b, 0, (bg_col + 1) * per + j)),
            pl.BlockSpec((1, seq_len, tc), lambda b, j: (b, 0, (bg_col + 2) * per + j)),
            pl.BlockSpec((conv_w.shape[0], tc), lambda b, j: (0, j)),
        ],
        out_specs=pl.BlockSpec((1, seq_len, tc), lambda b, j: (b, 0, j)),
        out_shape=jax.ShapeDtypeStruct((batch, seq_len, d_conv), BF16),
        compiler_params=_params("short_conv", ("parallel", "parallel")),
        name="short_conv",
    )(proj3, proj3, proj3, conv_w)


def _merge_kernel(x_ref, ya_ref, yb_ref, yc_ref, ga_ref, gb_ref, gc_ref, wb_ref, wo_ref, o_ref):
    r_a = ya_ref.shape[1]
    r_b = r_a + yb_ref.shape[1]
    pa = jnp.dot(ya_ref[...], wb_ref[0:r_a, :], preferred_element_type=F32)
    merged = ga_ref[...].astype(F32) * pa
    pb = jnp.dot(yb_ref[...], wb_ref[r_a:r_b, :], preferred_element_type=F32)
    merged += gb_ref[...].astype(F32) * pb
    pc = jnp.dot(yc_ref[...], wb_ref[r_b:, :], preferred_element_type=F32)
    merged += gc_ref[...].astype(F32) * pc
    o_ref[...] = x_ref[...] + jnp.dot(merged.astype(BF16), wo_ref[...], preferred_element_type=F32)


def _merge(x, ya, yb, yc, proj, wb, wo, *, tm):
    n, d = x.shape
    row = lambda i: (i, 0)
    const = lambda i: (0, 0)
    return pl.pallas_call(
        _merge_kernel,
        grid=(n // tm,),
        in_specs=[
            pl.BlockSpec((tm, d), row),
            pl.BlockSpec((tm, ya.shape[1]), row),
            pl.BlockSpec((tm, yb.shape[1]), row),
            pl.BlockSpec((tm, yc.shape[1]), row),
            pl.BlockSpec((tm, d), lambda i: (i, 0)),
            pl.BlockSpec((tm, d), lambda i: (i, 1)),
            pl.BlockSpec((tm, d), lambda i: (i, 2)),
            pl.BlockSpec(wb.shape, const, pipeline_mode=pl.Buffered(1)),
            pl.BlockSpec(wo.shape, const, pipeline_mode=pl.Buffered(1)),
        ],
        out_specs=pl.BlockSpec((tm, d), row),
        out_shape=jax.ShapeDtypeStruct((n, d), F32),
        compiler_params=_params("merge", ("parallel",)),
        name="merge",
    )(x, ya, yb, yc, proj, proj, proj, wb, wo)


def _rope_tables(seq_len, head_dim):
    pos = jnp.arange(seq_len, dtype=F32)
    inv = ROPE_THETA ** (-jnp.arange(0, head_dim, 2, dtype=F32) / head_dim)
    ang = pos[:, None] * inv[None, :]
    cos, sin = jnp.cos(ang), jnp.sin(ang)
    reps = LANES // head_dim
    return (jnp.tile(jnp.concatenate([cos, cos], axis=1), (1, reps)),
            jnp.tile(jnp.concatenate([-sin, sin], axis=1), (1, reps)))


def _block_diag(blocks):
    g, r, c = blocks.shape
    wide = jnp.transpose(blocks, (1, 0, 2)).reshape(r, g * c)
    row_g = jnp.arange(g * r)[:, None] // r
    col_g = jnp.arange(g * c)[None, :] // c
    return jnp.where(row_g == col_g, jnp.tile(wide, (g, 1)), 0.0)


def _s5_tables(lam_re, lam_im, log_dt, b_re, b_im, c_re, c_im, n_steps):
    bmat = jnp.concatenate([_block_diag(jnp.swapaxes(b_re, 1, 2)), _block_diag(jnp.swapaxes(b_im, 1, 2))], axis=1)
    cmats, tabs = [], []
    for d in range(2):
        lr, li = lam_re[d], lam_im[d]
        dt = jnp.exp(log_dt[d])[:, None]
        mag = jnp.exp(dt * lr)
        ar, ai = mag * jnp.cos(dt * li), mag * jnp.sin(dt * li)
        denom = lr * lr + li * li
        nr = ar - 1.0
        kr = (nr * lr + ai * li) / denom
        ki = (ai * lr - nr * li) / denom
        er = c_re * kr[:, None, :] - c_im * ki[:, None, :]
        ei = c_re * ki[:, None, :] + c_im * kr[:, None, :]
        cmats.append(jnp.concatenate([_block_diag(jnp.swapaxes(er, 1, 2)), _block_diag(jnp.swapaxes(-ei, 1, 2))],
                                     axis=0))
        ar, ai = ar.reshape(-1), ai.reshape(-1)
        pr, pi, sr, si, e = jnp.ones_like(ar), jnp.zeros_like(ai), ar, ai, n_steps
        while e:
            if e & 1:
                pr, pi = pr * sr - pi * si, pr * si + pi * sr
            sr, si = sr * sr - si * si, 2.0 * sr * si
            e >>= 1
        tabs.append(jnp.stack([jnp.broadcast_to(v[None, :], (SUBLANES, v.shape[0])) for v in (ar, ai, pr, pi)]))
    return bmat.astype(BF16), jnp.stack(tabs), jnp.stack(cmats).astype(BF16)


def kernel(x, norm_w, ffn_w13, ffn_w2, w_in, s5_lambda_re, s5_lambda_im, s5_log_dt, s5_b_re, s5_b_im, s5_c_re, s5_c_im, s5_d, s5_w_glu, s5_b_glu, diff_lambda, diff_subln, conv_w, w_branch, w_gate, b_gate, w_out, final_norm):
    batch, seq_len, d_model = x.shape
    depth = norm_w.shape[0]
    d_ff = ffn_w2.shape[2]
    d_ssm = s5_d.shape[-1]
    d_conv = conv_w.shape[-1]
    d_attn = (w_in.shape[-1] - d_ssm - 3 * d_conv) // 3
    head_dim = diff_lambda.shape[-1]
    n_heads = d_attn // (2 * head_dim)
    n = batch * seq_len
    assert 2 * head_dim == LANES and d_ssm == d_conv and d_attn == 2 * d_ssm

    tm_ffn = _tile(n, 1024)
    tf = min(2 * LANES, d_ff)
    tm_proj = _tile(seq_len, 1024)
    tn_proj = d_attn
    tm_merge = _tile(n, 512)
    tm_glu = _tile(seq_len, 1024)
    tq = _tile(seq_len, 1024)
    kc_attn = _tile(seq_len, 512)
    t_chunk = _tile(seq_len, 512)

    cos, sin = _rope_tables(seq_len, head_dim)
    xf = x.reshape(n, d_model)
    gate_cols = N_BRANCH * d_model
    u_col = gate_cols // d_ssm
    q_col = (gate_cols + d_ssm) // LANES
    k_col = q_col + d_attn // LANES
    v_col = k_col + d_attn // LANES
    bg_col = (gate_cols + d_ssm + 3 * d_attn) // d_conv
    b_gate3 = b_gate[:, None, :]

    for l in range(depth):
        lambda_init = 0.8 - 0.6 * math.exp(-0.3 * l)
        bmat, tabs, cmat = _s5_tables(s5_lambda_re[l], s5_lambda_im[l], s5_log_dt[l],
                                      s5_b_re[l], s5_b_im[l], s5_c_re[l], s5_c_im[l], t_chunk // SUBLANES)

        xf, h_mix = _ffn(xf, norm_w[l, 0][None, :], ffn_w13, ffn_w2, norm_w[l, 1][None, :], layer=l, idx=0,
                         post="next", tm=tm_ffn, tf=tf)

        proj = _proj(h_mix, w_gate, b_gate3, w_in, cos, sin, layer=l, seq_len=seq_len, u_cols=d_ssm,
                     attn_cols=d_attn, q_scale=head_dim ** -0.5 * math.log2(math.e), tm=tm_proj, tn=tn_proj)
        proj3 = proj.reshape(batch, seq_len, proj.shape[1])

        ys = _s5_scan(proj, bmat, tabs, cmat, batch=batch, seq_len=seq_len, u_col=u_col, d_ssm=d_ssm,
                      tc=t_chunk)
        y_a = _s5_glu(ys, proj, s5_d[l][None, :], s5_w_glu[l].astype(BF16), s5_b_glu[l][None, :],
                      u_col=u_col, tm=tm_glu, tc=t_chunk)
        y_b = _attention(proj3, diff_lambda[l], diff_subln[l][None, :], q_col=q_col, k_col=k_col, v_col=v_col,
                         n_heads=n_heads, head_dim=head_dim, lambda_init=lambda_init, tq=tq, kc=kc_attn)
        y_c = _short_conv(proj3, conv_w[l], bg_col=bg_col, d_conv=d_conv)

        xf = _merge(xf, y_a, y_b.reshape(n, d_attn), y_c.reshape(n, d_conv), proj,
                    w_branch[l].astype(BF16), w_out[l].astype(BF16), tm=tm_merge)

        xf = _ffn(xf, norm_w[l, 2][None, :], ffn_w13, ffn_w2, final_norm[None, :], layer=l, idx=1,
                  post="final" if l == depth - 1 else "none", tm=tm_ffn, tf=tf)
    return xf.reshape(batch, seq_len, d_model)
```

```python
import functools
import math

import jax
import jax.numpy as jnp
from jax import lax
from jax.experimental import pallas as pl
from jax.experimental.pallas import tpu as pltpu

NORM_EPS = 1e-6
ROPE_THETA = 10000.0
N_BRANCH = 3
LANES = 128
SUBLANES = 8
MIB = 1024 * 1024
F32 = jnp.float32
BF16 = jnp.bfloat16


def _rms(x, g):
    return x * lax.rsqrt(jnp.mean(x * x, axis=-1, keepdims=True) + NORM_EPS) * g


def _round_up(a, b):
    return (a + b - 1) // b * b


def _tile(n, want):
    t = min(n, want)
    while n % t:
        t -= 1
    return t


VMEM_LIMIT_MIB = {"ffn": 62, "proj": 56, "s5_scan": 52, "s5_glu": 40, "diff_attn": 52, "short_conv": 40,
                  "merge": 56}


def _params(name, sem):
    return pltpu.CompilerParams(dimension_semantics=sem, vmem_limit_bytes=VMEM_LIMIT_MIB[name] * MIB)


def _ff_tile_start(j, tf, d_ff):
    return jnp.minimum(j * (tf // LANES), (d_ff - tf) // LANES) * LANES


def _ffn_kernel(x_ref, g_ref, wa_ref, wb_ref, w2_ref, fg_ref, o_ref, *rest, post, d_ff):
    if post == "next":
        hn_ref, = rest
        h_ref = hn_ref
    else:
        h_ref, = rest
    j = pl.program_id(1)
    tf = wa_ref.shape[-1]

    @pl.when(j == 0)
    def _():
        h_ref[...] = _rms(x_ref[...], g_ref[...]).astype(BF16)
        o_ref[...] = jnp.zeros_like(o_ref)

    h = h_ref[...]
    a = jnp.dot(h, wa_ref[0, 0].astype(BF16), preferred_element_type=F32)
    b = jnp.dot(h, wb_ref[0, 0].astype(BF16), preferred_element_type=F32)
    col = _ff_tile_start(j, tf, d_ff) + lax.broadcasted_iota(jnp.int32, a.shape, 1)
    act = jnp.where(col >= j * tf, a * jax.nn.sigmoid(a) * b, 0.0).astype(BF16)
    o_ref[...] += jnp.dot(act, w2_ref[0, 0].astype(BF16), preferred_element_type=F32)

    @pl.when(j == pl.num_programs(1) - 1)
    def _():
        y = x_ref[...] + 0.5 * o_ref[...]
        if post == "final":
            y = _rms(y, fg_ref[...])
        if post == "next":
            hn_ref[...] = _rms(y, fg_ref[...]).astype(BF16)
        o_ref[...] = y


def _ffn(x, g, w13, w2, fg, *, layer, idx, post, tm, tf):
    n, d = x.shape
    row_blk = pl.BlockSpec((tm, d), lambda i, j: (i, 0))
    out_specs, out_shape = row_blk, jax.ShapeDtypeStruct((n, d), F32)
    if post == "next":
        out_specs, out_shape = [row_blk, row_blk], [out_shape, jax.ShapeDtypeStruct((n, d), BF16)]
    d_ff = w2.shape[2]
    assert d_ff % LANES == 0 and tf % LANES == 0 and tf <= d_ff
    start = lambda j: _ff_tile_start(j, tf, d_ff)
    elems = lambda *shape: tuple(pl.Element(s) for s in shape)

    return pl.pallas_call(
        functools.partial(_ffn_kernel, post=post, d_ff=d_ff),
        grid=(n // tm, pl.cdiv(d_ff, tf)),
        in_specs=[
            row_blk,
            pl.BlockSpec((1, d), lambda i, j: (0, 0)),
            pl.BlockSpec(elems(1, 1, d, tf), lambda i, j: (layer, idx, 0, start(j))),
            pl.BlockSpec(elems(1, 1, d, tf), lambda i, j: (layer, idx, 0, (d_ff // LANES + start(j) // LANES) * LANES)),
            pl.BlockSpec(elems(1, 1, tf, d), lambda i, j: (layer, idx, start(j), 0)),
            pl.BlockSpec((1, d), lambda i, j: (0, 0)),
        ],
        out_specs=out_specs,
        out_shape=out_shape,
        scratch_shapes=[] if post == "next" else [pltpu.VMEM((tm, d), BF16)],
        compiler_params=_params("ffn", ("parallel", "arbitrary")),
        name="ffn",
    )(x, g, w13, w13, w2, fg)


def _rope(acc, cos, sin, scale):
    lane = lax.broadcasted_iota(jnp.int32, cos.shape, 1)
    first_half = (lane & (LANES // 4)) == 0
    out = []
    for c in range(acc.shape[1] // LANES):
        blk = acc[:, c * LANES:(c + 1) * LANES]
        partner = jnp.where(first_half,
                            pltpu.roll(blk, LANES - LANES // 4, axis=1),
                            pltpu.roll(blk, LANES // 4, axis=1))
        out.append((blk * cos + partner * sin) * scale)
    return jnp.concatenate(out, axis=1)


def _proj_kernel(h_ref, wg_ref, wi_ref, b_ref, cos_ref, sin_ref, o_ref, *,
                 n_gate, n_in, u_cols, attn_cols, q_scale, chunk):
    j = pl.program_id(0)
    tn = o_ref.shape[1]

    def run(w_ref, epilogue):
        for c in range(tn // chunk):
            cols = slice(c * chunk, (c + 1) * chunk)
            acc = jnp.dot(h_ref[...], w_ref[:, cols].astype(BF16), preferred_element_type=F32)
            o_ref[:, cols] = epilogue(acc, c).astype(o_ref.dtype)

    @pl.when(j < n_gate)
    def _():
        run(wg_ref, lambda acc, c: jax.nn.sigmoid(acc + b_ref[:, c * chunk:(c + 1) * chunk]))

    def in_epilogue(blk):
        def epilogue(acc, c):
            col = blk * tn + c * chunk
            if u_cols <= col < u_cols + attn_cols:
                return _rope(acc, cos_ref[...], sin_ref[...], q_scale)
            if u_cols + attn_cols <= col < u_cols + 2 * attn_cols:
                return _rope(acc, cos_ref[...], sin_ref[...], 1.0)
            return acc
        return epilogue

    for blk in range(n_in):
        @pl.when(j == n_gate + blk)
        def _(blk=blk):
            run(wi_ref, in_epilogue(blk))


def _proj(h, w_gate, b_gate, w_in, cos, sin, *, layer, seq_len, u_cols, attn_cols, q_scale, tm, tn):
    n, d = h.shape
    n_gate = w_gate.shape[2] // tn
    n_in = w_in.shape[2] // tn
    chunk = min(u_cols, 2 * LANES)
    assert u_cols % chunk == 0 and attn_cols % chunk == 0 and tn % chunk == 0
    pos_blocks = seq_len // tm
    gate_blk = lambda j, i: (layer, 0, jnp.minimum(j, n_gate - 1))
    return pl.pallas_call(
        functools.partial(_proj_kernel, n_gate=n_gate, n_in=n_in, u_cols=u_cols, attn_cols=attn_cols,
                          q_scale=q_scale, chunk=chunk),
        grid=(n_gate + n_in, n // tm),
        in_specs=[
            pl.BlockSpec((tm, d), lambda j, i: (i, 0)),
            pl.BlockSpec((None, d, tn), gate_blk),
            pl.BlockSpec((None, d, tn), lambda j, i: (layer, 0, jnp.maximum(j - n_gate, 0))),
            pl.BlockSpec((None, 1, tn), gate_blk),
            pl.BlockSpec((tm, LANES), lambda j, i: (i % pos_blocks, 0)),
            pl.BlockSpec((tm, LANES), lambda j, i: (i % pos_blocks, 0)),
        ],
        out_specs=pl.BlockSpec((tm, tn), lambda j, i: (i, j)),
        out_shape=jax.ShapeDtypeStruct((n, (n_gate + n_in) * tn), BF16),
        compiler_params=_params("proj", ("arbitrary", "arbitrary")),
        name="proj",
    )(h, w_gate, w_in, b_gate, cos, sin)


def _scan_chunk(bu_ref, tab_ref, carry_ref, *, reverse, lane_group):
    t_len, s2 = bu_ref.shape
    s_dim = s2 // 2
    n_steps = t_len // SUBLANES
    first, last = (SUBLANES - 1, 0) if reverse else (0, SUBLANES - 1)
    toward_later = SUBLANES - 1 if reverse else 1
    for lg in range(s_dim // lane_group):
        re_cols = pl.ds(lg * lane_group, lane_group)
        im_cols = pl.ds(s_dim + lg * lane_group, lane_group)
        a_re, a_im = tab_ref[0, :, re_cols], tab_ref[1, :, re_cols]
        an_re, an_im = tab_ref[2, :, re_cols], tab_ref[3, :, re_cols]

        def rows_of(it):
            step = (n_steps - 1 - it) if reverse else it
            return pl.ds(pl.multiple_of(step * SUBLANES, SUBLANES), SUBLANES)

        def local_scan(it, state):
            s_re, s_im = state
            rows = rows_of(it)
            s_re, s_im = (a_re * s_re - a_im * s_im + bu_ref[rows, re_cols],
                          a_re * s_im + a_im * s_re + bu_ref[rows, im_cols])
            bu_ref[rows, re_cols] = s_re
            bu_ref[rows, im_cols] = s_im
            return s_re, s_im

        zero = jnp.zeros((SUBLANES, lane_group), F32)
        e_re, e_im = lax.fori_loop(0, n_steps, local_scan, (zero, zero), unroll=4)

        row = lax.broadcasted_iota(jnp.int32, (SUBLANES, lane_group), 0)
        c_re = jnp.where(row == first, carry_ref[0, :, re_cols], 0.0)
        c_im = jnp.where(row == first, carry_ref[1, :, re_cols], 0.0)

        def segment_end(c_re, c_im):
            return e_re + an_re * c_re - an_im * c_im, e_im + an_re * c_im + an_im * c_re

        order = range(SUBLANES - 2, -1, -1) if reverse else range(1, SUBLANES)
        for k in order:
            t_re, t_im = segment_end(c_re, c_im)
            c_re = jnp.where(row == k, pltpu.roll(t_re, toward_later, axis=0), c_re)
            c_im = jnp.where(row == k, pltpu.roll(t_im, toward_later, axis=0), c_im)
        t_re, t_im = segment_end(c_re, c_im)
        carry_ref[0, :, re_cols] = jnp.broadcast_to(t_re[last:last + 1], t_re.shape)
        carry_ref[1, :, re_cols] = jnp.broadcast_to(t_im[last:last + 1], t_im.shape)

        def add_incoming(it, f):
            f_re, f_im = f
            rows = rows_of(it)
            f_re, f_im = a_re * f_re - a_im * f_im, a_re * f_im + a_im * f_re
            bu_ref[rows, re_cols] += f_re
            bu_ref[rows, im_cols] += f_im
            return f_re, f_im

        lax.fori_loop(0, n_steps, add_incoming, (c_re, c_im), unroll=4)


def _s5_kernel(u_ref, perm_ref, bmat_ref, tab_ref, cmat_ref, y_ref, bu_ref, carry_ref, *, lane_group):
    d = pl.program_id(1)
    c = pl.program_id(2)

    @pl.when(c == 0)
    def _():
        carry_ref[...] = jnp.zeros_like(carry_ref)

    u_perm = jnp.dot(perm_ref[...], u_ref[...], preferred_element_type=F32).astype(BF16)

    d_ssm, s2 = bmat_ref.shape
    s_dim = s2 // 2
    halves = [(slice(h * d_ssm // 2, (h + 1) * d_ssm // 2), slice(h * s_dim // 2, (h + 1) * s_dim // 2),
               slice(s_dim + h * s_dim // 2, s_dim + (h + 1) * s_dim // 2)) for h in range(2)]
    for ch, st_re, st_im in halves:
        u_half = u_perm[:, ch]
        bu_ref[:, st_re] = jnp.dot(u_half, bmat_ref[ch, st_re], preferred_element_type=F32)
        bu_ref[:, st_im] = jnp.dot(u_half, bmat_ref[ch, st_im], preferred_element_type=F32)

    @pl.when(d == 0)
    def _():
        _scan_chunk(bu_ref, tab_ref.at[0], carry_ref, reverse=False, lane_group=lane_group)

    @pl.when(d == 1)
    def _():
        _scan_chunk(bu_ref, tab_ref.at[0], carry_ref, reverse=True, lane_group=lane_group)

    for ch, st_re, st_im in halves:
        y_ref[0, :, ch] = (
            jnp.dot(bu_ref[:, st_re].astype(BF16), cmat_ref[0, st_re, ch], preferred_element_type=F32)
            + jnp.dot(bu_ref[:, st_im].astype(BF16), cmat_ref[0, st_im, ch], preferred_element_type=F32))


def _segment_interleave(tc):
    dst = jnp.arange(tc)
    src = (dst % SUBLANES) * (tc // SUBLANES) + dst // SUBLANES
    return (src[:, None] == jnp.arange(tc)[None, :]).astype(BF16)


def _s5_scan(proj, bmat, tabs, cmat, *, batch, seq_len, u_col, d_ssm, tc):
    n = proj.shape[0]
    n_chunks = seq_len // tc
    s2 = bmat.shape[1]
    s_dim = s2 // 2
    lane_group = min(s_dim, 8 * LANES)

    def chunk_row(b, d, c):
        return b * n_chunks + jnp.where(d == 0, c, n_chunks - 1 - c)

    return pl.pallas_call(
        functools.partial(_s5_kernel, lane_group=lane_group),
        grid=(batch, 2, n_chunks),
        in_specs=[
            pl.BlockSpec((tc, d_ssm), lambda b, d, c: (chunk_row(b, d, c), u_col)),
            pl.BlockSpec((tc, tc), lambda b, d, c: (0, 0)),
            pl.BlockSpec((d_ssm, s2), lambda b, d, c: (0, 0)),
            pl.BlockSpec((1, 4, SUBLANES, s_dim), lambda b, d, c: (d, 0, 0, 0)),
            pl.BlockSpec((1, s2, d_ssm), lambda b, d, c: (d, 0, 0)),
        ],
        out_specs=pl.BlockSpec((1, tc, d_ssm), lambda b, d, c: (d, chunk_row(b, d, c), 0)),
        out_shape=jax.ShapeDtypeStruct((2, n, d_ssm), F32),
        scratch_shapes=[pltpu.VMEM((tc, s2), F32), pltpu.VMEM((2, SUBLANES, s_dim), F32)],
        compiler_params=_params("s5_scan", ("parallel", "arbitrary", "arbitrary")),
        name="s5_scan",
    )(proj, _segment_interleave(tc), bmat, tabs, cmat)


def _glu_kernel(ys_ref, u_ref, dskip_ref, w_ref, b_ref, o_ref, y_ref, *, tc):
    n_lane_tiles, tm, _ = y_ref.shape
    y_sum = ys_ref[0] + ys_ref[1]
    for j in range(n_lane_tiles):
        y_ref[j] = y_sum[:, j * LANES:(j + 1) * LANES]
    n_steps = tc // SUBLANES
    pieces = [jnp.concatenate([y_ref[j, pl.ds(base + g, n_steps, stride=SUBLANES), :]
                               for j in range(n_lane_tiles)], axis=1)
              for base in range(0, tm, tc) for g in range(SUBLANES)]
    y = jnp.concatenate(pieces, axis=0) + dskip_ref[...] * u_ref[...].astype(F32)
    y = jax.nn.gelu(y)
    z = jnp.dot(y.astype(BF16), w_ref[...], preferred_element_type=F32) + b_ref[...]
    o_ref[...] = (y * jax.nn.sigmoid(z)).astype(o_ref.dtype)


def _s5_glu(ys, proj, dskip, w, b, *, u_col, tm, tc):
    _, n, d_ssm = ys.shape
    assert tm % tc == 0
    return pl.pallas_call(
        functools.partial(_glu_kernel, tc=tc),
        scratch_shapes=[pltpu.VMEM((d_ssm // min(d_ssm, LANES), tm, min(d_ssm, LANES)), F32)],
        grid=(n // tm,),
        in_specs=[
            pl.BlockSpec((2, tm, d_ssm), lambda i: (0, i, 0)),
            pl.BlockSpec((tm, d_ssm), lambda i: (i, u_col)),
            pl.BlockSpec((1, d_ssm), lambda i: (0, 0)),
            pl.BlockSpec((d_ssm, d_ssm), lambda i: (0, 0)),
            pl.BlockSpec((1, d_ssm), lambda i: (0, 0)),
        ],
        out_specs=pl.BlockSpec((tm, d_ssm), lambda i: (i, 0)),
        out_shape=jax.ShapeDtypeStruct((n, d_ssm), BF16),
        compiler_params=_params("s5_glu", ("parallel",)),
        name="s5_glu",
    )(ys, proj, dskip, w, b)


def _attn_kernel(q_ref, k_ref, v_ref, lam_ref, sg_ref, o_ref, *, lambda_init, head_dim, kc):
    q = q_ref[0]
    tq = q.shape[0]
    lane = lax.broadcasted_iota(jnp.int32, q.shape, 1)
    zero = jnp.zeros_like(q)
    qq = jnp.concatenate([jnp.where(lane < head_dim, q, zero), jnp.where(lane >= head_dim, q, zero)], axis=0)
    n_chunks = k_ref.shape[1] // kc
    sub = 2 * SUBLANES

    def scores(c):
        return lax.dot_general(k_ref[0, pl.ds(c * kc, kc), :], qq, (((1,), (1,)), ((), ())),
                               preferred_element_type=F32)

    def exp_stage(s, m_new):
        pieces, part = [], None
        for r in range(0, kc, sub):
            e_r = jnp.exp2(s[r:r + sub] - m_new)
            pieces.append(e_r.astype(BF16))
            for g in range(sub // SUBLANES):
                tile = e_r[g * SUBLANES:(g + 1) * SUBLANES]
                part = tile if part is None else part + tile
        return jnp.concatenate(pieces, axis=0), jnp.sum(part, axis=0, keepdims=True)

    def value_stage(acc, item):
        c, e_b, alpha = item
        pv = lax.dot_general(v_ref[0, pl.ds(c * kc, kc), :], e_b, (((0,), (0,)), ((), ())),
                             preferred_element_type=F32)
        return pv if acc is None else acc * alpha + pv

    score_lead = 2
    m = l = acc = None
    s_q = [scores(c) for c in range(min(score_lead, n_chunks))]
    pending = None
    for c in range(n_chunks):
        if c + score_lead < n_chunks:
            s_q.append(scores(c + score_lead))
        s = s_q[c]
        m_c = jnp.max(s, axis=0, keepdims=True)
        m_new = m_c if m is None else jnp.maximum(m, m_c)
        e_b, l_c = exp_stage(s, m_new)
        alpha = None if m is None else jnp.exp2(m - m_new)
        l = l_c if m is None else l * alpha + l_c
        if pending is not None:
            acc = value_stage(acc, pending)
        pending = (c, e_b, alpha)
        m = m_new
    acc = value_stage(acc, pending)
    lv = lam_ref[...]
    lam = (jnp.exp(jnp.sum(lv[0:1] * lv[1:2], axis=-1, keepdims=True))
           - jnp.exp(jnp.sum(lv[2:3] * lv[3:4], axis=-1, keepdims=True)) + lambda_init)
    o_t = acc[:, 0:tq] * (1.0 / l[:, 0:tq]) - acc[:, tq:2 * tq] * (lam / l[:, tq:2 * tq])
    o_ref[0] = (_rms(o_t.T, sg_ref[...]) * (1.0 - lambda_init)).astype(o_ref.dtype)


def _attention(proj3, lam_vec, subln, *, q_col, k_col, v_col, n_heads, head_dim, lambda_init, tq, kc):
    batch, seq_len, _ = proj3.shape
    vd = 2 * head_dim
    return pl.pallas_call(
        functools.partial(_attn_kernel, lambda_init=lambda_init, head_dim=head_dim, kc=kc),
        grid=(batch, n_heads, seq_len // tq),
        in_specs=[
            pl.BlockSpec((1, tq, vd), lambda b, h, i: (b, i, q_col + h)),
            pl.BlockSpec((1, seq_len, vd), lambda b, h, i: (b, 0, k_col + h)),
            pl.BlockSpec((1, seq_len, vd), lambda b, h, i: (b, 0, v_col + h)),
            pl.BlockSpec((4, head_dim), lambda b, h, i: (0, 0)),
            pl.BlockSpec((1, vd), lambda b, h, i: (0, 0)),
        ],
        out_specs=pl.BlockSpec((1, tq, vd), lambda b, h, i: (b, i, h)),
        out_shape=jax.ShapeDtypeStruct((batch, seq_len, n_heads * vd), BF16),
        compiler_params=_params("diff_attn", ("parallel", "parallel", "arbitrary")),
        name="diff_attn",
    )(proj3, proj3, proj3, lam_vec, subln)


def _conv_kernel(bg_ref, cg_ref, xv_ref, w_ref, o_ref):
    z = cg_ref[0].astype(F32) * xv_ref[0].astype(F32)
    seq_len = z.shape[0]
    row = lax.broadcasted_iota(jnp.int32, z.shape, 0)
    z_prev = jnp.where(row == 0, 0.0, pltpu.roll(z, 1, axis=0))
    z_next = jnp.where(row == seq_len - 1, 0.0, pltpu.roll(z, seq_len - 1, axis=0))
    w = w_ref[...]
    zc = w[0:1] * z_prev + w[1:2] * z + w[2:3] * z_next
    o_ref[0] = (bg_ref[0].astype(F32) * zc).astype(o_ref.dtype)


def _short_conv(proj3, conv_w, *, bg_col, d_conv):
    batch, seq_len, _ = proj3.shape
    tc = min(d_conv, LANES)
    per = d_conv // tc
    return pl.pallas_call(
        _conv_kernel,
        grid=(batch, per),
        in_specs=[
            pl.BlockSpec((1, seq_len, tc), lambda b, j: (b, 0, bg_col * per + j)),
            pl.BlockSpec((1, seq_len, tc), lambda b, j: (b, 0, (bg_col + 1) * per + j)),
            pl.BlockSpec((1, seq_len, tc), lambda b, j: (b, 0, (bg_col + 2) * per + j)),
            pl.BlockSpec((conv_w.shape[0], tc), lambda b, j: (0, j)),
        ],
        out_specs=pl.BlockSpec((1, seq_len, tc), lambda b, j: (b, 0, j)),
        out_shape=jax.ShapeDtypeStruct((batch, seq_len, d_conv), BF16),
        compiler_params=_params("short_conv", ("parallel", "parallel")),
        name="short_conv",
    )(proj3, proj3, proj3, conv_w)


def _merge_kernel(x_ref, ya_ref, yb_ref, yc_ref, ga_ref, gb_ref, gc_ref, wb_ref, wo_ref, o_ref):
    r_a = ya_ref.shape[1]
    r_b = r_a + yb_ref.shape[1]
    pa = jnp.dot(ya_ref[...], wb_ref[0:r_a, :], preferred_element_type=F32)
    merged = ga_ref[...].astype(F32) * pa
    pb = jnp.dot(yb_ref[...], wb_ref[r_a:r_b, :], preferred_element_type=F32)
    merged += gb_ref[...].astype(F32) * pb
    pc = jnp.dot(yc_ref[...], wb_ref[r_b:, :], preferred_element_type=F32)
    merged += gc_ref[...].astype(F32) * pc
    o_ref[...] = x_ref[...] + jnp.dot(merged.astype(BF16), wo_ref[...], preferred_element_type=F32)


def _merge(x, ya, yb, yc, proj, wb, wo, *, tm):
    n, d = x.shape
    row = lambda i: (i, 0)
    const = lambda i: (0, 0)
    return pl.pallas_call(
        _merge_kernel,
        grid=(n // tm,),
        in_specs=[
            pl.BlockSpec((tm, d), row),
            pl.BlockSpec((tm, ya.shape[1]), row),
            pl.BlockSpec((tm, yb.shape[1]), row),
            pl.BlockSpec((tm, yc.shape[1]), row),
            pl.BlockSpec((tm, d), lambda i: (i, 0)),
            pl.BlockSpec((tm, d), lambda i: (i, 1)),
            pl.BlockSpec((tm, d), lambda i: (i, 2)),
            pl.BlockSpec(wb.shape, const, pipeline_mode=pl.Buffered(1)),
            pl.BlockSpec(wo.shape, const, pipeline_mode=pl.Buffered(1)),
        ],
        out_specs=pl.BlockSpec((tm, d), row),
        out_shape=jax.ShapeDtypeStruct((n, d), F32),
        compiler_params=_params("merge", ("parallel",)),
        name="merge",
    )(x, ya, yb, yc, proj, proj, proj, wb, wo)


def _rope_tables(seq_len, head_dim):
    pos = jnp.arange(seq_len, dtype=F32)
    inv = ROPE_THETA ** (-jnp.arange(0, head_dim, 2, dtype=F32) / head_dim)
    ang = pos[:, None] * inv[None, :]
    cos, sin = jnp.cos(ang), jnp.sin(ang)
    reps = LANES // head_dim
    return (jnp.tile(jnp.concatenate([cos, cos], axis=1), (1, reps)),
            jnp.tile(jnp.concatenate([-sin, sin], axis=1), (1, reps)))


def _block_diag(blocks):
    g, r, c = blocks.shape
    wide = jnp.transpose(blocks, (1, 0, 2)).reshape(r, g * c)
    row_g = jnp.arange(g * r)[:, None] // r
    col_g = jnp.arange(g * c)[None, :] // c
    return jnp.where(row_g == col_g, jnp.tile(wide, (g, 1)), 0.0)


def _s5_tables(lam_re, lam_im, log_dt, b_re, b_im, c_re, c_im, n_steps):
    bmat = jnp.concatenate([_block_diag(jnp.swapaxes(b_re, 1, 2)), _block_diag(jnp.swapaxes(b_im, 1, 2))], axis=1)
    cmats, tabs = [], []
    for d in range(2):
        lr, li = lam_re[d], lam_im[d]
        dt = jnp.exp(log_dt[d])[:, None]
        mag = jnp.exp(dt * lr)
        ar, ai = mag * jnp.cos(dt * li), mag * jnp.sin(dt * li)
        denom = lr * lr + li * li
        nr = ar - 1.0
        kr = (nr * lr + ai * li) / denom
        ki = (ai * lr - nr * li) / denom
        er = c_re * kr[:, None, :] - c_im * ki[:, None, :]
        ei = c_re * ki[:, None, :] + c_im * kr[:, None, :]
        cmats.append(jnp.concatenate([_block_diag(jnp.swapaxes(er, 1, 2)), _block_diag(jnp.swapaxes(-ei, 1, 2))],
                                     axis=0))
        ar, ai = ar.reshape(-1), ai.reshape(-1)
        pr, pi, sr, si, e = jnp.ones_like(ar), jnp.zeros_like(ai), ar, ai, n_steps
        while e:
            if e & 1:
                pr, pi = pr * sr - pi * si, pr * si + pi * sr
            sr, si = sr * sr - si * si, 2.0 * sr * si
            e >>= 1
        tabs.append(jnp.stack([jnp.broadcast_to(v[None, :], (SUBLANES, v.shape[0])) for v in (ar, ai, pr, pi)]))
    return bmat.astype(BF16), jnp.stack(tabs), jnp.stack(cmats).astype(BF16)


def kernel(x, norm_w, ffn_w13, ffn_w2, w_in, s5_lambda_re, s5_lambda_im, s5_log_dt, s5_b_re, s5_b_im, s5_c_re, s5_c_im, s5_d, s5_w_glu, s5_b_glu, diff_lambda, diff_subln, conv_w, w_branch, w_gate, b_gate, w_out, final_norm):
    batch, seq_len, d_model = x.shape
    depth = norm_w.shape[0]
    d_ff = ffn_w2.shape[2]
    d_ssm = s5_d.shape[-1]
    d_conv = conv_w.shape[-1]
    d_attn = (w_in.shape[-1] - d_ssm - 3 * d_conv) // 3
    head_dim = diff_lambda.shape[-1]
    n_heads = d_attn // (2 * head_dim)
    n = batch * seq_len
    assert 2 * head_dim == LANES and d_ssm == d_conv and d_attn == 2 * d_ssm

    tm_ffn = _tile(n, 1024)
    tf = min(2 * LANES, d_ff)
    tm_proj = _tile(seq_len, 1024)
    tn_proj = d_attn
    tm_merge = _tile(n, 512)
    tm_glu = _tile(seq_len, 1024)
    tq = _tile(seq_len, 1024)
    kc_attn = _tile(seq_len, 512)
    t_chunk = _tile(seq_len, 512)

    cos, sin = _rope_tables(seq_len, head_dim)
    xf = x.reshape(n, d_model)
    gate_cols = N_BRANCH * d_model
    u_col = gate_cols // d_ssm
    q_col = (gate_cols + d_ssm) // LANES
    k_col = q_col + d_attn // LANES
    v_col = k_col + d_attn // LANES
    bg_col = (gate_cols + d_ssm + 3 * d_attn) // d_conv
    b_gate3 = b_gate[:, None, :]

    for l in range(depth):
        lambda_init = 0.8 - 0.6 * math.exp(-0.3 * l)
        bmat, tabs, cmat = _s5_tables(s5_lambda_re[l], s5_lambda_im[l], s5_log_dt[l],
                                      s5_b_re[l], s5_b_im[l], s5_c_re[l], s5_c_im[l], t_chunk // SUBLANES)

        w13p, w2p = ffn_w13.astype(BF16), ffn_w2.astype(BF16)
        xf, h_mix = _ffn(xf, norm_w[l, 0][None, :], w13p, w2p, norm_w[l, 1][None, :], layer=l, idx=0,
                         post="next", tm=tm_ffn, tf=tf)

        proj = _proj(h_mix, w_gate, b_gate3, w_in, cos, sin, layer=l, seq_len=seq_len, u_cols=d_ssm,
                     attn_cols=d_attn, q_scale=head_dim ** -0.5 * math.log2(math.e), tm=tm_proj, tn=tn_proj)
        proj3 = proj.reshape(batch, seq_len, proj.shape[1])

        ys = _s5_scan(proj, bmat, tabs, cmat, batch=batch, seq_len=seq_len, u_col=u_col, d_ssm=d_ssm,
                      tc=t_chunk)
        y_a = _s5_glu(ys, proj, s5_d[l][None, :], s5_w_glu[l].astype(BF16), s5_b_glu[l][None, :],
                      u_col=u_col, tm=tm_glu, tc=t_chunk)
        y_b = _attention(proj3, diff_lambda[l], diff_subln[l][None, :], q_col=q_col, k_col=k_col, v_col=v_col,
                         n_heads=n_heads, head_dim=head_dim, lambda_init=lambda_init, tq=tq, kc=kc_attn)
        y_c = _short_conv(proj3, conv_w[l], bg_col=bg_col, d_conv=d_conv)

        xf = _merge(xf, y_a, y_b.reshape(n, d_attn), y_c.reshape(n, d_conv), proj,
                    w_branch[l].astype(BF16), w_out[l].astype(BF16), tm=tm_merge)

        xf = _ffn(xf, norm_w[l, 2][None, :], w13p, w2p, final_norm[None, :], layer=l, idx=1,
                  post="final" if l == depth - 1 else "none", tm=tm_ffn, tf=tf)
    return xf.reshape(batch, seq_len, d_model)
```

```python
import functools
import math

import jax
import jax.numpy as jnp
from jax import lax
from jax.experimental import pallas as pl
from jax.experimental.pallas import tpu as pltpu

NORM_EPS = 1e-6
ROPE_THETA = 10000.0
N_BRANCH = 3
LANES = 128
SUBLANES = 8
MIB = 1024 * 1024
F32 = jnp.float32
BF16 = jnp.bfloat16


def _rms(x, g):
    return x * lax.rsqrt(jnp.mean(x * x, axis=-1, keepdims=True) + NORM_EPS) * g


def _tile(n, want):
    t = min(n, want)
    while n % t:
        t -= 1
    return t


VMEM_LIMIT_MIB = {"ffn": 62, "proj": 56, "s5_scan": 52, "s5_glu": 40, "diff_attn": 52, "short_conv": 40,
                  "merge": 56}


def _params(name, sem):
    return pltpu.CompilerParams(dimension_semantics=sem, vmem_limit_bytes=VMEM_LIMIT_MIB[name] * MIB)


def _ff_tile_start(j, tf, d_ff):
    return jnp.minimum(j * (tf // LANES), (d_ff - tf) // LANES) * LANES


def _ffn_kernel(x_ref, g_ref, wa_ref, wb_ref, w2_ref, fg_ref, o_ref, *rest, post, d_ff):
    if post == "next":
        hn_ref, = rest
        h_ref = hn_ref
    else:
        h_ref, = rest
    j = pl.program_id(1)
    tf = wa_ref.shape[-1]

    @pl.when(j == 0)
    def _():
        h_ref[...] = _rms(x_ref[...], g_ref[...]).astype(BF16)
        o_ref[...] = jnp.zeros_like(o_ref)

    h = h_ref[...]
    a = jnp.dot(h, wa_ref[0, 0].astype(BF16), preferred_element_type=F32)
    b = jnp.dot(h, wb_ref[0, 0].astype(BF16), preferred_element_type=F32)
    col = _ff_tile_start(j, tf, d_ff) + lax.broadcasted_iota(jnp.int32, a.shape, 1)
    act = jnp.where(col >= j * tf, a * jax.nn.sigmoid(a) * b, 0.0).astype(BF16)
    o_ref[...] += jnp.dot(act, w2_ref[0, 0].astype(BF16), preferred_element_type=F32)

    @pl.when(j == pl.num_programs(1) - 1)
    def _():
        y = x_ref[...] + 0.5 * o_ref[...]
        if post == "final":
            y = _rms(y, fg_ref[...])
        if post == "next":
            hn_ref[...] = _rms(y, fg_ref[...]).astype(BF16)
        o_ref[...] = y


def _ffn(x, g, w13, w2, fg, *, layer, idx, post, tm, tf):
    n, d = x.shape
    row_blk = pl.BlockSpec((tm, d), lambda i, j: (i, 0))
    out_specs, out_shape = row_blk, jax.ShapeDtypeStruct((n, d), F32)
    if post == "next":
        out_specs, out_shape = [row_blk, row_blk], [out_shape, jax.ShapeDtypeStruct((n, d), BF16)]
    d_ff = w2.shape[2]
    assert d_ff % LANES == 0 and tf % LANES == 0 and tf <= d_ff
    start = lambda j: _ff_tile_start(j, tf, d_ff)
    elems = lambda *shape: tuple(pl.Element(s) for s in shape)

    return pl.pallas_call(
        functools.partial(_ffn_kernel, post=post, d_ff=d_ff),
        grid=(n // tm, pl.cdiv(d_ff, tf)),
        in_specs=[
            row_blk,
            pl.BlockSpec((1, d), lambda i, j: (0, 0)),
            pl.BlockSpec(elems(1, 1, d, tf), lambda i, j: (layer, idx, 0, start(j))),
            pl.BlockSpec(elems(1, 1, d, tf), lambda i, j: (layer, idx, 0, (d_ff // LANES + start(j) // LANES) * LANES)),
            pl.BlockSpec(elems(1, 1, tf, d), lambda i, j: (layer, idx, start(j), 0)),
            pl.BlockSpec((1, d), lambda i, j: (0, 0)),
        ],
        out_specs=out_specs,
        out_shape=out_shape,
        scratch_shapes=[] if post == "next" else [pltpu.VMEM((tm, d), BF16)],
        compiler_params=_params("ffn", ("parallel", "arbitrary")),
        name="ffn",
    )(x, g, w13, w13, w2, fg)


def _rope(acc, cos, sin, scale):
    lane = lax.broadcasted_iota(jnp.int32, cos.shape, 1)
    first_half = (lane & (LANES // 4)) == 0
    out = []
    for c in range(acc.shape[1] // LANES):
        blk = acc[:, c * LANES:(c + 1) * LANES]
        partner = jnp.where(first_half,
                            pltpu.roll(blk, LANES - LANES // 4, axis=1),
                            pltpu.roll(blk, LANES // 4, axis=1))
        out.append((blk * cos + partner * sin) * scale)
    return jnp.concatenate(out, axis=1)


def _proj_kernel(h_ref, wg_ref, wi_ref, b_ref, cos_ref, sin_ref, o_ref, *,
                 n_gate, n_in, u_cols, attn_cols, q_scale, chunk):
    j = pl.program_id(0)
    tn = o_ref.shape[1]

    def run(w_ref, epilogue):
        for c in range(tn // chunk):
            cols = slice(c * chunk, (c + 1) * chunk)
            acc = jnp.dot(h_ref[...], w_ref[:, cols].astype(BF16), preferred_element_type=F32)
            o_ref[:, cols] = epilogue(acc, c).astype(o_ref.dtype)

    @pl.when(j < n_gate)
    def _():
        run(wg_ref, lambda acc, c: jax.nn.sigmoid(acc + b_ref[:, c * chunk:(c + 1) * chunk]))

    def in_epilogue(blk):
        def epilogue(acc, c):
            col = blk * tn + c * chunk
            if u_cols <= col < u_cols + attn_cols:
                return _rope(acc, cos_ref[...], sin_ref[...], q_scale)
            if u_cols + attn_cols <= col < u_cols + 2 * attn_cols:
                return _rope(acc, cos_ref[...], sin_ref[...], 1.0)
            return acc
        return epilogue

    for blk in range(n_in):
        @pl.when(j == n_gate + blk)
        def _(blk=blk):
            run(wi_ref, in_epilogue(blk))


def _proj(h, w_gate, b_gate, w_in, cos, sin, *, layer, seq_len, u_cols, attn_cols, q_scale, tm, tn):
    n, d = h.shape
    n_gate = w_gate.shape[2] // tn
    n_in = w_in.shape[2] // tn
    chunk = min(u_cols, 2 * LANES)
    assert u_cols % chunk == 0 and attn_cols % chunk == 0 and tn % chunk == 0
    pos_blocks = seq_len // tm
    gate_blk = lambda j, i: (layer, 0, jnp.minimum(j, n_gate - 1))
    return pl.pallas_call(
        functools.partial(_proj_kernel, n_gate=n_gate, n_in=n_in, u_cols=u_cols, attn_cols=attn_cols,
                          q_scale=q_scale, chunk=chunk),
        grid=(n_gate + n_in, n // tm),
        in_specs=[
            pl.BlockSpec((tm, d), lambda j, i: (i, 0)),
            pl.BlockSpec((None, d, tn), gate_blk),
            pl.BlockSpec((None, d, tn), lambda j, i: (layer, 0, jnp.maximum(j - n_gate, 0))),
            pl.BlockSpec((None, 1, tn), gate_blk),
            pl.BlockSpec((tm, LANES), lambda j, i: (i % pos_blocks, 0)),
            pl.BlockSpec((tm, LANES), lambda j, i: (i % pos_blocks, 0)),
        ],
        out_specs=pl.BlockSpec((tm, tn), lambda j, i: (i, j)),
        out_shape=jax.ShapeDtypeStruct((n, (n_gate + n_in) * tn), BF16),
        compiler_params=_params("proj", ("arbitrary", "arbitrary")),
        name="proj",
    )(h, w_gate, w_in, b_gate, cos, sin)


def _scan_chunk(bu_ref, tab_ref, carry_ref, *, reverse, lane_group):
    t_len, s2 = bu_ref.shape
    s_dim = s2 // 2
    n_steps = t_len // SUBLANES
    first, last = (SUBLANES - 1, 0) if reverse else (0, SUBLANES - 1)
    toward_later = SUBLANES - 1 if reverse else 1
    for lg in range(s_dim // lane_group):
        re_cols = pl.ds(lg * lane_group, lane_group)
        im_cols = pl.ds(s_dim + lg * lane_group, lane_group)
        a_re, a_im = tab_ref[0, :, re_cols], tab_ref[1, :, re_cols]
        an_re, an_im = tab_ref[2, :, re_cols], tab_ref[3, :, re_cols]

        def rows_of(it):
            step = (n_steps - 1 - it) if reverse else it
            return pl.ds(pl.multiple_of(step * SUBLANES, SUBLANES), SUBLANES)

        def local_scan(it, state):
            s_re, s_im = state
            rows = rows_of(it)
            s_re, s_im = (a_re * s_re - a_im * s_im + bu_ref[rows, re_cols],
                          a_re * s_im + a_im * s_re + bu_ref[rows, im_cols])
            bu_ref[rows, re_cols] = s_re
            bu_ref[rows, im_cols] = s_im
            return s_re, s_im

        zero = jnp.zeros((SUBLANES, lane_group), F32)
        e_re, e_im = lax.fori_loop(0, n_steps, local_scan, (zero, zero), unroll=8)

        row = lax.broadcasted_iota(jnp.int32, (SUBLANES, lane_group), 0)
        c_re = jnp.where(row == first, carry_ref[0, :, re_cols], 0.0)
        c_im = jnp.where(row == first, carry_ref[1, :, re_cols], 0.0)

        def segment_end(c_re, c_im):
            return e_re + an_re * c_re - an_im * c_im, e_im + an_re * c_im + an_im * c_re

        order = range(SUBLANES - 2, -1, -1) if reverse else range(1, SUBLANES)
        for k in order:
            t_re, t_im = segment_end(c_re, c_im)
            c_re = jnp.where(row == k, pltpu.roll(t_re, toward_later, axis=0), c_re)
            c_im = jnp.where(row == k, pltpu.roll(t_im, toward_later, axis=0), c_im)
        t_re, t_im = segment_end(c_re, c_im)
        carry_ref[0, :, re_cols] = jnp.broadcast_to(t_re[last:last + 1], t_re.shape)
        carry_ref[1, :, re_cols] = jnp.broadcast_to(t_im[last:last + 1], t_im.shape)

        def add_incoming(it, f):
            f_re, f_im = f
            rows = rows_of(it)
            f_re, f_im = a_re * f_re - a_im * f_im, a_re * f_im + a_im * f_re
            bu_ref[rows, re_cols] += f_re
            bu_ref[rows, im_cols] += f_im
            return f_re, f_im

        lax.fori_loop(0, n_steps, add_incoming, (c_re, c_im), unroll=8)


def _s5_kernel(u_ref, perm_ref, bmat_ref, tab_ref, cmat_ref, y_ref, bu_ref, carry_ref, *, lane_group):
    d = pl.program_id(1)
    c = pl.program_id(2)

    @pl.when(c == 0)
    def _():
        carry_ref[...] = jnp.zeros_like(carry_ref)

    u_perm = jnp.dot(perm_ref[...], u_ref[...], preferred_element_type=F32).astype(BF16)

    d_ssm, s2 = bmat_ref.shape
    s_dim = s2 // 2
    halves = [(slice(h * d_ssm // 2, (h + 1) * d_ssm // 2), slice(h * s_dim // 2, (h + 1) * s_dim // 2),
               slice(s_dim + h * s_dim // 2, s_dim + (h + 1) * s_dim // 2)) for h in range(2)]
    for ch, st_re, st_im in halves:
        u_half = u_perm[:, ch]
        bu_ref[:, st_re] = jnp.dot(u_half, bmat_ref[ch, st_re], preferred_element_type=F32)
        bu_ref[:, st_im] = jnp.dot(u_half, bmat_ref[ch, st_im], preferred_element_type=F32)

    @pl.when(d == 0)
    def _():
        _scan_chunk(bu_ref, tab_ref.at[0], carry_ref, reverse=False, lane_group=lane_group)

    @pl.when(d == 1)
    def _():
        _scan_chunk(bu_ref, tab_ref.at[0], carry_ref, reverse=True, lane_group=lane_group)

    for ch, st_re, st_im in halves:
        y_ref[0, :, ch] = (
            jnp.dot(bu_ref[:, st_re].astype(BF16), cmat_ref[0, st_re, ch], preferred_element_type=F32)
            + jnp.dot(bu_ref[:, st_im].astype(BF16), cmat_ref[0, st_im, ch], preferred_element_type=F32))


def _segment_interleave(tc):
    dst = jnp.arange(tc)
    src = (dst % SUBLANES) * (tc // SUBLANES) + dst // SUBLANES
    return (src[:, None] == jnp.arange(tc)[None, :]).astype(BF16)


def _s5_scan(proj, bmat, tabs, cmat, *, batch, seq_len, u_col, d_ssm, tc):
    n = proj.shape[0]
    n_chunks = seq_len // tc
    s2 = bmat.shape[1]
    s_dim = s2 // 2
    lane_group = min(s_dim, 8 * LANES)

    def chunk_row(b, d, c):
        return b * n_chunks + jnp.where(d == 0, c, n_chunks - 1 - c)

    return pl.pallas_call(
        functools.partial(_s5_kernel, lane_group=lane_group),
        grid=(batch, 2, n_chunks),
        in_specs=[
            pl.BlockSpec((tc, d_ssm), lambda b, d, c: (chunk_row(b, d, c), u_col)),
            pl.BlockSpec((tc, tc), lambda b, d, c: (0, 0)),
            pl.BlockSpec((d_ssm, s2), lambda b, d, c: (0, 0)),
            pl.BlockSpec((1, 4, SUBLANES, s_dim), lambda b, d, c: (d, 0, 0, 0)),
            pl.BlockSpec((1, s2, d_ssm), lambda b, d, c: (d, 0, 0)),
        ],
        out_specs=pl.BlockSpec((1, tc, d_ssm), lambda b, d, c: (d, chunk_row(b, d, c), 0)),
        out_shape=jax.ShapeDtypeStruct((2, n, d_ssm), F32),
        scratch_shapes=[pltpu.VMEM((tc, s2), F32), pltpu.VMEM((2, SUBLANES, s_dim), F32)],
        compiler_params=_params("s5_scan", ("parallel", "arbitrary", "arbitrary")),
        name="s5_scan",
    )(proj, _segment_interleave(tc), bmat, tabs, cmat)


def _glu_kernel(ys_ref, u_ref, dskip_ref, w_ref, b_ref, o_ref, y_ref, *, tc):
    n_lane_tiles, tm, _ = y_ref.shape
    y_sum = ys_ref[0] + ys_ref[1]
    for j in range(n_lane_tiles):
        y_ref[j] = y_sum[:, j * LANES:(j + 1) * LANES]
    n_steps = tc // SUBLANES
    pieces = [jnp.concatenate([y_ref[j, pl.ds(base + g, n_steps, stride=SUBLANES), :]
                               for j in range(n_lane_tiles)], axis=1)
              for base in range(0, tm, tc) for g in range(SUBLANES)]
    y = jnp.concatenate(pieces, axis=0) + dskip_ref[...] * u_ref[...].astype(F32)
    y = jax.nn.gelu(y)
    z = jnp.dot(y.astype(BF16), w_ref[...], preferred_element_type=F32) + b_ref[...]
    o_ref[...] = (y * jax.nn.sigmoid(z)).astype(o_ref.dtype)


def _s5_glu(ys, proj, dskip, w, b, *, u_col, tm, tc):
    _, n, d_ssm = ys.shape
    assert tm % tc == 0
    return pl.pallas_call(
        functools.partial(_glu_kernel, tc=tc),
        scratch_shapes=[pltpu.VMEM((d_ssm // min(d_ssm, LANES), tm, min(d_ssm, LANES)), F32)],
        grid=(n // tm,),
        in_specs=[
            pl.BlockSpec((2, tm, d_ssm), lambda i: (0, i, 0)),
            pl.BlockSpec((tm, d_ssm), lambda i: (i, u_col)),
            pl.BlockSpec((1, d_ssm), lambda i: (0, 0)),
            pl.BlockSpec((d_ssm, d_ssm), lambda i: (0, 0)),
            pl.BlockSpec((1, d_ssm), lambda i: (0, 0)),
        ],
        out_specs=pl.BlockSpec((tm, d_ssm), lambda i: (i, 0)),
        out_shape=jax.ShapeDtypeStruct((n, d_ssm), BF16),
        compiler_params=_params("s5_glu", ("parallel",)),
        name="s5_glu",
    )(ys, proj, dskip, w, b)


def _attn_kernel(q_ref, k_ref, v_ref, lam_ref, sg_ref, o_ref, *, lambda_init, head_dim, kc):
    q = q_ref[0]
    tq = q.shape[0]
    lane = lax.broadcasted_iota(jnp.int32, q.shape, 1)
    zero = jnp.zeros_like(q)
    qq = jnp.concatenate([jnp.where(lane < head_dim, q, zero), jnp.where(lane >= head_dim, q, zero)], axis=0)
    n_chunks = k_ref.shape[1] // kc
    sub = 2 * SUBLANES

    def scores(c):
        return lax.dot_general(k_ref[0, pl.ds(c * kc, kc), :], qq, (((1,), (1,)), ((), ())),
                               preferred_element_type=F32)

    def exp_stage(s, m_new):
        pieces, part = [], None
        for r in range(0, kc, sub):
            e_r = jnp.exp2(s[r:r + sub] - m_new)
            pieces.append(e_r.astype(BF16))
            for g in range(sub // SUBLANES):
                tile = e_r[g * SUBLANES:(g + 1) * SUBLANES]
                part = tile if part is None else part + tile
        return jnp.concatenate(pieces, axis=0), jnp.sum(part, axis=0, keepdims=True)

    def value_stage(acc, item):
        c, e_b, alpha = item
        pv = lax.dot_general(v_ref[0, pl.ds(c * kc, kc), :], e_b, (((0,), (0,)), ((), ())),
                             preferred_element_type=F32)
        return pv if acc is None else acc * alpha + pv

    score_lead = 2
    m = l = acc = None
    s_q = [scores(c) for c in range(min(score_lead, n_chunks))]
    pending = None
    for c in range(n_chunks):
        if c + score_lead < n_chunks:
            s_q.append(scores(c + score_lead))
        s = s_q[c]
        m_c = jnp.max(s, axis=0, keepdims=True)
        m_new = m_c if m is None else jnp.maximum(m, m_c)
        e_b, l_c = exp_stage(s, m_new)
        alpha = None if m is None else jnp.exp2(m - m_new)
        l = l_c if m is None else l * alpha + l_c
        if pending is not None:
            acc = value_stage(acc, pending)
        pending = (c, e_b, alpha)
        m = m_new
    acc = value_stage(acc, pending)
    lv = lam_ref[...]
    lam = (jnp.exp(jnp.sum(lv[0:1] * lv[1:2], axis=-1, keepdims=True))
           - jnp.exp(jnp.sum(lv[2:3] * lv[3:4], axis=-1, keepdims=True)) + lambda_init)
    o_t = acc[:, 0:tq] * (1.0 / l[:, 0:tq]) - acc[:, tq:2 * tq] * (lam / l[:, tq:2 * tq])
    o_ref[0] = (_rms(o_t.T, sg_ref[...]) * (1.0 - lambda_init)).astype(o_ref.dtype)


def _attention(proj3, lam_vec, subln, *, q_col, k_col, v_col, n_heads, head_dim, lambda_init, tq, kc):
    batch, seq_len, _ = proj3.shape
    vd = 2 * head_dim
    return pl.pallas_call(
        functools.partial(_attn_kernel, lambda_init=lambda_init, head_dim=head_dim, kc=kc),
        grid=(batch, n_heads, seq_len // tq),
        in_specs=[
            pl.BlockSpec((1, tq, vd), lambda b, h, i: (b, i, q_col + h)),
            pl.BlockSpec((1, seq_len, vd), lambda b, h, i: (b, 0, k_col + h)),
            pl.BlockSpec((1, seq_len, vd), lambda b, h, i: (b, 0, v_col + h)),
            pl.BlockSpec((4, head_dim), lambda b, h, i: (0, 0)),
            pl.BlockSpec((1, vd), lambda b, h, i: (0, 0)),
        ],
        out_specs=pl.BlockSpec((1, tq, vd), lambda b, h, i: (b, i, h)),
        out_shape=jax.ShapeDtypeStruct((batch, seq_len, n_heads * vd), BF16),
        compiler_params=_params("diff_attn", ("parallel", "parallel", "arbitrary")),
        name="diff_attn",
    )(proj3, proj3, proj3, lam_vec, subln)


def _conv_kernel(bg_ref, cg_ref, xv_ref, w_ref, o_ref):
    z = cg_ref[0].astype(F32) * xv_ref[0].astype(F32)
    seq_len = z.shape[0]
    row = lax.broadcasted_iota(jnp.int32, z.shape, 0)
    z_prev = jnp.where(row == 0, 0.0, pltpu.roll(z, 1, axis=0))
    z_next = jnp.where(row == seq_len - 1, 0.0, pltpu.roll(z, seq_len - 1, axis=0))
    w = w_ref[...]
    zc = w[0:1] * z_prev + w[1:2] * z + w[2:3] * z_next
    o_ref[0] = (bg_ref[0].astype(F32) * zc).astype(o_ref.dtype)


def _short_conv(proj3, conv_w, *, bg_col, d_conv):
    batch, seq_len, _ = proj3.shape
    tc = min(d_conv, LANES)
    per = d_conv // tc
    return pl.pallas_call(
        _conv_kernel,
        grid=(batch, per),
        in_specs=[
            pl.BlockSpec((1, seq_len, tc), lambda b, j: (b, 0, bg_col * per + j)),
            pl.BlockSpec((1, seq_len, tc), lambda b, j: (b, 0, (bg_col + 1) * per + j)),
            pl.BlockSpec((1, seq_len, tc), lambda b, j: (b, 0, (bg_col + 2) * per + j)),
            pl.BlockSpec((conv_w.shape[0], tc), lambda b, j: (0, j)),
        ],
        out_specs=pl.BlockSpec((1, seq_len, tc), lambda b, j: (b, 0, j)),
        out_shape=jax.ShapeDtypeStruct((batch, seq_len, d_conv), BF16),
        compiler_params=_params("short_conv", ("parallel", "parallel")),
        name="short_conv",
    )(proj3, proj3, proj3, conv_w)


def _merge_kernel(x_ref, ya_ref, yb_ref, yc_ref, ga_ref, gb_ref, gc_ref, wb_ref, wo_ref, o_ref):
    r_a = ya_ref.shape[1]
    r_b = r_a + yb_ref.shape[1]
    pa = jnp.dot(ya_ref[...], wb_ref[0:r_a, :], preferred_element_type=F32)
    merged = ga_ref[...].astype(F32) * pa
    pb = jnp.dot(yb_ref[...], wb_ref[r_a:r_b, :], preferred_element_type=F32)
    merged += gb_ref[...].astype(F32) * pb
    pc = jnp.dot(yc_ref[...], wb_ref[r_b:, :], preferred_element_type=F32)
    merged += gc_ref[...].astype(F32) * pc
    o_ref[...] = x_ref[...] + jnp.dot(merged.astype(BF16), wo_ref[...], preferred_element_type=F32)


def _merge(x, ya, yb, yc, proj, wb, wo, *, tm):
    n, d = x.shape
    row = lambda i: (i, 0)
    const = lambda i: (0, 0)
    return pl.pallas_call(
        _merge_kernel,
        grid=(n // tm,),
        in_specs=[
            pl.BlockSpec((tm, d), row),
            pl.BlockSpec((tm, ya.shape[1]), row),
            pl.BlockSpec((tm, yb.shape[1]), row),
            pl.BlockSpec((tm, yc.shape[1]), row),
            pl.BlockSpec((tm, d), lambda i: (i, 0)),
            pl.BlockSpec((tm, d), lambda i: (i, 1)),
            pl.BlockSpec((tm, d), lambda i: (i, 2)),
            pl.BlockSpec(wb.shape, const, pipeline_mode=pl.Buffered(1)),
            pl.BlockSpec(wo.shape, const, pipeline_mode=pl.Buffered(1)),
        ],
        out_specs=pl.BlockSpec((tm, d), row),
        out_shape=jax.ShapeDtypeStruct((n, d), F32),
        compiler_params=_params("merge", ("parallel",)),
        name="merge",
    )(x, ya, yb, yc, proj, proj, proj, wb, wo)


def _rope_tables(seq_len, head_dim):
    pos = jnp.arange(seq_len, dtype=F32)
    inv = ROPE_THETA ** (-jnp.arange(0, head_dim, 2, dtype=F32) / head_dim)
    ang = pos[:, None] * inv[None, :]
    cos, sin = jnp.cos(ang), jnp.sin(ang)
    reps = LANES // head_dim
    return (jnp.tile(jnp.concatenate([cos, cos], axis=1), (1, reps)),
            jnp.tile(jnp.concatenate([-sin, sin], axis=1), (1, reps)))


def _block_diag(blocks):
    g, r, c = blocks.shape
    wide = jnp.transpose(blocks, (1, 0, 2)).reshape(r, g * c)
    row_g = jnp.arange(g * r)[:, None] // r
    col_g = jnp.arange(g * c)[None, :] // c
    return jnp.where(row_g == col_g, jnp.tile(wide, (g, 1)), 0.0)


def _s5_tables(lam_re, lam_im, log_dt, b_re, b_im, c_re, c_im, n_steps):
    bmat = jnp.concatenate([_block_diag(jnp.swapaxes(b_re, 1, 2)), _block_diag(jnp.swapaxes(b_im, 1, 2))], axis=1)
    cmats, tabs = [], []
    for d in range(2):
        lr, li = lam_re[d], lam_im[d]
        dt = jnp.exp(log_dt[d])[:, None]
        mag = jnp.exp(dt * lr)
        ar, ai = mag * jnp.cos(dt * li), mag * jnp.sin(dt * li)
        denom = lr * lr + li * li
        nr = ar - 1.0
        kr = (nr * lr + ai * li) / denom
        ki = (ai * lr - nr * li) / denom
        er = c_re * kr[:, None, :] - c_im * ki[:, None, :]
        ei = c_re * ki[:, None, :] + c_im * kr[:, None, :]
        cmats.append(jnp.concatenate([_block_diag(jnp.swapaxes(er, 1, 2)), _block_diag(jnp.swapaxes(-ei, 1, 2))],
                                     axis=0))
        ar, ai = ar.reshape(-1), ai.reshape(-1)
        pr, pi, sr, si, e = jnp.ones_like(ar), jnp.zeros_like(ai), ar, ai, n_steps
        while e:
            if e & 1:
                pr, pi = pr * sr - pi * si, pr * si + pi * sr
            sr, si = sr * sr - si * si, 2.0 * sr * si
            e >>= 1
        tabs.append(jnp.stack([jnp.broadcast_to(v[None, :], (SUBLANES, v.shape[0])) for v in (ar, ai, pr, pi)]))
    return bmat.astype(BF16), jnp.stack(tabs), jnp.stack(cmats).astype(BF16)


def kernel(x, norm_w, ffn_w13, ffn_w2, w_in, s5_lambda_re, s5_lambda_im, s5_log_dt, s5_b_re, s5_b_im, s5_c_re, s5_c_im, s5_d, s5_w_glu, s5_b_glu, diff_lambda, diff_subln, conv_w, w_branch, w_gate, b_gate, w_out, final_norm):
    batch, seq_len, d_model = x.shape
    depth = norm_w.shape[0]
    d_ff = ffn_w2.shape[2]
    d_ssm = s5_d.shape[-1]
    d_conv = conv_w.shape[-1]
    d_attn = (w_in.shape[-1] - d_ssm - 3 * d_conv) // 3
    head_dim = diff_lambda.shape[-1]
    n_heads = d_attn // (2 * head_dim)
    n = batch * seq_len
    assert 2 * head_dim == LANES and d_ssm == d_conv and d_attn == 2 * d_ssm

    tm_ffn = _tile(n, 1024)
    tf = min(2 * LANES, d_ff)
    tm_proj = _tile(seq_len, 1024)
    tn_proj = d_attn
    tm_merge = _tile(n, 512)
    tm_glu = _tile(seq_len, 1024)
    tq = _tile(seq_len, 1024)
    kc_attn = _tile(seq_len, 512)
    t_chunk = _tile(seq_len, 512)

    cos, sin = _rope_tables(seq_len, head_dim)
    xf = x.reshape(n, d_model)
    gate_cols = N_BRANCH * d_model
    u_col = gate_cols // d_ssm
    q_col = (gate_cols + d_ssm) // LANES
    k_col = q_col + d_attn // LANES
    v_col = k_col + d_attn // LANES
    bg_col = (gate_cols + d_ssm + 3 * d_attn) // d_conv
    b_gate3 = b_gate[:, None, :]

    for l in range(depth):
        lambda_init = 0.8 - 0.6 * math.exp(-0.3 * l)
        bmat, tabs, cmat = _s5_tables(s5_lambda_re[l], s5_lambda_im[l], s5_log_dt[l],
                                      s5_b_re[l], s5_b_im[l], s5_c_re[l], s5_c_im[l], t_chunk // SUBLANES)

        xf, h_mix = _ffn(xf, norm_w[l, 0][None, :], ffn_w13, ffn_w2, norm_w[l, 1][None, :], layer=l, idx=0,
                         post="next", tm=tm_ffn, tf=tf)

        proj = _proj(h_mix, w_gate, b_gate3, w_in, cos, sin, layer=l, seq_len=seq_len, u_cols=d_ssm,
                     attn_cols=d_attn, q_scale=head_dim ** -0.5 * math.log2(math.e), tm=tm_proj, tn=tn_proj)
        proj3 = proj.reshape(batch, seq_len, proj.shape[1])

        ys = _s5_scan(proj, bmat, tabs, cmat, batch=batch, seq_len=seq_len, u_col=u_col, d_ssm=d_ssm,
                      tc=t_chunk)
        y_a = _s5_glu(ys, proj, s5_d[l][None, :], s5_w_glu[l].astype(BF16), s5_b_glu[l][None, :],
                      u_col=u_col, tm=tm_glu, tc=t_chunk)
        y_b = _attention(proj3, diff_lambda[l], diff_subln[l][None, :], q_col=q_col, k_col=k_col, v_col=v_col,
                         n_heads=n_heads, head_dim=head_dim, lambda_init=lambda_init, tq=tq, kc=kc_attn)
        y_c = _short_conv(proj3, conv_w[l], bg_col=bg_col, d_conv=d_conv)

        xf = _merge(xf, y_a, y_b.reshape(n, d_attn), y_c.reshape(n, d_conv), proj,
                    w_branch[l].astype(BF16), w_out[l].astype(BF16), tm=tm_merge)

        xf = _ffn(xf, norm_w[l, 2][None, :], ffn_w13, ffn_w2, final_norm[None, :], layer=l, idx=1,
                  post="final" if l == depth - 1 else "none", tm=tm_ffn, tf=tf)
    return xf.reshape(batch, seq_len, d_model)
```

```python
import functools
import math

import jax
import jax.numpy as jnp
from jax import lax
from jax.experimental import pallas as pl
from jax.experimental.pallas import tpu as pltpu

NORM_EPS = 1e-6
ROPE_THETA = 10000.0
N_BRANCH = 3
LANES = 128
SUBLANES = 8
MIB = 1024 * 1024
F32 = jnp.float32
BF16 = jnp.bfloat16


def _rms(x, g):
    return x * lax.rsqrt(jnp.mean(x * x, axis=-1, keepdims=True) + NORM_EPS) * g


def _tile(n, want):
    t = min(n, want)
    while n % t:
        t -= 1
    return t


VMEM_LIMIT_MIB = {"ffn": 62, "proj": 56, "s5_scan": 52, "s5_glu": 40, "diff_attn": 52, "short_conv": 40,
                  "merge": 56}


def _params(name, sem):
    return pltpu.CompilerParams(dimension_semantics=sem, vmem_limit_bytes=VMEM_LIMIT_MIB[name] * MIB)


def _ff_tile_start(j, tf, d_ff):
    return jnp.minimum(j * (tf // LANES), (d_ff - tf) // LANES) * LANES


def _ffn_kernel(x_ref, g_ref, wa_ref, wb_ref, w2_ref, fg_ref, o_ref, *rest, post, d_ff):
    if post == "next":
        hn_ref, = rest
        h_ref = hn_ref
    else:
        h_ref, = rest
    j = pl.program_id(1)
    tf = wa_ref.shape[-1]

    @pl.when(j == 0)
    def _():
        h_ref[...] = _rms(x_ref[...], g_ref[...]).astype(BF16)
        o_ref[...] = jnp.zeros_like(o_ref)

    h = h_ref[...]
    a = jnp.dot(h, wa_ref[0, 0].astype(BF16), preferred_element_type=F32)
    b = jnp.dot(h, wb_ref[0, 0].astype(BF16), preferred_element_type=F32)
    col = _ff_tile_start(j, tf, d_ff) + lax.broadcasted_iota(jnp.int32, a.shape, 1)
    act = jnp.where(col >= j * tf, a * jax.nn.sigmoid(a) * b, 0.0).astype(BF16)
    o_ref[...] += jnp.dot(act, w2_ref[0, 0].astype(BF16), preferred_element_type=F32)

    @pl.when(j == pl.num_programs(1) - 1)
    def _():
        y = x_ref[...] + 0.5 * o_ref[...]
        if post == "final":
            y = _rms(y, fg_ref[...])
        if post == "next":
            hn_ref[...] = _rms(y, fg_ref[...]).astype(BF16)
        o_ref[...] = y


def _ffn(x, g, w13, w2, fg, *, layer, idx, post, tm, tf):
    n, d = x.shape
    row_blk = pl.BlockSpec((tm, d), lambda i, j: (i, 0))
    out_specs, out_shape = row_blk, jax.ShapeDtypeStruct((n, d), F32)
    if post == "next":
        out_specs, out_shape = [row_blk, row_blk], [out_shape, jax.ShapeDtypeStruct((n, d), BF16)]
    d_ff = w2.shape[2]
    assert d_ff % LANES == 0 and tf % LANES == 0 and tf <= d_ff
    start = lambda j: _ff_tile_start(j, tf, d_ff)
    elems = lambda *shape: tuple(pl.Element(s) for s in shape)

    return pl.pallas_call(
        functools.partial(_ffn_kernel, post=post, d_ff=d_ff),
        grid=(n // tm, pl.cdiv(d_ff, tf)),
        in_specs=[
            row_blk,
            pl.BlockSpec((1, d), lambda i, j: (0, 0)),
            pl.BlockSpec(elems(1, 1, d, tf), lambda i, j: (layer, idx, 0, start(j))),
            pl.BlockSpec(elems(1, 1, d, tf), lambda i, j: (layer, idx, 0, (d_ff // LANES + start(j) // LANES) * LANES)),
            pl.BlockSpec(elems(1, 1, tf, d), lambda i, j: (layer, idx, start(j), 0)),
            pl.BlockSpec((1, d), lambda i, j: (0, 0)),
        ],
        out_specs=out_specs,
        out_shape=out_shape,
        scratch_shapes=[] if post == "next" else [pltpu.VMEM((tm, d), BF16)],
        compiler_params=_params("ffn", ("parallel", "arbitrary")),
        name="ffn",
    )(x, g, w13, w13, w2, fg)


def _rope(acc, cos, sin, scale):
    lane = lax.broadcasted_iota(jnp.int32, cos.shape, 1)
    first_half = (lane & (LANES // 4)) == 0
    out = []
    for c in range(acc.shape[1] // LANES):
        blk = acc[:, c * LANES:(c + 1) * LANES]
        partner = jnp.where(first_half,
                            pltpu.roll(blk, LANES - LANES // 4, axis=1),
                            pltpu.roll(blk, LANES // 4, axis=1))
        out.append((blk * cos + partner * sin) * scale)
    return jnp.concatenate(out, axis=1)


def _proj_kernel(h_ref, wg_ref, wi_ref, b_ref, cos_ref, sin_ref, o_ref, *,
                 n_gate, n_in, u_cols, attn_cols, q_scale, chunk):
    j = pl.program_id(0)
    tn = o_ref.shape[1]

    def run(w_ref, epilogue):
        for c in range(tn // chunk):
            cols = slice(c * chunk, (c + 1) * chunk)
            acc = jnp.dot(h_ref[...], w_ref[:, cols].astype(BF16), preferred_element_type=F32)
            o_ref[:, cols] = epilogue(acc, c).astype(o_ref.dtype)

    @pl.when(j < n_gate)
    def _():
        run(wg_ref, lambda acc, c: jax.nn.sigmoid(acc + b_ref[:, c * chunk:(c + 1) * chunk]))

    def in_epilogue(blk):
        def epilogue(acc, c):
            col = blk * tn + c * chunk
            if u_cols <= col < u_cols + attn_cols:
                return _rope(acc, cos_ref[...], sin_ref[...], q_scale)
            if u_cols + attn_cols <= col < u_cols + 2 * attn_cols:
                return _rope(acc, cos_ref[...], sin_ref[...], 1.0)
            return acc
        return epilogue

    for blk in range(n_in):
        @pl.when(j == n_gate + blk)
        def _(blk=blk):
            run(wi_ref, in_epilogue(blk))


def _proj(h, w_gate, b_gate, w_in, cos, sin, *, layer, seq_len, u_cols, attn_cols, q_scale, tm, tn):
    n, d = h.shape
    n_gate = w_gate.shape[2] // tn
    n_in = w_in.shape[2] // tn
    chunk = min(u_cols, 2 * LANES)
    assert u_cols % chunk == 0 and attn_cols % chunk == 0 and tn % chunk == 0
    pos_blocks = seq_len // tm
    gate_blk = lambda j, i: (layer, 0, jnp.minimum(j, n_gate - 1))
    return pl.pallas_call(
        functools.partial(_proj_kernel, n_gate=n_gate, n_in=n_in, u_cols=u_cols, attn_cols=attn_cols,
                          q_scale=q_scale, chunk=chunk),
        grid=(n_gate + n_in, n // tm),
        in_specs=[
            pl.BlockSpec((tm, d), lambda j, i: (i, 0)),
            pl.BlockSpec((None, d, tn), gate_blk),
            pl.BlockSpec((None, d, tn), lambda j, i: (layer, 0, jnp.maximum(j - n_gate, 0))),
            pl.BlockSpec((None, 1, tn), gate_blk),
            pl.BlockSpec((tm, LANES), lambda j, i: (i % pos_blocks, 0)),
            pl.BlockSpec((tm, LANES), lambda j, i: (i % pos_blocks, 0)),
        ],
        out_specs=pl.BlockSpec((tm, tn), lambda j, i: (i, j)),
        out_shape=jax.ShapeDtypeStruct((n, (n_gate + n_in) * tn), BF16),
        compiler_params=_params("proj", ("arbitrary", "arbitrary")),
        name="proj",
    )(h, w_gate, w_in, b_gate, cos, sin)


def _scan_chunk(bu_ref, tab_ref, carry_ref, *, reverse, lane_group):
    t_len, s2 = bu_ref.shape
    s_dim = s2 // 2
    n_steps = t_len // SUBLANES
    first, last = (SUBLANES - 1, 0) if reverse else (0, SUBLANES - 1)
    toward_later = SUBLANES - 1 if reverse else 1
    for lg in range(s_dim // lane_group):
        re_cols = pl.ds(lg * lane_group, lane_group)
        im_cols = pl.ds(s_dim + lg * lane_group, lane_group)
        a_re, a_im = tab_ref[0, :, re_cols], tab_ref[1, :, re_cols]
        an_re, an_im = tab_ref[2, :, re_cols], tab_ref[3, :, re_cols]

        def rows_of(it):
            step = (n_steps - 1 - it) if reverse else it
            return pl.ds(pl.multiple_of(step * SUBLANES, SUBLANES), SUBLANES)

        def local_scan(it, state):
            s_re, s_im = state
            rows = rows_of(it)
            s_re, s_im = (a_re * s_re - a_im * s_im + bu_ref[rows, re_cols],
                          a_re * s_im + a_im * s_re + bu_ref[rows, im_cols])
            bu_ref[rows, re_cols] = s_re
            bu_ref[rows, im_cols] = s_im
            return s_re, s_im

        zero = jnp.zeros((SUBLANES, lane_group), F32)
        e_re, e_im = lax.fori_loop(0, n_steps, local_scan, (zero, zero), unroll=8)

        row = lax.broadcasted_iota(jnp.int32, (SUBLANES, lane_group), 0)
        c_re = jnp.where(row == first, carry_ref[0, :, re_cols], 0.0)
        c_im = jnp.where(row == first, carry_ref[1, :, re_cols], 0.0)

        def segment_end(c_re, c_im):
            return e_re + an_re * c_re - an_im * c_im, e_im + an_re * c_im + an_im * c_re

        order = range(SUBLANES - 2, -1, -1) if reverse else range(1, SUBLANES)
        for k in order:
            t_re, t_im = segment_end(c_re, c_im)
            c_re = jnp.where(row == k, pltpu.roll(t_re, toward_later, axis=0), c_re)
            c_im = jnp.where(row == k, pltpu.roll(t_im, toward_later, axis=0), c_im)
        t_re, t_im = segment_end(c_re, c_im)
        carry_ref[0, :, re_cols] = jnp.broadcast_to(t_re[last:last + 1], t_re.shape)
        carry_ref[1, :, re_cols] = jnp.broadcast_to(t_im[last:last + 1], t_im.shape)

        def add_incoming(it, f):
            f_re, f_im = f
            rows = rows_of(it)
            f_re, f_im = a_re * f_re - a_im * f_im, a_re * f_im + a_im * f_re
            bu_ref[rows, re_cols] += f_re
            bu_ref[rows, im_cols] += f_im
            return f_re, f_im

        lax.fori_loop(0, n_steps, add_incoming, (c_re, c_im), unroll=8)


def _s5_kernel(u_ref, perm_ref, bmat_ref, tab_ref, cmat_ref, y_ref, bu_ref, carry_ref, *, lane_group):
    d = pl.program_id(1)
    c = pl.program_id(2)

    @pl.when(c == 0)
    def _():
        carry_ref[...] = jnp.zeros_like(carry_ref)

    u_perm = jnp.dot(perm_ref[...], u_ref[...], preferred_element_type=F32).astype(BF16)

    d_ssm, s2 = bmat_ref.shape
    s_dim = s2 // 2
    halves = [(slice(h * d_ssm // 2, (h + 1) * d_ssm // 2), slice(h * s_dim // 2, (h + 1) * s_dim // 2),
               slice(s_dim + h * s_dim // 2, s_dim + (h + 1) * s_dim // 2)) for h in range(2)]
    for ch, st_re, st_im in halves:
        u_half = u_perm[:, ch]
        bu_ref[:, st_re] = jnp.dot(u_half, bmat_ref[ch, st_re], preferred_element_type=F32)
        bu_ref[:, st_im] = jnp.dot(u_half, bmat_ref[ch, st_im], preferred_element_type=F32)

    @pl.when(d == 0)
    def _():
        _scan_chunk(bu_ref, tab_ref.at[0], carry_ref, reverse=False, lane_group=lane_group)

    @pl.when(d == 1)
    def _():
        _scan_chunk(bu_ref, tab_ref.at[0], carry_ref, reverse=True, lane_group=lane_group)

    for ch, st_re, st_im in halves:
        y_ref[0, :, ch] = (
            jnp.dot(bu_ref[:, st_re].astype(BF16), cmat_ref[0, st_re, ch], preferred_element_type=F32)
            + jnp.dot(bu_ref[:, st_im].astype(BF16), cmat_ref[0, st_im, ch], preferred_element_type=F32))


def _segment_interleave(tc):
    dst = jnp.arange(tc)
    src = (dst % SUBLANES) * (tc // SUBLANES) + dst // SUBLANES
    return (src[:, None] == jnp.arange(tc)[None, :]).astype(BF16)


def _s5_scan(proj, bmat, tabs, cmat, *, layer, batch, seq_len, u_col, d_ssm, tc):
    n = proj.shape[0]
    n_chunks = seq_len // tc
    s2 = bmat.shape[2]
    s_dim = s2 // 2
    lane_group = min(s_dim, 8 * LANES)

    def chunk_row(b, d, c):
        return b * n_chunks + jnp.where(d == 0, c, n_chunks - 1 - c)

    return pl.pallas_call(
        functools.partial(_s5_kernel, lane_group=lane_group),
        grid=(batch, 2, n_chunks),
        in_specs=[
            pl.BlockSpec((tc, d_ssm), lambda b, d, c: (chunk_row(b, d, c), u_col)),
            pl.BlockSpec((tc, tc), lambda b, d, c: (0, 0)),
            pl.BlockSpec((None, d_ssm, s2), lambda b, d, c: (layer, 0, 0)),
            pl.BlockSpec((None, 1, 4, SUBLANES, s_dim), lambda b, d, c: (layer, d, 0, 0, 0)),
            pl.BlockSpec((None, 1, s2, d_ssm), lambda b, d, c: (layer, d, 0, 0)),
        ],
        out_specs=pl.BlockSpec((1, tc, d_ssm), lambda b, d, c: (d, chunk_row(b, d, c), 0)),
        out_shape=jax.ShapeDtypeStruct((2, n, d_ssm), F32),
        scratch_shapes=[pltpu.VMEM((tc, s2), F32), pltpu.VMEM((2, SUBLANES, s_dim), F32)],
        compiler_params=_params("s5_scan", ("parallel", "arbitrary", "arbitrary")),
        name="s5_scan",
    )(proj, _segment_interleave(tc), bmat, tabs, cmat)


def _glu_kernel(ys_ref, u_ref, dskip_ref, w_ref, b_ref, o_ref, y_ref, *, tc):
    n_lane_tiles, tm, _ = y_ref.shape
    y_sum = ys_ref[0] + ys_ref[1]
    for j in range(n_lane_tiles):
        y_ref[j] = y_sum[:, j * LANES:(j + 1) * LANES]
    n_steps = tc // SUBLANES
    pieces = [jnp.concatenate([y_ref[j, pl.ds(base + g, n_steps, stride=SUBLANES), :]
                               for j in range(n_lane_tiles)], axis=1)
              for base in range(0, tm, tc) for g in range(SUBLANES)]
    y = jnp.concatenate(pieces, axis=0) + dskip_ref[...] * u_ref[...].astype(F32)
    y = jax.nn.gelu(y)
    z = jnp.dot(y.astype(BF16), w_ref[...], preferred_element_type=F32) + b_ref[...]
    o_ref[...] = (y * jax.nn.sigmoid(z)).astype(o_ref.dtype)


def _s5_glu(ys, proj, dskip, w, b, *, layer, u_col, tm, tc):
    _, n, d_ssm = ys.shape
    assert tm % tc == 0
    return pl.pallas_call(
        functools.partial(_glu_kernel, tc=tc),
        scratch_shapes=[pltpu.VMEM((d_ssm // min(d_ssm, LANES), tm, min(d_ssm, LANES)), F32)],
        grid=(n // tm,),
        in_specs=[
            pl.BlockSpec((2, tm, d_ssm), lambda i: (0, i, 0)),
            pl.BlockSpec((tm, d_ssm), lambda i: (i, u_col)),
            pl.BlockSpec((1, d_ssm), lambda i: (0, 0)),
            pl.BlockSpec((None, d_ssm, d_ssm), lambda i: (layer, 0, 0)),
            pl.BlockSpec((1, d_ssm), lambda i: (0, 0)),
        ],
        out_specs=pl.BlockSpec((tm, d_ssm), lambda i: (i, 0)),
        out_shape=jax.ShapeDtypeStruct((n, d_ssm), BF16),
        compiler_params=_params("s5_glu", ("parallel",)),
        name="s5_glu",
    )(ys, proj, dskip, w, b)


def _attn_kernel(q_ref, k_ref, v_ref, lam_ref, sg_ref, o_ref, *, lambda_init, head_dim, kc):
    q = q_ref[0]
    tq = q.shape[0]
    lane = lax.broadcasted_iota(jnp.int32, q.shape, 1)
    zero = jnp.zeros_like(q)
    qq = jnp.concatenate([jnp.where(lane < head_dim, q, zero), jnp.where(lane >= head_dim, q, zero)], axis=0)
    n_chunks = k_ref.shape[1] // kc
    sub = 2 * SUBLANES

    def scores(c):
        return lax.dot_general(k_ref[0, pl.ds(c * kc, kc), :], qq, (((1,), (1,)), ((), ())),
                               preferred_element_type=F32)

    def exp_stage(s, m_new):
        pieces, part = [], None
        for r in range(0, kc, sub):
            e_r = jnp.exp2(s[r:r + sub] - m_new)
            pieces.append(e_r.astype(BF16))
            for g in range(sub // SUBLANES):
                tile = e_r[g * SUBLANES:(g + 1) * SUBLANES]
                part = tile if part is None else part + tile
        return jnp.concatenate(pieces, axis=0), jnp.sum(part, axis=0, keepdims=True)

    def value_stage(acc, item):
        c, e_b, alpha = item
        pv = lax.dot_general(v_ref[0, pl.ds(c * kc, kc), :], e_b, (((0,), (0,)), ((), ())),
                             preferred_element_type=F32)
        return pv if acc is None else acc * alpha + pv

    score_lead = 2
    m = l = acc = None
    s_q = [scores(c) for c in range(min(score_lead, n_chunks))]
    pending = None
    for c in range(n_chunks):
        if c + score_lead < n_chunks:
            s_q.append(scores(c + score_lead))
        s = s_q[c]
        m_c = jnp.max(s, axis=0, keepdims=True)
        m_new = m_c if m is None else jnp.maximum(m, m_c)
        e_b, l_c = exp_stage(s, m_new)
        alpha = None if m is None else jnp.exp2(m - m_new)
        l = l_c if m is None else l * alpha + l_c
        if pending is not None:
            acc = value_stage(acc, pending)
        pending = (c, e_b, alpha)
        m = m_new
    acc = value_stage(acc, pending)
    lv = lam_ref[...]
    lam = (jnp.exp(jnp.sum(lv[0:1] * lv[1:2], axis=-1, keepdims=True))
           - jnp.exp(jnp.sum(lv[2:3] * lv[3:4], axis=-1, keepdims=True)) + lambda_init)
    o_t = acc[:, 0:tq] * (1.0 / l[:, 0:tq]) - acc[:, tq:2 * tq] * (lam / l[:, tq:2 * tq])
    o_ref[0] = (_rms(o_t.T, sg_ref[...]) * (1.0 - lambda_init)).astype(o_ref.dtype)


def _attention(proj3, lam_vec, subln, *, q_col, k_col, v_col, n_heads, head_dim, lambda_init, tq, kc):
    batch, seq_len, _ = proj3.shape
    vd = 2 * head_dim
    return pl.pallas_call(
        functools.partial(_attn_kernel, lambda_init=lambda_init, head_dim=head_dim, kc=kc),
        grid=(batch, n_heads, seq_len // tq),
        in_specs=[
            pl.BlockSpec((1, tq, vd), lambda b, h, i: (b, i, q_col + h)),
            pl.BlockSpec((1, seq_len, vd), lambda b, h, i: (b, 0, k_col + h)),
            pl.BlockSpec((1, seq_len, vd), lambda b, h, i: (b, 0, v_col + h)),
            pl.BlockSpec((4, head_dim), lambda b, h, i: (0, 0)),
            pl.BlockSpec((1, vd), lambda b, h, i: (0, 0)),
        ],
        out_specs=pl.BlockSpec((1, tq, vd), lambda b, h, i: (b, i, h)),
        out_shape=jax.ShapeDtypeStruct((batch, seq_len, n_heads * vd), BF16),
        compiler_params=_params("diff_attn", ("parallel", "parallel", "arbitrary")),
        name="diff_attn",
    )(proj3, proj3, proj3, lam_vec, subln)


def _conv_kernel(bg_ref, cg_ref, xv_ref, w_ref, o_ref):
    z = cg_ref[0].astype(F32) * xv_ref[0].astype(F32)
    seq_len = z.shape[0]
    row = lax.broadcasted_iota(jnp.int32, z.shape, 0)
    z_prev = jnp.where(row == 0, 0.0, pltpu.roll(z, 1, axis=0))
    z_next = jnp.where(row == seq_len - 1, 0.0, pltpu.roll(z, seq_len - 1, axis=0))
    w = w_ref[...]
    zc = w[0:1] * z_prev + w[1:2] * z + w[2:3] * z_next
    o_ref[0] = (bg_ref[0].astype(F32) * zc).astype(o_ref.dtype)


def _short_conv(proj3, conv_w, *, bg_col, d_conv):
    batch, seq_len, _ = proj3.shape
    tc = min(d_conv, LANES)
    per = d_conv // tc
    return pl.pallas_call(
        _conv_kernel,
        grid=(batch, per),
        in_specs=[
            pl.BlockSpec((1, seq_len, tc), lambda b, j: (b, 0, bg_col * per + j)),
            pl.BlockSpec((1, seq_len, tc), lambda b, j: (b, 0, (bg_col + 1) * per + j)),
            pl.BlockSpec((1, seq_len, tc), lambda b, j: (b, 0, (bg_col + 2) * per + j)),
            pl.BlockSpec((conv_w.shape[0], tc), lambda b, j: (0, j)),
        ],
        out_specs=pl.BlockSpec((1, seq_len, tc), lambda b, j: (b, 0, j)),
        out_shape=jax.ShapeDtypeStruct((batch, seq_len, d_conv), BF16),
        compiler_params=_params("short_conv", ("parallel", "parallel")),
        name="short_conv",
    )(proj3, proj3, proj3, conv_w)


def _merge_kernel(x_ref, ya_ref, yb_ref, yc_ref, ga_ref, gb_ref, gc_ref, wb_ref, wo_ref, o_ref):
    r_a = ya_ref.shape[1]
    r_b = r_a + yb_ref.shape[1]
    pa = jnp.dot(ya_ref[...], wb_ref[0:r_a, :], preferred_element_type=F32)
    merged = ga_ref[...].astype(F32) * pa
    pb = jnp.dot(yb_ref[...], wb_ref[r_a:r_b, :], preferred_element_type=F32)
    merged += gb_ref[...].astype(F32) * pb
    pc = jnp.dot(yc_ref[...], wb_ref[r_b:, :], preferred_element_type=F32)
    merged += gc_ref[...].astype(F32) * pc
    o_ref[...] = x_ref[...] + jnp.dot(merged.astype(BF16), wo_ref[...], preferred_element_type=F32)


def _merge(x, ya, yb, yc, proj, wb, wo, *, layer, tm):
    n, d = x.shape
    resident = lambda w: pl.BlockSpec((None,) + w.shape[1:], lambda i: (layer, 0, 0), pipeline_mode=pl.Buffered(1))
    row = lambda i: (i, 0)
    return pl.pallas_call(
        _merge_kernel,
        grid=(n // tm,),
        in_specs=[
            pl.BlockSpec((tm, d), row),
            pl.BlockSpec((tm, ya.shape[1]), row),
            pl.BlockSpec((tm, yb.shape[1]), row),
            pl.BlockSpec((tm, yc.shape[1]), row),
            pl.BlockSpec((tm, d), lambda i: (i, 0)),
            pl.BlockSpec((tm, d), lambda i: (i, 1)),
            pl.BlockSpec((tm, d), lambda i: (i, 2)),
            resident(wb),
            resident(wo),
        ],
        out_specs=pl.BlockSpec((tm, d), row),
        out_shape=jax.ShapeDtypeStruct((n, d), F32),
        compiler_params=_params("merge", ("parallel",)),
        name="merge",
    )(x, ya, yb, yc, proj, proj, proj, wb, wo)


def _rope_tables(seq_len, head_dim):
    pos = jnp.arange(seq_len, dtype=F32)
    inv = ROPE_THETA ** (-jnp.arange(0, head_dim, 2, dtype=F32) / head_dim)
    ang = pos[:, None] * inv[None, :]
    cos, sin = jnp.cos(ang), jnp.sin(ang)
    reps = LANES // head_dim
    return (jnp.tile(jnp.concatenate([cos, cos], axis=1), (1, reps)),
            jnp.tile(jnp.concatenate([-sin, sin], axis=1), (1, reps)))


def _block_diag(blocks):
    g, r, c = blocks.shape
    wide = jnp.transpose(blocks, (1, 0, 2)).reshape(r, g * c)
    row_g = jnp.arange(g * r)[:, None] // r
    col_g = jnp.arange(g * c)[None, :] // c
    return jnp.where(row_g == col_g, jnp.tile(wide, (g, 1)), 0.0)


def _s5_tables(lam_re, lam_im, log_dt, b_re, b_im, c_re, c_im, n_steps):
    bmat = jnp.concatenate([_block_diag(jnp.swapaxes(b_re, 1, 2)), _block_diag(jnp.swapaxes(b_im, 1, 2))], axis=1)
    cmats, tabs = [], []
    for d in range(2):
        lr, li = lam_re[d], lam_im[d]
        dt = jnp.exp(log_dt[d])[:, None]
        mag = jnp.exp(dt * lr)
        ar, ai = mag * jnp.cos(dt * li), mag * jnp.sin(dt * li)
        denom = lr * lr + li * li
        nr = ar - 1.0
        kr = (nr * lr + ai * li) / denom
        ki = (ai * lr - nr * li) / denom
        er = c_re * kr[:, None, :] - c_im * ki[:, None, :]
        ei = c_re * ki[:, None, :] + c_im * kr[:, None, :]
        cmats.append(jnp.concatenate([_block_diag(jnp.swapaxes(er, 1, 2)), _block_diag(jnp.swapaxes(-ei, 1, 2))],
                                     axis=0))
        ar, ai = ar.reshape(-1), ai.reshape(-1)
        pr, pi, sr, si, e = jnp.ones_like(ar), jnp.zeros_like(ai), ar, ai, n_steps
        while e:
            if e & 1:
                pr, pi = pr * sr - pi * si, pr * si + pi * sr
            sr, si = sr * sr - si * si, 2.0 * sr * si
            e >>= 1
        tabs.append(jnp.stack([jnp.broadcast_to(v[None, :], (SUBLANES, v.shape[0])) for v in (ar, ai, pr, pi)]))
    return bmat.astype(BF16), jnp.stack(tabs), jnp.stack(cmats).astype(BF16)


def kernel(x, norm_w, ffn_w13, ffn_w2, w_in, s5_lambda_re, s5_lambda_im, s5_log_dt, s5_b_re, s5_b_im, s5_c_re, s5_c_im, s5_d, s5_w_glu, s5_b_glu, diff_lambda, diff_subln, conv_w, w_branch, w_gate, b_gate, w_out, final_norm):
    batch, seq_len, d_model = x.shape
    depth = norm_w.shape[0]
    d_ff = ffn_w2.shape[2]
    d_ssm = s5_d.shape[-1]
    d_conv = conv_w.shape[-1]
    d_attn = (w_in.shape[-1] - d_ssm - 3 * d_conv) // 3
    head_dim = diff_lambda.shape[-1]
    n_heads = d_attn // (2 * head_dim)
    n = batch * seq_len
    assert 2 * head_dim == LANES and d_ssm == d_conv and d_attn == 2 * d_ssm

    tm_ffn = _tile(n, 1024)
    tf = min(2 * LANES, d_ff)
    tm_proj = _tile(seq_len, 1024)
    tn_proj = d_attn
    tm_merge = _tile(n, 512)
    tm_glu = _tile(seq_len, 1024)
    tq = _tile(seq_len, 1024)
    kc_attn = _tile(seq_len, 512)
    t_chunk = _tile(seq_len, 512)

    cos, sin = _rope_tables(seq_len, head_dim)
    xf = x.reshape(n, d_model)
    gate_cols = N_BRANCH * d_model
    u_col = gate_cols // d_ssm
    q_col = (gate_cols + d_ssm) // LANES
    k_col = q_col + d_attn // LANES
    v_col = k_col + d_attn // LANES
    bg_col = (gate_cols + d_ssm + 3 * d_attn) // d_conv
    b_gate3 = b_gate[:, None, :]
    bmat, tabs, cmat = jax.vmap(functools.partial(_s5_tables, n_steps=t_chunk // SUBLANES))(
        s5_lambda_re, s5_lambda_im, s5_log_dt, s5_b_re, s5_b_im, s5_c_re, s5_c_im)
    w_glu_bf, w_branch_bf, w_out_bf = s5_w_glu.astype(BF16), w_branch.astype(BF16), w_out.astype(BF16)

    for l in range(depth):
        lambda_init = 0.8 - 0.6 * math.exp(-0.3 * l)

        xf, h_mix = _ffn(xf, norm_w[l, 0][None, :], ffn_w13, ffn_w2, norm_w[l, 1][None, :], layer=l, idx=0,
                         post="next", tm=tm_ffn, tf=tf)

        proj = _proj(h_mix, w_gate, b_gate3, w_in, cos, sin, layer=l, seq_len=seq_len, u_cols=d_ssm,
                     attn_cols=d_attn, q_scale=head_dim ** -0.5 * math.log2(math.e), tm=tm_proj, tn=tn_proj)
        proj3 = proj.reshape(batch, seq_len, proj.shape[1])

        ys = _s5_scan(proj, bmat, tabs, cmat, layer=l, batch=batch, seq_len=seq_len, u_col=u_col, d_ssm=d_ssm,
                      tc=t_chunk)
        y_a = _s5_glu(ys, proj, s5_d[l][None, :], w_glu_bf, s5_b_glu[l][None, :], layer=l,
                      u_col=u_col, tm=tm_glu, tc=t_chunk)
        y_b = _attention(proj3, diff_lambda[l], diff_subln[l][None, :], q_col=q_col, k_col=k_col, v_col=v_col,
                         n_heads=n_heads, head_dim=head_dim, lambda_init=lambda_init, tq=tq, kc=kc_attn)
        y_c = _short_conv(proj3, conv_w[l], bg_col=bg_col, d_conv=d_conv)

        xf = _merge(xf, y_a, y_b.reshape(n, d_attn), y_c.reshape(n, d_conv), proj,
                    w_branch_bf, w_out_bf, layer=l, tm=tm_merge)

        xf = _ffn(xf, norm_w[l, 2][None, :], ffn_w13, ffn_w2, final_norm[None, :], layer=l, idx=1,
                  post="final" if l == depth - 1 else "none", tm=tm_ffn, tf=tf)
    return xf.reshape(batch, seq_len, d_model)
```

```python
import functools
import math

import jax
import jax.numpy as jnp
from jax import lax
from jax.experimental import pallas as pl
from jax.experimental.pallas import tpu as pltpu

NORM_EPS = 1e-6
ROPE_THETA = 10000.0
N_BRANCH = 3
LANES = 128
SUBLANES = 8
MIB = 1024 * 1024
F32 = jnp.float32
BF16 = jnp.bfloat16


def _rms(x, g):
    return x * lax.rsqrt(jnp.mean(x * x, axis=-1, keepdims=True) + NORM_EPS) * g


def _tile(n, want):
    t = min(n, want)
    while n % t:
        t -= 1
    return t


VMEM_LIMIT_MIB = {"ffn": 62, "proj": 56, "s5_scan": 52, "s5_glu": 40, "diff_attn": 52, "short_conv": 40,
                  "merge": 60}


def _params(name, sem):
    return pltpu.CompilerParams(dimension_semantics=sem, vmem_limit_bytes=VMEM_LIMIT_MIB[name] * MIB)


def _ff_tile_start(j, tf, d_ff):
    return jnp.minimum(j * (tf // LANES), (d_ff - tf) // LANES) * LANES


def _ffn_kernel(x_ref, g_ref, wa_ref, wb_ref, w2_ref, fg_ref, o_ref, *rest, post, d_ff):
    if post == "next":
        hn_ref, = rest
        h_ref = hn_ref
    else:
        h_ref, = rest
    j = pl.program_id(1)
    tf = wa_ref.shape[-1]

    @pl.when(j == 0)
    def _():
        h_ref[...] = _rms(x_ref[...], g_ref[...]).astype(BF16)
        o_ref[...] = jnp.zeros_like(o_ref)

    h = h_ref[...]
    a = jnp.dot(h, wa_ref[0, 0].astype(BF16), preferred_element_type=F32)
    b = jnp.dot(h, wb_ref[0, 0].astype(BF16), preferred_element_type=F32)
    col = _ff_tile_start(j, tf, d_ff) + lax.broadcasted_iota(jnp.int32, a.shape, 1)
    act = jnp.where(col >= j * tf, a * jax.nn.sigmoid(a) * b, 0.0).astype(BF16)
    o_ref[...] += jnp.dot(act, w2_ref[0, 0].astype(BF16), preferred_element_type=F32)

    @pl.when(j == pl.num_programs(1) - 1)
    def _():
        y = x_ref[...] + 0.5 * o_ref[...]
        if post == "final":
            y = _rms(y, fg_ref[...])
        if post == "next":
            hn_ref[...] = _rms(y, fg_ref[...]).astype(BF16)
        o_ref[...] = y


def _ffn(x, g, w13, w2, fg, *, layer, idx, post, tm, tf):
    n, d = x.shape
    row_blk = pl.BlockSpec((tm, d), lambda i, j: (i, 0))
    out_specs, out_shape = row_blk, jax.ShapeDtypeStruct((n, d), F32)
    if post == "next":
        out_specs, out_shape = [row_blk, row_blk], [out_shape, jax.ShapeDtypeStruct((n, d), BF16)]
    d_ff = w2.shape[2]
    assert d_ff % LANES == 0 and tf % LANES == 0 and tf <= d_ff
    start = lambda j: _ff_tile_start(j, tf, d_ff)
    elems = lambda *shape: tuple(pl.Element(s) for s in shape)

    return pl.pallas_call(
        functools.partial(_ffn_kernel, post=post, d_ff=d_ff),
        grid=(n // tm, pl.cdiv(d_ff, tf)),
        in_specs=[
            row_blk,
            pl.BlockSpec((1, d), lambda i, j: (0, 0)),
            pl.BlockSpec(elems(1, 1, d, tf), lambda i, j: (layer, idx, 0, start(j))),
            pl.BlockSpec(elems(1, 1, d, tf), lambda i, j: (layer, idx, 0, (d_ff // LANES + start(j) // LANES) * LANES)),
            pl.BlockSpec(elems(1, 1, tf, d), lambda i, j: (layer, idx, start(j), 0)),
            pl.BlockSpec((1, d), lambda i, j: (0, 0)),
        ],
        out_specs=out_specs,
        out_shape=out_shape,
        scratch_shapes=[] if post == "next" else [pltpu.VMEM((tm, d), BF16)],
        compiler_params=_params("ffn", ("parallel", "arbitrary")),
        name="ffn",
    )(x, g, w13, w13, w2, fg)


def _rope(acc, cos, sin, scale):
    lane = lax.broadcasted_iota(jnp.int32, cos.shape, 1)
    first_half = (lane & (LANES // 4)) == 0
    out = []
    for c in range(acc.shape[1] // LANES):
        blk = acc[:, c * LANES:(c + 1) * LANES]
        partner = jnp.where(first_half,
                            pltpu.roll(blk, LANES - LANES // 4, axis=1),
                            pltpu.roll(blk, LANES // 4, axis=1))
        out.append((blk * cos + partner * sin) * scale)
    return jnp.concatenate(out, axis=1)


def _proj_kernel(h_ref, wg_ref, wi_ref, b_ref, cos_ref, sin_ref, o_ref, *,
                 n_gate, n_in, u_cols, attn_cols, q_scale, chunk):
    j = pl.program_id(0)
    tn = o_ref.shape[1]

    def run(w_ref, epilogue):
        for c in range(tn // chunk):
            cols = slice(c * chunk, (c + 1) * chunk)
            acc = jnp.dot(h_ref[...], w_ref[:, cols].astype(BF16), preferred_element_type=F32)
            o_ref[:, cols] = epilogue(acc, c).astype(o_ref.dtype)

    @pl.when(j < n_gate)
    def _():
        run(wg_ref, lambda acc, c: jax.nn.sigmoid(acc + b_ref[:, c * chunk:(c + 1) * chunk]))

    def in_epilogue(blk):
        def epilogue(acc, c):
            col = blk * tn + c * chunk
            if u_cols <= col < u_cols + attn_cols:
                return _rope(acc, cos_ref[...], sin_ref[...], q_scale)
            if u_cols + attn_cols <= col < u_cols + 2 * attn_cols:
                return _rope(acc, cos_ref[...], sin_ref[...], 1.0)
            return acc
        return epilogue

    for blk in range(n_in):
        @pl.when(j == n_gate + blk)
        def _(blk=blk):
            run(wi_ref, in_epilogue(blk))


def _proj(h, w_gate, b_gate, w_in, cos, sin, *, layer, seq_len, u_cols, attn_cols, q_scale, tm, tn):
    n, d = h.shape
    n_gate = w_gate.shape[2] // tn
    n_in = w_in.shape[2] // tn
    chunk = min(u_cols, 2 * LANES)
    assert u_cols % chunk == 0 and attn_cols % chunk == 0 and tn % chunk == 0
    pos_blocks = seq_len // tm
    gate_blk = lambda j, i: (layer, 0, jnp.minimum(j, n_gate - 1))
    return pl.pallas_call(
        functools.partial(_proj_kernel, n_gate=n_gate, n_in=n_in, u_cols=u_cols, attn_cols=attn_cols,
                          q_scale=q_scale, chunk=chunk),
        grid=(n_gate + n_in, n // tm),
        in_specs=[
            pl.BlockSpec((tm, d), lambda j, i: (i, 0)),
            pl.BlockSpec((None, d, tn), gate_blk),
            pl.BlockSpec((None, d, tn), lambda j, i: (layer, 0, jnp.maximum(j - n_gate, 0))),
            pl.BlockSpec((None, 1, tn), gate_blk),
            pl.BlockSpec((tm, LANES), lambda j, i: (i % pos_blocks, 0)),
            pl.BlockSpec((tm, LANES), lambda j, i: (i % pos_blocks, 0)),
        ],
        out_specs=pl.BlockSpec((tm, tn), lambda j, i: (i, j)),
        out_shape=jax.ShapeDtypeStruct((n, (n_gate + n_in) * tn), BF16),
        compiler_params=_params("proj", ("arbitrary", "arbitrary")),
        name="proj",
    )(h, w_gate, w_in, b_gate, cos, sin)


def _scan_chunk(bu_ref, tab_ref, carry_ref, *, reverse, lane_group):
    t_len, s2 = bu_ref.shape
    s_dim = s2 // 2
    n_steps = t_len // SUBLANES
    first, last = (SUBLANES - 1, 0) if reverse else (0, SUBLANES - 1)
    toward_later = SUBLANES - 1 if reverse else 1
    for lg in range(s_dim // lane_group):
        re_cols = pl.ds(lg * lane_group, lane_group)
        im_cols = pl.ds(s_dim + lg * lane_group, lane_group)
        a_re, a_im = tab_ref[0, :, re_cols], tab_ref[1, :, re_cols]
        an_re, an_im = tab_ref[2, :, re_cols], tab_ref[3, :, re_cols]

        def rows_of(it):
            step = (n_steps - 1 - it) if reverse else it
            return pl.ds(pl.multiple_of(step * SUBLANES, SUBLANES), SUBLANES)

        def local_scan(it, state):
            s_re, s_im = state
            rows = rows_of(it)
            s_re, s_im = (a_re * s_re - a_im * s_im + bu_ref[rows, re_cols],
                          a_re * s_im + a_im * s_re + bu_ref[rows, im_cols])
            bu_ref[rows, re_cols] = s_re
            bu_ref[rows, im_cols] = s_im
            return s_re, s_im

        zero = jnp.zeros((SUBLANES, lane_group), F32)
        e_re, e_im = lax.fori_loop(0, n_steps, local_scan, (zero, zero), unroll=8)

        row = lax.broadcasted_iota(jnp.int32, (SUBLANES, lane_group), 0)
        c_re = jnp.where(row == first, carry_ref[0, :, re_cols], 0.0)
        c_im = jnp.where(row == first, carry_ref[1, :, re_cols], 0.0)

        def segment_end(c_re, c_im):
            return e_re + an_re * c_re - an_im * c_im, e_im + an_re * c_im + an_im * c_re

        order = range(SUBLANES - 2, -1, -1) if reverse else range(1, SUBLANES)
        for k in order:
            t_re, t_im = segment_end(c_re, c_im)
            c_re = jnp.where(row == k, pltpu.roll(t_re, toward_later, axis=0), c_re)
            c_im = jnp.where(row == k, pltpu.roll(t_im, toward_later, axis=0), c_im)
        t_re, t_im = segment_end(c_re, c_im)
        carry_ref[0, :, re_cols] = jnp.broadcast_to(t_re[last:last + 1], t_re.shape)
        carry_ref[1, :, re_cols] = jnp.broadcast_to(t_im[last:last + 1], t_im.shape)

        def add_incoming(it, f):
            f_re, f_im = f
            rows = rows_of(it)
            f_re, f_im = a_re * f_re - a_im * f_im, a_re * f_im + a_im * f_re
            bu_ref[rows, re_cols] += f_re
            bu_ref[rows, im_cols] += f_im
            return f_re, f_im

        lax.fori_loop(0, n_steps, add_incoming, (c_re, c_im), unroll=8)


def _s5_kernel(u_ref, perm_ref, bmat_ref, tab_ref, cmat_ref, y_ref, bu_ref, carry_ref, *, lane_group):
    d = pl.program_id(1)
    c = pl.program_id(2)

    @pl.when(c == 0)
    def _():
        carry_ref[...] = jnp.zeros_like(carry_ref)

    u_perm = jnp.dot(perm_ref[...], u_ref[...], preferred_element_type=F32).astype(BF16)

    d_ssm, s2 = bmat_ref.shape
    s_dim = s2 // 2
    halves = [(slice(h * d_ssm // 2, (h + 1) * d_ssm // 2), slice(h * s_dim // 2, (h + 1) * s_dim // 2),
               slice(s_dim + h * s_dim // 2, s_dim + (h + 1) * s_dim // 2)) for h in range(2)]
    for ch, st_re, st_im in halves:
        u_half = u_perm[:, ch]
        bu_ref[:, st_re] = jnp.dot(u_half, bmat_ref[ch, st_re], preferred_element_type=F32)
        bu_ref[:, st_im] = jnp.dot(u_half, bmat_ref[ch, st_im], preferred_element_type=F32)

    @pl.when(d == 0)
    def _():
        _scan_chunk(bu_ref, tab_ref.at[0], carry_ref, reverse=False, lane_group=lane_group)

    @pl.when(d == 1)
    def _():
        _scan_chunk(bu_ref, tab_ref.at[0], carry_ref, reverse=True, lane_group=lane_group)

    for ch, st_re, st_im in halves:
        y_ref[0, :, ch] = (
            jnp.dot(bu_ref[:, st_re].astype(BF16), cmat_ref[0, st_re, ch], preferred_element_type=F32)
            + jnp.dot(bu_ref[:, st_im].astype(BF16), cmat_ref[0, st_im, ch], preferred_element_type=F32))


def _segment_interleave(tc):
    dst = jnp.arange(tc)
    src = (dst % SUBLANES) * (tc // SUBLANES) + dst // SUBLANES
    return (src[:, None] == jnp.arange(tc)[None, :]).astype(BF16)


def _s5_scan(proj, bmat, tabs, cmat, *, layer, batch, seq_len, u_col, d_ssm, tc):
    n = proj.shape[0]
    n_chunks = seq_len // tc
    s2 = bmat.shape[2]
    s_dim = s2 // 2
    lane_group = min(s_dim, 8 * LANES)

    def chunk_row(b, d, c):
        return b * n_chunks + jnp.where(d == 0, c, n_chunks - 1 - c)

    return pl.pallas_call(
        functools.partial(_s5_kernel, lane_group=lane_group),
        grid=(batch, 2, n_chunks),
        in_specs=[
            pl.BlockSpec((tc, d_ssm), lambda b, d, c: (chunk_row(b, d, c), u_col)),
            pl.BlockSpec((tc, tc), lambda b, d, c: (0, 0)),
            pl.BlockSpec((None, d_ssm, s2), lambda b, d, c: (layer, 0, 0)),
            pl.BlockSpec((None, 1, 4, SUBLANES, s_dim), lambda b, d, c: (layer, d, 0, 0, 0)),
            pl.BlockSpec((None, 1, s2, d_ssm), lambda b, d, c: (layer, d, 0, 0)),
        ],
        out_specs=pl.BlockSpec((1, tc, d_ssm), lambda b, d, c: (d, chunk_row(b, d, c), 0)),
        out_shape=jax.ShapeDtypeStruct((2, n, d_ssm), F32),
        scratch_shapes=[pltpu.VMEM((tc, s2), F32), pltpu.VMEM((2, SUBLANES, s_dim), F32)],
        compiler_params=_params("s5_scan", ("parallel", "arbitrary", "arbitrary")),
        name="s5_scan",
    )(proj, _segment_interleave(tc), bmat, tabs, cmat)


def _glu_kernel(ys_ref, u_ref, dskip_ref, w_ref, b_ref, o_ref, y_ref, *, tc):
    o_ref[...] = _glu_value(ys_ref, u_ref, dskip_ref, w_ref, b_ref, y_ref, tc).astype(o_ref.dtype)


def _glu_value(ys_ref, u_ref, dskip_ref, w_ref, b_ref, y_ref, tc):
    n_lane_tiles, tm, _ = y_ref.shape
    y_sum = ys_ref[0] + ys_ref[1]
    for j in range(n_lane_tiles):
        y_ref[j] = y_sum[:, j * LANES:(j + 1) * LANES]
    n_steps = tc // SUBLANES
    pieces = [jnp.concatenate([y_ref[j, pl.ds(base + g, n_steps, stride=SUBLANES), :]
                               for j in range(n_lane_tiles)], axis=1)
              for base in range(0, tm, tc) for g in range(SUBLANES)]
    y = jnp.concatenate(pieces, axis=0) + dskip_ref[...] * u_ref[...].astype(F32)
    y = jax.nn.gelu(y)
    z = jnp.dot(y.astype(BF16), w_ref[...], preferred_element_type=F32) + b_ref[...]
    return y * jax.nn.sigmoid(z)


def _s5_glu(ys, proj, dskip, w, b, *, layer, u_col, tm, tc):
    _, n, d_ssm = ys.shape
    assert tm % tc == 0
    return pl.pallas_call(
        functools.partial(_glu_kernel, tc=tc),
        scratch_shapes=[pltpu.VMEM((d_ssm // min(d_ssm, LANES), tm, min(d_ssm, LANES)), F32)],
        grid=(n // tm,),
        in_specs=[
            pl.BlockSpec((2, tm, d_ssm), lambda i: (0, i, 0)),
            pl.BlockSpec((tm, d_ssm), lambda i: (i, u_col)),
            pl.BlockSpec((1, d_ssm), lambda i: (0, 0)),
            pl.BlockSpec((None, d_ssm, d_ssm), lambda i: (layer, 0, 0)),
            pl.BlockSpec((1, d_ssm), lambda i: (0, 0)),
        ],
        out_specs=pl.BlockSpec((tm, d_ssm), lambda i: (i, 0)),
        out_shape=jax.ShapeDtypeStruct((n, d_ssm), BF16),
        compiler_params=_params("s5_glu", ("parallel",)),
        name="s5_glu",
    )(ys, proj, dskip, w, b)


def _attn_kernel(q_ref, k_ref, v_ref, lam_ref, sg_ref, o_ref, *, lambda_init, head_dim, kc):
    q = q_ref[0]
    tq = q.shape[0]
    lane = lax.broadcasted_iota(jnp.int32, q.shape, 1)
    zero = jnp.zeros_like(q)
    qq = jnp.concatenate([jnp.where(lane < head_dim, q, zero), jnp.where(lane >= head_dim, q, zero)], axis=0)
    n_chunks = k_ref.shape[1] // kc
    sub = 2 * SUBLANES

    def scores(c):
        return lax.dot_general(k_ref[0, pl.ds(c * kc, kc), :], qq, (((1,), (1,)), ((), ())),
                               preferred_element_type=F32)

    def exp_stage(s, m_new):
        pieces, part = [], None
        for r in range(0, kc, sub):
            e_r = jnp.exp2(s[r:r + sub] - m_new)
            pieces.append(e_r.astype(BF16))
            for g in range(sub // SUBLANES):
                tile = e_r[g * SUBLANES:(g + 1) * SUBLANES]
                part = tile if part is None else part + tile
        return jnp.concatenate(pieces, axis=0), jnp.sum(part, axis=0, keepdims=True)

    def value_stage(acc, item):
        c, e_b, alpha = item
        pv = lax.dot_general(v_ref[0, pl.ds(c * kc, kc), :], e_b, (((0,), (0,)), ((), ())),
                             preferred_element_type=F32)
        return pv if acc is None else acc * alpha + pv

    score_lead = 2
    m = l = acc = None
    s_q = [scores(c) for c in range(min(score_lead, n_chunks))]
    pending = None
    for c in range(n_chunks):
        if c + score_lead < n_chunks:
            s_q.append(scores(c + score_lead))
        s = s_q[c]
        m_c = jnp.max(s, axis=0, keepdims=True)
        m_new = m_c if m is None else jnp.maximum(m, m_c)
        e_b, l_c = exp_stage(s, m_new)
        alpha = None if m is None else jnp.exp2(m - m_new)
        l = l_c if m is None else l * alpha + l_c
        if pending is not None:
            acc = value_stage(acc, pending)
        pending = (c, e_b, alpha)
        m = m_new
    acc = value_stage(acc, pending)
    lv = lam_ref[...]
    lam = (jnp.exp(jnp.sum(lv[0:1] * lv[1:2], axis=-1, keepdims=True))
           - jnp.exp(jnp.sum(lv[2:3] * lv[3:4], axis=-1, keepdims=True)) + lambda_init)
    o_t = acc[:, 0:tq] * (1.0 / l[:, 0:tq]) - acc[:, tq:2 * tq] * (lam / l[:, tq:2 * tq])
    o_ref[0] = (_rms(o_t.T, sg_ref[...]) * (1.0 - lambda_init)).astype(o_ref.dtype)


def _attention(proj3, lam_vec, subln, *, q_col, k_col, v_col, n_heads, head_dim, lambda_init, tq, kc):
    batch, seq_len, _ = proj3.shape
    vd = 2 * head_dim
    return pl.pallas_call(
        functools.partial(_attn_kernel, lambda_init=lambda_init, head_dim=head_dim, kc=kc),
        grid=(batch, n_heads, seq_len // tq),
        in_specs=[
            pl.BlockSpec((1, tq, vd), lambda b, h, i: (b, i, q_col + h)),
            pl.BlockSpec((1, seq_len, vd), lambda b, h, i: (b, 0, k_col + h)),
            pl.BlockSpec((1, seq_len, vd), lambda b, h, i: (b, 0, v_col + h)),
            pl.BlockSpec((4, head_dim), lambda b, h, i: (0, 0)),
            pl.BlockSpec((1, vd), lambda b, h, i: (0, 0)),
        ],
        out_specs=pl.BlockSpec((1, tq, vd), lambda b, h, i: (b, i, h)),
        out_shape=jax.ShapeDtypeStruct((batch, seq_len, n_heads * vd), BF16),
        compiler_params=_params("diff_attn", ("parallel", "parallel", "arbitrary")),
        name="diff_attn",
    )(proj3, proj3, proj3, lam_vec, subln)


def _conv_kernel(bg_ref, cg_ref, xv_ref, w_ref, o_ref):
    z = cg_ref[0].astype(F32) * xv_ref[0].astype(F32)
    seq_len = z.shape[0]
    row = lax.broadcasted_iota(jnp.int32, z.shape, 0)
    z_prev = jnp.where(row == 0, 0.0, pltpu.roll(z, 1, axis=0))
    z_next = jnp.where(row == seq_len - 1, 0.0, pltpu.roll(z, seq_len - 1, axis=0))
    w = w_ref[...]
    zc = w[0:1] * z_prev + w[1:2] * z + w[2:3] * z_next
    o_ref[0] = (bg_ref[0].astype(F32) * zc).astype(o_ref.dtype)


def _short_conv(proj3, conv_w, *, bg_col, d_conv):
    batch, seq_len, _ = proj3.shape
    tc = min(d_conv, LANES)
    per = d_conv // tc
    return pl.pallas_call(
        _conv_kernel,
        grid=(batch, per),
        in_specs=[
            pl.BlockSpec((1, seq_len, tc), lambda b, j: (b, 0, bg_col * per + j)),
            pl.BlockSpec((1, seq_len, tc), lambda b, j: (b, 0, (bg_col + 1) * per + j)),
            pl.BlockSpec((1, seq_len, tc), lambda b, j: (b, 0, (bg_col + 2) * per + j)),
            pl.BlockSpec((conv_w.shape[0], tc), lambda b, j: (0, j)),
        ],
        out_specs=pl.BlockSpec((1, seq_len, tc), lambda b, j: (b, 0, j)),
        out_shape=jax.ShapeDtypeStruct((batch, seq_len, d_conv), BF16),
        compiler_params=_params("short_conv", ("parallel", "parallel")),
        name="short_conv",
    )(proj3, proj3, proj3, conv_w)


def _merge_kernel(x_ref, ys_ref, u_ref, dskip_ref, wg_ref, bgl_ref, yb_ref, yc_ref, ga_ref, gb_ref, gc_ref,
                  wb_ref, wo_ref, o_ref, y_scr, *, tc):
    ya = _glu_value(ys_ref, u_ref, dskip_ref, wg_ref, bgl_ref, y_scr, tc).astype(BF16)
    r_a = ya.shape[1]
    r_b = r_a + yb_ref.shape[1]
    pa = jnp.dot(ya, wb_ref[0:r_a, :], preferred_element_type=F32)
    merged = ga_ref[...].astype(F32) * pa
    pb = jnp.dot(yb_ref[...], wb_ref[r_a:r_b, :], preferred_element_type=F32)
    merged += gb_ref[...].astype(F32) * pb
    pc = jnp.dot(yc_ref[...], wb_ref[r_b:, :], preferred_element_type=F32)
    merged += gc_ref[...].astype(F32) * pc
    o_ref[...] = x_ref[...] + jnp.dot(merged.astype(BF16), wo_ref[...], preferred_element_type=F32)


def _merge(x, ys, dskip, wglu, bglu, yb, yc, proj, wb, wo, *, layer, u_col, tc, tm):
    n, d = x.shape
    d_ssm = ys.shape[2]
    assert tm % tc == 0
    resident = lambda w: pl.BlockSpec((None,) + w.shape[1:], lambda i: (layer, 0, 0), pipeline_mode=pl.Buffered(1))
    row = lambda i: (i, 0)
    return pl.pallas_call(
        functools.partial(_merge_kernel, tc=tc),
        scratch_shapes=[pltpu.VMEM((d_ssm // min(d_ssm, LANES), tm, min(d_ssm, LANES)), F32)],
        grid=(n // tm,),
        in_specs=[
            pl.BlockSpec((tm, d), row),
            pl.BlockSpec((2, tm, d_ssm), lambda i: (0, i, 0)),
            pl.BlockSpec((tm, d_ssm), lambda i: (i, u_col)),
            pl.BlockSpec((1, d_ssm), lambda i: (0, 0)),
            resident(wglu),
            pl.BlockSpec((1, d_ssm), lambda i: (0, 0)),
            pl.BlockSpec((tm, yb.shape[1]), row),
            pl.BlockSpec((tm, yc.shape[1]), row),
            pl.BlockSpec((tm, d), lambda i: (i, 0)),
            pl.BlockSpec((tm, d), lambda i: (i, 1)),
            pl.BlockSpec((tm, d), lambda i: (i, 2)),
            resident(wb),
            resident(wo),
        ],
        out_specs=pl.BlockSpec((tm, d), row),
        out_shape=jax.ShapeDtypeStruct((n, d), F32),
        compiler_params=_params("merge", ("parallel",)),
        name="merge",
    )(x, ys, proj, dskip, wglu, bglu, yb, yc, proj, proj, proj, wb, wo)


def _rope_tables(seq_len, head_dim):
    pos = jnp.arange(seq_len, dtype=F32)
    inv = ROPE_THETA ** (-jnp.arange(0, head_dim, 2, dtype=F32) / head_dim)
    ang = pos[:, None] * inv[None, :]
    cos, sin = jnp.cos(ang), jnp.sin(ang)
    reps = LANES // head_dim
    return (jnp.tile(jnp.concatenate([cos, cos], axis=1), (1, reps)),
            jnp.tile(jnp.concatenate([-sin, sin], axis=1), (1, reps)))


def _block_diag(blocks):
    g, r, c = blocks.shape
    wide = jnp.transpose(blocks, (1, 0, 2)).reshape(r, g * c)
    row_g = jnp.arange(g * r)[:, None] // r
    col_g = jnp.arange(g * c)[None, :] // c
    return jnp.where(row_g == col_g, jnp.tile(wide, (g, 1)), 0.0)


def _s5_tables(lam_re, lam_im, log_dt, b_re, b_im, c_re, c_im, n_steps):
    bmat = jnp.concatenate([_block_diag(jnp.swapaxes(b_re, 1, 2)), _block_diag(jnp.swapaxes(b_im, 1, 2))], axis=1)
    cmats, tabs = [], []
    for d in range(2):
        lr, li = lam_re[d], lam_im[d]
        dt = jnp.exp(log_dt[d])[:, None]
        mag = jnp.exp(dt * lr)
        ar, ai = mag * jnp.cos(dt * li), mag * jnp.sin(dt * li)
        denom = lr * lr + li * li
        nr = ar - 1.0
        kr = (nr * lr + ai * li) / denom
        ki = (ai * lr - nr * li) / denom
        er = c_re * kr[:, None, :] - c_im * ki[:, None, :]
        ei = c_re * ki[:, None, :] + c_im * kr[:, None, :]
        cmats.append(jnp.concatenate([_block_diag(jnp.swapaxes(er, 1, 2)), _block_diag(jnp.swapaxes(-ei, 1, 2))],
                                     axis=0))
        ar, ai = ar.reshape(-1), ai.reshape(-1)
        pr, pi, sr, si, e = jnp.ones_like(ar), jnp.zeros_like(ai), ar, ai, n_steps
        while e:
            if e & 1:
                pr, pi = pr * sr - pi * si, pr * si + pi * sr
            sr, si = sr * sr - si * si, 2.0 * sr * si
            e >>= 1
        tabs.append(jnp.stack([jnp.broadcast_to(v[None, :], (SUBLANES, v.shape[0])) for v in (ar, ai, pr, pi)]))
    return bmat.astype(BF16), jnp.stack(tabs), jnp.stack(cmats).astype(BF16)


def kernel(x, norm_w, ffn_w13, ffn_w2, w_in, s5_lambda_re, s5_lambda_im, s5_log_dt, s5_b_re, s5_b_im, s5_c_re, s5_c_im, s5_d, s5_w_glu, s5_b_glu, diff_lambda, diff_subln, conv_w, w_branch, w_gate, b_gate, w_out, final_norm):
    batch, seq_len, d_model = x.shape
    depth = norm_w.shape[0]
    d_ff = ffn_w2.shape[2]
    d_ssm = s5_d.shape[-1]
    d_conv = conv_w.shape[-1]
    d_attn = (w_in.shape[-1] - d_ssm - 3 * d_conv) // 3
    head_dim = diff_lambda.shape[-1]
    n_heads = d_attn // (2 * head_dim)
    n = batch * seq_len
    assert 2 * head_dim == LANES and d_ssm == d_conv and d_attn == 2 * d_ssm

    tm_ffn = _tile(n, 1024)
    tf = min(2 * LANES, d_ff)
    tm_proj = _tile(seq_len, 1024)
    tn_proj = d_attn
    tm_merge = _tile(n, 512)
    tm_glu = _tile(seq_len, 1024)
    tq = _tile(seq_len, 1024)
    kc_attn = _tile(seq_len, 512)
    t_chunk = _tile(seq_len, 512)

    cos, sin = _rope_tables(seq_len, head_dim)
    xf = x.reshape(n, d_model)
    gate_cols = N_BRANCH * d_model
    u_col = gate_cols // d_ssm
    q_col = (gate_cols + d_ssm) // LANES
    k_col = q_col + d_attn // LANES
    v_col = k_col + d_attn // LANES
    bg_col = (gate_cols + d_ssm + 3 * d_attn) // d_conv
    b_gate3 = b_gate[:, None, :]
    bmat, tabs, cmat = jax.vmap(functools.partial(_s5_tables, n_steps=t_chunk // SUBLANES))(
        s5_lambda_re, s5_lambda_im, s5_log_dt, s5_b_re, s5_b_im, s5_c_re, s5_c_im)
    w_glu_bf, w_branch_bf, w_out_bf = s5_w_glu.astype(BF16), w_branch.astype(BF16), w_out.astype(BF16)

    for l in range(depth):
        lambda_init = 0.8 - 0.6 * math.exp(-0.3 * l)

        xf, h_mix = _ffn(xf, norm_w[l, 0][None, :], ffn_w13, ffn_w2, norm_w[l, 1][None, :], layer=l, idx=0,
                         post="next", tm=tm_ffn, tf=tf)

        proj = _proj(h_mix, w_gate, b_gate3, w_in, cos, sin, layer=l, seq_len=seq_len, u_cols=d_ssm,
                     attn_cols=d_attn, q_scale=head_dim ** -0.5 * math.log2(math.e), tm=tm_proj, tn=tn_proj)
        proj3 = proj.reshape(batch, seq_len, proj.shape[1])

        ys = _s5_scan(proj, bmat, tabs, cmat, layer=l, batch=batch, seq_len=seq_len, u_col=u_col, d_ssm=d_ssm,
                      tc=t_chunk)
        y_b = _attention(proj3, diff_lambda[l], diff_subln[l][None, :], q_col=q_col, k_col=k_col, v_col=v_col,
                         n_heads=n_heads, head_dim=head_dim, lambda_init=lambda_init, tq=tq, kc=kc_attn)
        y_c = _short_conv(proj3, conv_w[l], bg_col=bg_col, d_conv=d_conv)

        xf = _merge(xf, ys, s5_d[l][None, :], w_glu_bf, s5_b_glu[l][None, :], y_b.reshape(n, d_attn),
                    y_c.reshape(n, d_conv), proj, w_branch_bf, w_out_bf, layer=l, u_col=u_col, tc=t_chunk,
                    tm=tm_merge)

        xf = _ffn(xf, norm_w[l, 2][None, :], ffn_w13, ffn_w2, final_norm[None, :], layer=l, idx=1,
                  post="final" if l == depth - 1 else "none", tm=tm_ffn, tf=tf)
    return xf.reshape(batch, seq_len, d_model)
```
